```python
import jax, jax.numpy as jnp
from jax import lax
import numpy as np

D_MODEL = 2048
BATCH = 16
SEQ = 2048
DEPTH = 1

HEAD_DIM = 128
N_FOX_HEADS = 8
N_MOBA_HEADS = 8
FOX_WIDTH = N_FOX_HEADS * HEAD_DIM
MOBA_WIDTH = N_MOBA_HEADS * HEAD_DIM
Q_BLOCK = 128
MOBA_BLOCK = 256
MOBA_TOPK = 3
ROPE_THETA = 500000.0
ROPE_DIM = HEAD_DIM // 4
N_EXPERTS = 32
TOP_K = 4
D_EXPERT = D_MODEL
SWIGLU_ALPHA = 1.702
SWIGLU_LIMIT = 7.0
MOE_ROW_BLOCK = 512
RMS_EPS = 1e-5
IN_SPLITS = [FOX_WIDTH, FOX_WIDTH, FOX_WIDTH, N_FOX_HEADS, MOBA_WIDTH, MOBA_WIDTH, MOBA_WIDTH, D_MODEL, D_MODEL]
IN_COLS = sum(IN_SPLITS)

kernel_name = "fox_moba_gated_hybrid_moe"


def rms_norm(x, g):
    xf = x.astype(jnp.float32)
    y = xf * lax.rsqrt(jnp.mean(xf * xf, axis=-1, keepdims=True) + RMS_EPS)
    return (y * g.astype(jnp.float32)).astype(x.dtype)


def partial_rotary(t, pos):
    half = ROPE_DIM // 2
    inv_freq = 1.0 / (ROPE_THETA ** (jnp.arange(half, dtype=jnp.float32) / half))
    ang = pos.astype(jnp.float32)[:, None] * inv_freq[None, :]
    cos, sin = jnp.cos(ang), jnp.sin(ang)
    tr = t[..., :ROPE_DIM].astype(jnp.float32)
    t1, t2 = tr[..., :half], tr[..., half:]
    rot = jnp.concatenate([t1 * cos - t2 * sin, t2 * cos + t1 * sin], axis=-1).astype(t.dtype)
    return jnp.concatenate([rot, t[..., ROPE_DIM:]], axis=-1)


def forgetting_attention(q, k, v, log_f):
    B, H, S, hd = q.shape
    nq = S // Q_BLOCK
    scale = hd ** -0.5
    c = jnp.cumsum(log_f, axis=-1)
    pos = jnp.arange(S)
    qb = q.reshape(B, H, nq, Q_BLOCK, hd).transpose(2, 0, 1, 3, 4)
    cb = c.reshape(B, H, nq, Q_BLOCK).transpose(2, 0, 1, 3)
    pb = pos.reshape(nq, Q_BLOCK)

    def query_block(args):
        qi, ci, pi = args
        s = jnp.einsum('bhqd,bhkd->bhqk', qi, k).astype(jnp.float32) * scale
        s = s + ci[..., None] - c[..., None, :]
        s = jnp.where(pi[:, None] >= pos[None, :], s, -jnp.inf)
        p = jax.nn.softmax(s, axis=-1).astype(v.dtype)
        return jnp.einsum('bhqk,bhkd->bhqd', p, v)

    o = lax.map(query_block, (qb, cb, pb))
    return o.transpose(1, 2, 0, 3, 4).reshape(B, H, S, hd)


def moba_attention(q, k, v):
    B, H, S, hd = q.shape
    scale = hd ** -0.5
    nblk = -(-S // MOBA_BLOCK)
    pad = ((0, 0), (0, 0), (0, nblk * MOBA_BLOCK - S), (0, 0))
    kb = jnp.pad(k, pad).reshape(B, H, nblk, MOBA_BLOCK, hd)
    vb = jnp.pad(v, pad).reshape(B, H, nblk, MOBA_BLOCK, hd)
    kmean = jnp.mean(kb.astype(jnp.float32), axis=3)
    n_sel = min(MOBA_TOPK, nblk)
    n_past = n_sel * MOBA_BLOCK
    nq = S // Q_BLOCK
    qb = q.reshape(B, H, nq, Q_BLOCK, hd).transpose(2, 0, 1, 3, 4).reshape(nq * B, H, Q_BLOCK, hd)
    blk_ids = jnp.arange(nblk)
    h_ids = jnp.arange(H)[:, None, None]

    def step(args):
        qi, n = args
        ci = n // B
        bi = n % B
        q_start = ci * Q_BLOCK
        own = q_start // MOBA_BLOCK
        qpos = q_start + jnp.arange(Q_BLOCK)
        kb_b, vb_b, km_b = kb[bi], vb[bi], kmean[bi]
        gate = jnp.einsum('hqd,hnd->hqn', qi.astype(jnp.float32), km_b)
        gate = jnp.where((blk_ids < own)[None, None, :], gate, -jnp.inf)
        _, sel = lax.top_k(gate, n_sel)
        sel_ok = sel < own
        k_sel = kb_b[h_ids, sel]
        v_sel = vb_b[h_ids, sel]
        s_past = jnp.einsum('hqd,hqnkd->hqnk', qi, k_sel).astype(jnp.float32) * scale
        s_past = jnp.where(sel_ok[..., None], s_past, -jnp.inf).reshape(H, Q_BLOCK, n_past)
        k_own = lax.dynamic_index_in_dim(kb_b, own, axis=1, keepdims=False)
        v_own = lax.dynamic_index_in_dim(vb_b, own, axis=1, keepdims=False)
        kpos = own * MOBA_BLOCK + jnp.arange(MOBA_BLOCK)
        s_own = jnp.einsum('hqd,hkd->hqk', qi, k_own).astype(jnp.float32) * scale
        s_own = jnp.where(qpos[:, None] >= kpos[None, :], s_own, -jnp.inf)
        p = jax.nn.softmax(jnp.concatenate([s_past, s_own], axis=-1), axis=-1).astype(v.dtype)
        p_past = p[..., :n_past].reshape(H, Q_BLOCK, n_sel, MOBA_BLOCK)
        return (jnp.einsum('hqnk,hqnkd->hqd', p_past, v_sel)
                + jnp.einsum('hqk,hkd->hqd', p[..., n_past:], v_own))

    o = lax.map(step, (qb, jnp.arange(nq * B)))
    return o.reshape(nq, B, H, Q_BLOCK, hd).transpose(1, 2, 0, 3, 4).reshape(B, H, S, hd)


def moe_ffn(h, w_router, b_router, w1, b1, w2, b2):
    B, S, D = h.shape
    T = B * S
    M = MOE_ROW_BLOCK
    ht = h.reshape(T, D)
    logits = (ht @ w_router).astype(jnp.float32) + b_router.astype(jnp.float32)
    top_vals, top_idx = lax.top_k(logits, TOP_K)
    gates = jax.nn.softmax(top_vals, axis=-1).astype(h.dtype)
    A = T * TOP_K
    flat_e = top_idx.reshape(A).astype(jnp.int32)
    flat_tok = jnp.arange(A, dtype=jnp.int32) // TOP_K
    flat_w = gates.reshape(A)
    order = jnp.argsort(flat_e)
    sorted_e = flat_e[order]
    counts = jnp.bincount(flat_e, length=N_EXPERTS).astype(jnp.int32)
    padded = (counts + M - 1) // M * M
    start = jnp.cumsum(counts) - counts
    pend = jnp.cumsum(padded)
    pstart = pend - padded
    dest = pstart[sorted_e] + jnp.arange(A, dtype=jnp.int32) - start[sorted_e]
    n_blocks = (A + N_EXPERTS * (M - 1) + M - 1) // M
    rows = n_blocks * M
    row_tok = jnp.full((rows,), T, jnp.int32).at[dest].set(flat_tok[order])
    row_w = jnp.zeros((rows,), h.dtype).at[dest].set(flat_w[order])
    block_e = jnp.minimum(jnp.searchsorted(pend, jnp.arange(n_blocks, dtype=jnp.int32) * M, side='right'),
                          N_EXPERTS - 1)
    h_pad = jnp.concatenate([ht, jnp.zeros((1, D), ht.dtype)], axis=0)

    def expert_block(args):
        tok, wgt, e = args
        xb = h_pad[tok]
        u = xb @ w1[e] + b1[e]
        glu = jnp.minimum(u[:, ::2], SWIGLU_LIMIT)
        lin = jnp.clip(u[:, 1::2], -SWIGLU_LIMIT, SWIGLU_LIMIT)
        a = glu * jax.nn.sigmoid(SWIGLU_ALPHA * glu) * (lin + 1.0)
        return (a @ w2[e] + b2[e]) * wgt[:, None]

    y = lax.map(expert_block, (row_tok.reshape(n_blocks, M), row_w.reshape(n_blocks, M), block_e))
    out = jax.ops.segment_sum(y.reshape(rows, D), row_tok, num_segments=T + 1)[:T]
    return out.reshape(B, S, D)


def hybrid_layer(x, norm_mix_g, w_in, b_forget, w_branch_fox, w_branch_moba, w_out, norm_ffn_g,
                 w_router, b_router, w_mlp1, b_mlp1, w_mlp2, b_mlp2):
    B, S, D = x.shape
    h = rms_norm(x, norm_mix_g)
    proj = jnp.einsum('bsd,de->bse', h, w_in)
    qf, kf, vf, fl, qm, km, vm, ga, gb = jnp.split(proj, list(np.cumsum(IN_SPLITS)[:-1]), axis=-1)

    def heads(t, n):
        return t.reshape(B, S, n, HEAD_DIM).transpose(0, 2, 1, 3)

    def merge(t):
        return t.transpose(0, 2, 1, 3).reshape(B, S, -1)

    pos = jnp.arange(S)
    log_f = jax.nn.log_sigmoid(fl.astype(jnp.float32) + b_forget.astype(jnp.float32)).transpose(0, 2, 1)
    o_fox = forgetting_attention(heads(qf, N_FOX_HEADS), heads(kf, N_FOX_HEADS), heads(vf, N_FOX_HEADS), log_f)
    o_moba = moba_attention(partial_rotary(heads(qm, N_MOBA_HEADS), pos),
                            partial_rotary(heads(km, N_MOBA_HEADS), pos),
                            heads(vm, N_MOBA_HEADS))
    y_fox = merge(o_fox) @ w_branch_fox
    y_moba = merge(o_moba) @ w_branch_moba
    mixed = jax.nn.sigmoid(ga) * y_fox + jax.nn.sigmoid(gb) * y_moba
    x = x + mixed @ w_out
    x = x + moe_ffn(rms_norm(x, norm_ffn_g), w_router, b_router, w_mlp1, b_mlp1, w_mlp2, b_mlp2)
    return x


def setup_inputs(seed: int = 0) -> dict:
    key = jax.random.key(seed)
    ks = jax.random.split(key, 16)
    f32 = jnp.float32

    def nrm(k, shape, scale):
        return jax.random.normal(k, shape, f32) * scale

    return {
        'x': nrm(ks[0], (BATCH, SEQ, D_MODEL), 1.0),
        'norm_mix_g': 1.0 + nrm(ks[1], (DEPTH, D_MODEL), 0.02),
        'w_in': nrm(ks[2], (DEPTH, D_MODEL, IN_COLS), D_MODEL ** -0.5),
        'b_forget': jax.random.uniform(ks[3], (DEPTH, N_FOX_HEADS), f32, 1.0, 4.0),
        'w_branch_fox': nrm(ks[4], (DEPTH, FOX_WIDTH, D_MODEL), FOX_WIDTH ** -0.5),
        'w_branch_moba': nrm(ks[5], (DEPTH, MOBA_WIDTH, D_MODEL), MOBA_WIDTH ** -0.5),
        'w_out': nrm(ks[6], (DEPTH, D_MODEL, D_MODEL), D_MODEL ** -0.5),
        'norm_ffn_g': 1.0 + nrm(ks[7], (DEPTH, D_MODEL), 0.02),
        'w_router': nrm(ks[8], (DEPTH, D_MODEL, N_EXPERTS), D_MODEL ** -0.5),
        'b_router': nrm(ks[9], (DEPTH, N_EXPERTS), 0.01),
        'w_mlp1': nrm(ks[10], (DEPTH, N_EXPERTS, D_MODEL, 2 * D_EXPERT), D_MODEL ** -0.5),
        'b_mlp1': nrm(ks[11], (DEPTH, N_EXPERTS, 2 * D_EXPERT), 0.02),
        'w_mlp2': nrm(ks[12], (DEPTH, N_EXPERTS, D_EXPERT, D_MODEL), D_EXPERT ** -0.5),
        'b_mlp2': nrm(ks[13], (DEPTH, N_EXPERTS, D_MODEL), 0.02),
        'norm_final_g': 1.0 + nrm(ks[14], (D_MODEL,), 0.02),
    }


def reference(x, norm_mix_g, w_in, b_forget, w_branch_fox, w_branch_moba, w_out, norm_ffn_g,
              w_router, b_router, w_mlp1, b_mlp1, w_mlp2, b_mlp2, norm_final_g):
    for l in range(DEPTH):
        x = hybrid_layer(x, norm_mix_g[l], w_in[l], b_forget[l], w_branch_fox[l], w_branch_moba[l], w_out[l],
                         norm_ffn_g[l], w_router[l], b_router[l], w_mlp1[l], b_mlp1[l], w_mlp2[l], b_mlp2[l])
    return rms_norm(x, norm_final_g)
```

```python
import functools

import jax
import jax.numpy as jnp
import numpy as np
from jax import lax
from jax.experimental import pallas as pl
from jax.experimental.pallas import tpu as pltpu

HEAD_DIM = 128
N_HEADS = 8
MOBA_BLOCK = 256
MOBA_TOPK = 3
ROPE_THETA = 500000.0
ROPE_DIM = HEAD_DIM // 4
N_EXPERTS = 32
TOP_K = 4
SWIGLU_ALPHA = 1.702
SWIGLU_LIMIT = 7.0
RMS_EPS = 1e-5

LANES = 128
VMEM_LIMIT_BYTES = 56 * 1024 * 1024
MOE_ROWS = 512

F32 = jnp.float32
BF16 = jnp.bfloat16
NEG_INF = float("-inf")


def _params(sem, **kw):
    return pltpu.CompilerParams(dimension_semantics=sem, vmem_limit_bytes=VMEM_LIMIT_BYTES, **kw)


def _rmsnorm_kernel(x_ref, g_ref, o_ref):
    x = x_ref[...]
    ms = jnp.mean(x * x, axis=-1, keepdims=True)
    o_ref[...] = (x * lax.rsqrt(ms + RMS_EPS) * g_ref[...]).astype(o_ref.dtype)


def _rmsnorm(x, g, tm):
    t, d = x.shape
    return pl.pallas_call(
        _rmsnorm_kernel,
        out_shape=jax.ShapeDtypeStruct((t, d), BF16),
        grid=(t // tm,),
        in_specs=[pl.BlockSpec((tm, d), lambda i: (i, 0)), pl.BlockSpec((1, d), lambda i: (0, 0))],
        out_specs=pl.BlockSpec((tm, d), lambda i: (i, 0)),
        compiler_params=_params(("parallel",)),
        name="rmsnorm_mix",
    )(x, g.reshape(1, d))


def _proj_kernel(a_ref, w_ref, cs_ref, *rest, rotary, sigmoid, heads):
    if rotary:
        cos_ref, sa_ref, sb_ref, o_ref = rest
    else:
        (o_ref,) = rest
    res = jnp.dot(a_ref[...], w_ref[...], preferred_element_type=F32) * cs_ref[...]
    if sigmoid:
        res = jax.nn.sigmoid(res)
    for j in range(res.shape[1] // LANES):
        r = res[:, j * LANES:(j + 1) * LANES]
        if rotary:
            half = ROPE_DIM // 2
            r = (r * cos_ref[...] + pltpu.roll(r, half, 1) * sa_ref[...]
                 + pltpu.roll(r, LANES - half, 1) * sb_ref[...])
        if heads:
            o_ref[0, j] = r.astype(o_ref.dtype)
        else:
            o_ref[:, j * LANES:(j + 1) * LANES] = r.astype(o_ref.dtype)


def _proj(a, w, colscale, *, seq, tm, tn, out_dtype, heads, sigmoid=False, rope=None, name):
    t, k = a.shape
    n = w.shape[1]
    sblk = seq // tm
    in_specs = [
        pl.BlockSpec((tm, k), lambda m, j: (m, 0)),
        pl.BlockSpec((k, tn), lambda m, j: (0, j)),
        pl.BlockSpec((1, tn), lambda m, j: (0, j)),
    ]
    args = [a, w, colscale]
    if rope is not None:
        in_specs += [pl.BlockSpec((tm, LANES), lambda m, j: (m % sblk, 0))] * 3
        args += list(rope)
    if heads:
        out_shape = jax.ShapeDtypeStruct((t // seq, n // LANES, seq, LANES), out_dtype)
        out_spec = pl.BlockSpec((1, tn // LANES, tm, LANES), lambda m, j: (m // sblk, j, m % sblk, 0))
    else:
        out_shape = jax.ShapeDtypeStruct((t, n), out_dtype)
        out_spec = pl.BlockSpec((tm, tn), lambda m, j: (m, j))
    return pl.pallas_call(
        functools.partial(_proj_kernel, rotary=rope is not None, sigmoid=sigmoid, heads=heads),
        out_shape=out_shape,
        grid=(t // tm, n // tn),
        in_specs=in_specs,
        out_specs=out_spec,
        compiler_params=_params(("parallel", "arbitrary")),
        name=name,
    )(*args)


def _rope_tables(seq):
    half = ROPE_DIM // 2
    inv_freq = 1.0 / (ROPE_THETA ** (jnp.arange(half, dtype=F32) / half))
    ang = jnp.arange(seq, dtype=F32)[:, None] * inv_freq[None, :]
    cos, sin = jnp.cos(ang), jnp.sin(ang)
    ones = jnp.ones((seq, LANES - ROPE_DIM), F32)
    zeros = jnp.zeros((seq, LANES - ROPE_DIM), F32)
    zh = jnp.zeros((seq, half), F32)
    cos_t = jnp.concatenate([cos, cos, ones], axis=1)
    sin_a = jnp.concatenate([zh, sin, zeros], axis=1)
    sin_b = jnp.concatenate([-sin, zh, zeros], axis=1)
    return cos_t, sin_a, sin_b


def _forget_kernel(fl_ref, b_ref, c_ref):
    z = fl_ref[...] + b_ref[...]
    logf = jnp.minimum(z, 0.0) - jnp.log1p(jnp.exp(-jnp.abs(z)))
    row = lax.broadcasted_iota(jnp.int32, (LANES, LANES), 0)
    col = lax.broadcasted_iota(jnp.int32, (LANES, LANES), 1)
    tri = (row <= col).astype(BF16)
    carry = jnp.zeros((logf.shape[0], 1), F32)
    for j in range(logf.shape[1] // LANES):
        xb = logf[:, j * LANES:(j + 1) * LANES]
        hi = xb.astype(BF16)
        r1 = xb - hi.astype(F32)
        mid = r1.astype(BF16)
        lo = (r1 - mid.astype(F32)).astype(BF16)
        cs = (jnp.dot(hi, tri, preferred_element_type=F32) + jnp.dot(mid, tri, preferred_element_type=F32)
              + jnp.dot(lo, tri, preferred_element_type=F32)) + carry
        c_ref[:, j * LANES:(j + 1) * LANES] = cs
        carry = cs[:, LANES - 1:LANES]


def _forget_cumsum(fl_t, b_forget):
    b, h, s = fl_t.shape
    return pl.pallas_call(
        _forget_kernel,
        out_shape=jax.ShapeDtypeStruct((b, h, s), F32),
        grid=(b,),
        in_specs=[pl.BlockSpec((None, h, s), lambda i: (i, 0, 0)), pl.BlockSpec((h, 1), lambda i: (0, 0))],
        out_specs=pl.BlockSpec((None, h, s), lambda i: (i, 0, 0)),
        compiler_params=_params(("parallel",)),
        name="forget_cumsum",
    )(fl_t, b_forget.reshape(h, 1).astype(F32))


def _attn_kernel(*refs, moba, tq):
    if moba:
        q_ref, k_ref, v_ref, o_ref, km_ref = refs
    else:
        q_ref, k_ref, v_ref, c_ref, o_ref = refs
    qi = pl.program_id(2)
    q = q_ref[...]
    nt = (((1,), (1,)), ((), ()))

    if moba:
        @pl.when(qi == 0)
        def _():
            km_ref[...] = jnp.zeros_like(km_ref)
            for n in range(k_ref.shape[0] // MOBA_BLOCK):
                kb = k_ref[n * MOBA_BLOCK:(n + 1) * MOBA_BLOCK, :].astype(F32)
                km_ref[n:n + 1, :] = jnp.mean(kb, axis=0, keepdims=True)

        lane = lax.broadcasted_iota(jnp.int32, (tq, LANES), 1)
        gate = lax.dot_general(q, km_ref[...].astype(BF16), nt, preferred_element_type=F32)
        gate = jnp.where(lane < qi, gate, NEG_INF)

    def block(n):
        start = pl.multiple_of(n * tq, tq)
        k = k_ref[pl.ds(start, tq), :]
        v = v_ref[pl.ds(start, tq), :]
        s = lax.dot_general(q, k, nt, preferred_element_type=F32)
        if not moba:
            s = s - c_ref[n]
        return s, v

    s, v = block(qi)
    row = lax.broadcasted_iota(jnp.int32, (tq, tq), 0)
    col = lax.broadcasted_iota(jnp.int32, (tq, tq), 1)
    s = jnp.where(row >= col, s, NEG_INF)
    m = jnp.max(s, axis=1, keepdims=True)
    p = jnp.exp(s - m)
    l = jnp.sum(p, axis=1, keepdims=True)
    acc = jnp.dot(p.astype(BF16), v, preferred_element_type=F32)

    def body(n, carry):
        m, l, acc = carry
        s, v = block(n)
        if moba:
            g_n = jnp.max(jnp.where(lane == n, gate, NEG_INF), axis=1, keepdims=True)
            beats = (gate > g_n) | ((gate == g_n) & (lane < n))
            rank = jnp.sum(beats.astype(F32), axis=1, keepdims=True)
            s = jnp.where(rank < MOBA_TOPK, s, NEG_INF)
        m_new = jnp.maximum(m, jnp.max(s, axis=1, keepdims=True))
        alpha = jnp.exp(m - m_new)
        p = jnp.exp(s - m_new)
        l = alpha * l + jnp.sum(p, axis=1, keepdims=True)
        acc = alpha * acc + jnp.dot(p.astype(BF16), v, preferred_element_type=F32)
        return m_new, l, acc

    m, l, acc = lax.fori_loop(0, qi, body, (m, l, acc))
    o_ref[...] = (acc / l).astype(o_ref.dtype)


def _attention(q_arr, q_off, k_arr, k_off, v_arr, v_off, c5, *, moba, name):
    b, _, s, d = q_arr.shape
    tq = MOBA_BLOCK
    nq = s // tq
    in_specs = [
        pl.BlockSpec((None, None, tq, d), lambda bi, h, qi: (bi, q_off + h, qi, 0)),
        pl.BlockSpec((None, None, s, d), lambda bi, h, qi: (bi, k_off + h, 0, 0)),
        pl.BlockSpec((None, None, s, d), lambda bi, h, qi: (bi, v_off + h, 0, 0)),
    ]
    args = [q_arr, k_arr, v_arr]
    scratch = []
    if moba:
        scratch = [pltpu.VMEM((LANES, d), F32)]
    else:
        in_specs.append(pl.BlockSpec((None, None, nq, 1, tq), lambda bi, h, qi: (bi, h, 0, 0, 0)))
        args.append(c5)
    return pl.pallas_call(
        functools.partial(_attn_kernel, moba=moba, tq=tq),
        out_shape=jax.ShapeDtypeStruct((b * s, N_HEADS * d), BF16),
        grid=(b, N_HEADS, nq),
        in_specs=in_specs,
        out_specs=pl.BlockSpec((tq, d), lambda bi, h, qi: (bi * nq + qi, h)),
        scratch_shapes=scratch,
        compiler_params=_params(("parallel", "parallel", "arbitrary")),
        name=name,
    )(*args)


def _mix_kernel(of_ref, om_ref, g_ref, x_ref, wf_ref, wm_ref, wo_ref, gn_ref, wr_ref, br_ref,
                x2_ref, h2_ref, lg_ref):
    d = x_ref.shape[1]
    yf = jnp.dot(of_ref[...], wf_ref[...], preferred_element_type=F32)
    ym = jnp.dot(om_ref[...], wm_ref[...], preferred_element_type=F32)
    mixed = g_ref[:, :d].astype(F32) * yf + g_ref[:, d:].astype(F32) * ym
    x2 = x_ref[...] + jnp.dot(mixed.astype(BF16), wo_ref[...], preferred_element_type=F32)
    x2_ref[...] = x2
    ms = jnp.mean(x2 * x2, axis=-1, keepdims=True)
    h2 = x2 * lax.rsqrt(ms + RMS_EPS) * gn_ref[...]
    h2_ref[...] = h2
    lg_ref[...] = jnp.dot(h2.astype(BF16), wr_ref[...], preferred_element_type=F32) + br_ref[...]


def _mix(o_fox, o_moba, gates, x, w_f, w_m, w_o, g_ffn, w_r, b_r, tm):
    t, d = x.shape
    w = o_fox.shape[1]
    const = lambda i: (0, 0)
    resident = dict(pipeline_mode=pl.Buffered(1))
    return pl.pallas_call(
        _mix_kernel,
        out_shape=(jax.ShapeDtypeStruct((t, d), F32), jax.ShapeDtypeStruct((t, d), F32),
                   jax.ShapeDtypeStruct((t, LANES), F32)),
        grid=(t // tm,),
        in_specs=[
            pl.BlockSpec((tm, w), lambda i: (i, 0)),
            pl.BlockSpec((tm, w), lambda i: (i, 0)),
            pl.BlockSpec((tm, 2 * d), lambda i: (i, 0)),
            pl.BlockSpec((tm, d), lambda i: (i, 0)),
            pl.BlockSpec((w, d), const, **resident),
            pl.BlockSpec((w, d), const, **resident),
            pl.BlockSpec((d, d), const, **resident),
            pl.BlockSpec((1, d), const),
            pl.BlockSpec((d, LANES), const),
            pl.BlockSpec((1, LANES), const),
        ],
        out_specs=(pl.BlockSpec((tm, d), lambda i: (i, 0)), pl.BlockSpec((tm, d), lambda i: (i, 0)),
                   pl.BlockSpec((tm, LANES), lambda i: (i, 0))),
        compiler_params=_params(("parallel",)),
        name="mix_out_router",
    )(o_fox, o_moba, gates, x, w_f, w_m, w_o, g_ffn, w_r, b_r)


def _router_kernel(lg_ref, meta_ref, gate_ref, cnt_ref, carry_ref):
    @pl.when(pl.program_id(0) == 0)
    def _():
        carry_ref[...] = jnp.zeros_like(carry_ref)

    l = lg_ref[...]
    tr = l.shape[0]
    lane = lax.broadcasted_iota(jnp.int32, (tr, LANES), 1).astype(F32)
    vals, idxs = [], []
    for _ in range(TOP_K):
        m = jnp.max(l, axis=1, keepdims=True)
        ix = jnp.min(jnp.where(l == m, lane, float(LANES)), axis=1, keepdims=True)
        vals.append(m)
        idxs.append(ix)
        l = jnp.where(lane == ix, NEG_INF, l)
    ex = [jnp.exp(v - vals[0]) for v in vals]
    den = ex[0] + ex[1] + ex[2] + ex[3]
    onehot = jnp.zeros((tr, LANES), F32)
    for ix in idxs:
        onehot = onehot + (lane == ix).astype(F32)
    row = lax.broadcasted_iota(jnp.int32, (tr, tr), 0)
    col = lax.broadcasted_iota(jnp.int32, (tr, tr), 1)
    before = (col < row).astype(BF16)
    running = jnp.dot(before, onehot.astype(BF16), preferred_element_type=F32) + carry_ref[...]
    meta = jnp.zeros((tr, LANES), F32)
    gate = jnp.zeros((tr, LANES), F32)
    for k in range(TOP_K):
        pos = jnp.sum(jnp.where(lane == idxs[k], running, 0.0), axis=1, keepdims=True)
        meta = jnp.where(lane == float(k), idxs[k], meta)
        meta = jnp.where(lane == float(TOP_K + k), pos, meta)
        gate = jnp.where(lane == float(k), ex[k] / den, gate)
    meta_ref[...] = meta.astype(jnp.int32)
    gate_ref[...] = gate
    carry_ref[...] += jnp.sum(onehot, axis=0, keepdims=True)
    cnt_ref[...] = carry_ref[...]


def _router(logits, tr):
    t = logits.shape[0]
    return pl.pallas_call(
        _router_kernel,
        out_shape=(jax.ShapeDtypeStruct((t, LANES), jnp.int32), jax.ShapeDtypeStruct((t, LANES), F32),
                   jax.ShapeDtypeStruct((1, LANES), F32)),
        grid=(t // tr,),
        in_specs=[pl.BlockSpec((tr, LANES), lambda i: (i, 0))],
        out_specs=(pl.BlockSpec((tr, LANES), lambda i: (i, 0)), pl.BlockSpec((tr, LANES), lambda i: (i, 0)),
                   pl.BlockSpec((1, LANES), lambda i: (0, 0))),
        scratch_shapes=[pltpu.VMEM((1, LANES), F32)],
        compiler_params=_params(("arbitrary",)),
        name="router_topk",
    )(logits)


def _row_copy(src_hbm, row, dst_vmem, slot, sem):
    return pltpu.make_async_copy(src_hbm.at[pl.ds(row, 1)], dst_vmem.at[pl.ds(slot, 1)], sem)


def _gather_kernel(nused_ref, tok_ref, h_hbm, o_ref, buf, sem):
    r = pl.program_id(0)
    m = buf.shape[0]

    @pl.when(r < nused_ref[0])
    def _():
        def issue(i, c):
            _row_copy(h_hbm, tok_ref[0, 0, i], buf, i, sem).start()
            return c

        lax.fori_loop(0, m, issue, 0)
        pltpu.make_async_copy(h_hbm.at[pl.ds(0, m)], buf, sem).wait()
        o_ref[...] = buf[...].astype(o_ref.dtype)

    @pl.when(r >= nused_ref[0])
    def _():
        o_ref[...] = jnp.zeros_like(o_ref)


def _gather_rows(h2, row_tok, n_used, m):
    t, d = h2.shape
    n_blocks = row_tok.shape[0] // m
    return pl.pallas_call(
        _gather_kernel,
        out_shape=jax.ShapeDtypeStruct((n_blocks * m, d), BF16),
        grid_spec=pltpu.PrefetchScalarGridSpec(
            num_scalar_prefetch=1,
            grid=(n_blocks,),
            in_specs=[
                pl.BlockSpec((1, 1, m), lambda r, nu: (r, 0, 0), memory_space=pltpu.SMEM),
                pl.BlockSpec(memory_space=pl.ANY),
            ],
            out_specs=pl.BlockSpec((m, d), lambda r, nu: (r, 0)),
            scratch_shapes=[pltpu.VMEM((m, d), F32), pltpu.SemaphoreType.DMA(())],
        ),
        compiler_params=_params(("arbitrary",), disable_bounds_checks=True),
        name="expert_row_gather",
    )(n_used, row_tok.reshape(n_blocks, 1, m), h2)


def _gmm1_kernel(be_ref, nused_ref, x_ref, wg_ref, wl_ref, bg_ref, bl_ref, a_ref):
    @pl.when(pl.program_id(1) < nused_ref[0])
    def _():
        x = x_ref[...]
        ug = jnp.dot(x, wg_ref[...], preferred_element_type=F32) + bg_ref[...]
        ul = jnp.dot(x, wl_ref[...], preferred_element_type=F32) + bl_ref[...]
        glu = jnp.minimum(ug, SWIGLU_LIMIT)
        lin = jnp.clip(ul, -SWIGLU_LIMIT, SWIGLU_LIMIT)
        a_ref[...] = (glu * jax.nn.sigmoid(SWIGLU_ALPHA * glu) * (lin + 1.0)).astype(a_ref.dtype)

    @pl.when(pl.program_id(1) >= nused_ref[0])
    def _():
        a_ref[...] = jnp.zeros_like(a_ref)


def _gmm1(xb, w1g, w1l, b1g, b1l, block_e, n_used, m, tn):
    rows, d = xb.shape
    f = w1g.shape[2]
    n_blocks = rows // m

    def rc(r, nu):
        return jnp.minimum(r, nu[0] - 1)

    return pl.pallas_call(
        _gmm1_kernel,
        out_shape=jax.ShapeDtypeStruct((rows, f), BF16),
        grid_spec=pltpu.PrefetchScalarGridSpec(
            num_scalar_prefetch=2,
            grid=(f // tn, n_blocks),
            in_specs=[
                pl.BlockSpec((m, d), lambda j, r, be, nu: (rc(r, nu), 0)),
                pl.BlockSpec((None, d, tn), lambda j, r, be, nu: (be[rc(r, nu)], 0, j)),
                pl.BlockSpec((None, d, tn), lambda j, r, be, nu: (be[rc(r, nu)], 0, j)),
                pl.BlockSpec((None, 1, tn), lambda j, r, be, nu: (be[rc(r, nu)], 0, j)),
                pl.BlockSpec((None, 1, tn), lambda j, r, be, nu: (be[rc(r, nu)], 0, j)),
            ],
            out_specs=pl.BlockSpec((m, tn), lambda j, r, be, nu: (r, j)),
        ),
        compiler_params=_params(("arbitrary", "arbitrary")),
        name="expert_mlp1_swiglu",
    )(block_e, n_used, xb, w1g, w1l, b1g, b1l)


def _gmm2_kernel(be_ref, nused_ref, a_ref, w_ref, b_ref, y_ref):
    @pl.when(pl.program_id(1) < nused_ref[0])
    def _():
        y_ref[...] = jnp.dot(a_ref[...], w_ref[...], preferred_element_type=F32) + b_ref[...]

    @pl.when(pl.program_id(1) >= nused_ref[0])
    def _():
        y_ref[...] = jnp.zeros_like(y_ref)


def _gmm2(a, w2, b2, block_e, n_used, m, tn):
    rows, f = a.shape
    d = w2.shape[2]
    n_blocks = rows // m

    def rc(r, nu):
        return jnp.minimum(r, nu[0] - 1)

    return pl.pallas_call(
        _gmm2_kernel,
        out_shape=jax.ShapeDtypeStruct((rows, d), F32),
        grid_spec=pltpu.PrefetchScalarGridSpec(
            num_scalar_prefetch=2,
            grid=(d // tn, n_blocks),
            in_specs=[
                pl.BlockSpec((m, f), lambda j, r, be, nu: (rc(r, nu), 0)),
                pl.BlockSpec((None, f, tn), lambda j, r, be, nu: (be[rc(r, nu)], 0, j)),
                pl.BlockSpec((None, 1, tn), lambda j, r, be, nu: (be[rc(r, nu)], 0, j)),
            ],
            out_specs=pl.BlockSpec((m, tn), lambda j, r, be, nu: (r, j)),
        ),
        compiler_params=_params(("arbitrary", "arbitrary")),
        name="expert_mlp2",
    )(block_e, n_used, a, w2, b2)


def _combine_kernel(dest_ref, y_hbm, gate_ref, x2_ref, g_ref, o_ref, buf, sem, *, final_norm):
    tc = x2_ref.shape[0]

    def issue(i, c):
        for k in range(TOP_K):
            _row_copy(y_hbm, dest_ref[0, 0, i * TOP_K + k], buf.at[k], i, sem).start()
        return c

    lax.fori_loop(0, tc, issue, 0)
    for k in range(TOP_K):
        pltpu.make_async_copy(y_hbm.at[pl.ds(0, tc)], buf.at[k], sem).wait()
    acc = x2_ref[...]
    for k in range(TOP_K):
        acc = acc + gate_ref[:, k:k + 1] * buf[k]
    if final_norm:
        ms = jnp.mean(acc * acc, axis=-1, keepdims=True)
        acc = acc * lax.rsqrt(ms + RMS_EPS) * g_ref[...]
    o_ref[...] = acc


def _combine(y, dest, gates, x2, g_final, tc, final_norm):
    t, d = x2.shape
    nt = t // tc
    return pl.pallas_call(
        functools.partial(_combine_kernel, final_norm=final_norm),
        out_shape=jax.ShapeDtypeStruct((t, d), F32),
        grid=(nt,),
        in_specs=[
            pl.BlockSpec((1, 1, tc * TOP_K), lambda i: (i, 0, 0), memory_space=pltpu.SMEM),
            pl.BlockSpec(memory_space=pl.ANY),
            pl.BlockSpec((tc, LANES), lambda i: (i, 0)),
            pl.BlockSpec((tc, d), lambda i: (i, 0)),
            pl.BlockSpec((1, d), lambda i: (0, 0)),
        ],
        out_specs=pl.BlockSpec((tc, d), lambda i: (i, 0)),
        scratch_shapes=[pltpu.VMEM((TOP_K, tc, d), F32), pltpu.SemaphoreType.DMA(())],
        compiler_params=_params(("arbitrary",), disable_bounds_checks=True),
        name="moe_combine_final_norm",
    )(dest.reshape(nt, 1, tc * TOP_K), y, gates, x2, g_final.reshape(1, d))


def _layer(x, norm_mix_g, w_in, b_forget, w_branch_fox, w_branch_moba, w_out, norm_ffn_g,
           w_router, b_router, w_mlp1, b_mlp1, w_mlp2, b_mlp2):
    b, s, d = x.shape
    t = b * s
    width = N_HEADS * HEAD_DIM
    xt = x.reshape(t, d)
    scale = HEAD_DIM ** -0.5

    o = np.cumsum([0, width, width, width, N_HEADS, width, width, width, d, d])
    w_fox = w_in[:, o[0]:o[3]].astype(BF16)
    w_fl = jnp.pad(w_in[:, o[3]:o[4]], ((0, 0), (0, LANES - N_HEADS))).astype(BF16)
    w_mqk = w_in[:, o[4]:o[6]].astype(BF16)
    w_mv = w_in[:, o[6]:o[7]].astype(BF16)
    w_g = w_in[:, o[7]:o[9]].astype(BF16)
    q_scale = jnp.concatenate([jnp.full((1, width), scale, F32), jnp.ones((1, 2 * width), F32)], axis=1)

    tm = min(1024, s)
    h = _rmsnorm(xt, norm_mix_g, tm=min(512, t))
    fox = _proj(h, w_fox, q_scale, seq=s, tm=tm, tn=512, out_dtype=BF16, heads=True, name="proj_fox_qkv")
    mqk = _proj(h, w_mqk, q_scale[:, :2 * width], seq=s, tm=tm, tn=512, out_dtype=BF16, heads=True,
                rope=_rope_tables(s), name="proj_moba_qk")
    mv = _proj(h, w_mv, q_scale[:, width:2 * width], seq=s, tm=tm, tn=512, out_dtype=BF16, heads=True,
               name="proj_moba_v")
    gates = _proj(h, w_g, jnp.ones((1, 2 * d), F32), seq=s, tm=tm, tn=512, out_dtype=BF16, heads=False,
                  sigmoid=True, name="proj_gates")
    fl = _proj(h, w_fl, jnp.ones((1, LANES), F32), seq=s, tm=tm, tn=LANES, out_dtype=F32, heads=False,
               name="proj_forget")

    fl_t = fl[:, :N_HEADS].reshape(b, s, N_HEADS).transpose(0, 2, 1)
    c = _forget_cumsum(fl_t, b_forget)
    c5 = c.reshape(b, N_HEADS, s // MOBA_BLOCK, 1, MOBA_BLOCK)

    o_fox = _attention(fox, 0, fox, N_HEADS, fox, 2 * N_HEADS, c5, moba=False, name="fox_attention")
    o_moba = _attention(mqk, 0, mqk, N_HEADS, mv, 0, None, moba=True, name="moba_attention")

    w_r = jnp.pad(w_router, ((0, 0), (0, LANES - N_EXPERTS))).astype(BF16)
    b_r = jnp.concatenate([b_router.astype(F32), jnp.full((LANES - N_EXPERTS,), NEG_INF, F32)]).reshape(1, LANES)
    x2, h2, logits = _mix(o_fox, o_moba, gates, xt, w_branch_fox.astype(BF16), w_branch_moba.astype(BF16),
                          w_out.astype(BF16), norm_ffn_g.reshape(1, d), w_r, b_r, tm=min(256, t))

    meta, gate_w, cnt = _router(logits, tr=min(512, t))

    m = MOE_ROWS
    assign = t * TOP_K
    n_blocks = (assign + N_EXPERTS * (m - 1) + m - 1) // m
    counts = cnt[0, :N_EXPERTS].astype(jnp.int32)
    padded = (counts + m - 1) // m * m
    pend = jnp.cumsum(padded)
    pstart = pend - padded
    dest = pstart[meta[:, :TOP_K]] + meta[:, TOP_K:2 * TOP_K]
    n_used = (pend[-1] // m).astype(jnp.int32).reshape(1)
    block_e = jnp.minimum(jnp.searchsorted(pend, jnp.arange(n_blocks, dtype=jnp.int32) * m, side='right'),
                          N_EXPERTS - 1).astype(jnp.int32)
    row_tok = jnp.zeros((n_blocks * m,), jnp.int32).at[dest.reshape(-1)].set(
        jnp.arange(assign, dtype=jnp.int32) // TOP_K)

    xb = _gather_rows(h2, row_tok, n_used, m)
    f = w_mlp1.shape[2] // 2
    w1g = w_mlp1[:, :, 0::2].astype(BF16)
    w1l = w_mlp1[:, :, 1::2].astype(BF16)
    b1g = b_mlp1[:, 0::2].reshape(N_EXPERTS, 1, f)
    b1l = b_mlp1[:, 1::2].reshape(N_EXPERTS, 1, f)
    a = _gmm1(xb, w1g, w1l, b1g, b1l, block_e, n_used, m, tn=min(512, f))
    y = _gmm2(a, w_mlp2.astype(BF16), b_mlp2.reshape(N_EXPERTS, 1, d), block_e, n_used, m, tn=min(1024, d))
    return x2, y, dest, gate_w


def kernel(x, norm_mix_g, w_in, b_forget, w_branch_fox, w_branch_moba, w_out, norm_ffn_g, w_router, b_router,
           w_mlp1, b_mlp1, w_mlp2, b_mlp2, norm_final_g):
    depth = w_in.shape[0]
    b, s, d = x.shape
    for l in range(depth):
        x2, y, dest, gate_w = _layer(x, norm_mix_g[l], w_in[l], b_forget[l], w_branch_fox[l], w_branch_moba[l],
                                     w_out[l], norm_ffn_g[l], w_router[l], b_router[l], w_mlp1[l], b_mlp1[l],
                                     w_mlp2[l], b_mlp2[l])
        out = _combine(y, dest, gate_w, x2, norm_final_g, tc=min(128, b * s), final_norm=l == depth - 1)
        x = out.reshape(b, s, d)
    return x
```

```python
import functools

import jax
import jax.numpy as jnp
import numpy as np
from jax import lax
from jax.experimental import pallas as pl
from jax.experimental.pallas import tpu as pltpu

HEAD_DIM = 128
N_HEADS = 8
MOBA_BLOCK = 256
MOBA_TOPK = 3
ROPE_THETA = 500000.0
ROPE_DIM = HEAD_DIM // 4
N_EXPERTS = 32
TOP_K = 4
SWIGLU_ALPHA = 1.702
SWIGLU_LIMIT = 7.0
RMS_EPS = 1e-5

LANES = 128
VMEM_LIMIT_BYTES = 56 * 1024 * 1024
MOE_ROWS = 512

F32 = jnp.float32
BF16 = jnp.bfloat16
NEG_INF = float("-inf")


def _params(sem, **kw):
    return pltpu.CompilerParams(dimension_semantics=sem, vmem_limit_bytes=VMEM_LIMIT_BYTES, **kw)


def _store_token_tiles(ref, val):
    n, d = val.shape
    nj = d // LANES
    for j in range(nj):
        ref[pl.ds(j, n, stride=nj), :] = val[:, j * LANES:(j + 1) * LANES]


def _token_pitch(nj):
    p = -(-nj // 4)
    return 4 * (p if p % 2 else p + 1)


def _token_copy(src_hbm, row, dst_vmem, slot, nj, sem):
    return pltpu.make_async_copy(src_hbm.at[pl.ds(pl.multiple_of(row * nj, nj), nj)],
                                 dst_vmem.at[pl.ds(slot * _token_pitch(nj), nj)], sem)


def _load_token_cols(buf, n, nj, j):
    return buf[pl.ds(j, n, stride=_token_pitch(nj)), :]


def _rmsnorm_kernel(x_ref, g_ref, o_ref):
    x = x_ref[...]
    ms = jnp.mean(x * x, axis=-1, keepdims=True)
    o_ref[...] = (x * lax.rsqrt(ms + RMS_EPS) * g_ref[...]).astype(o_ref.dtype)


def _rmsnorm(x, g, tm):
    t, d = x.shape
    return pl.pallas_call(
        _rmsnorm_kernel,
        out_shape=jax.ShapeDtypeStruct((t, d), BF16),
        grid=(t // tm,),
        in_specs=[pl.BlockSpec((tm, d), lambda i: (i, 0)), pl.BlockSpec((1, d), lambda i: (0, 0))],
        out_specs=pl.BlockSpec((tm, d), lambda i: (i, 0)),
        compiler_params=_params(("parallel",)),
        name="rmsnorm_mix",
    )(x, g.reshape(1, d))


def _proj_kernel(a_ref, w_ref, cs_ref, *rest, rotary, sigmoid, heads):
    if rotary:
        cos_ref, sa_ref, sb_ref, o_ref = rest
    else:
        (o_ref,) = rest
    res = jnp.dot(a_ref[...], w_ref[...], preferred_element_type=F32) * cs_ref[...]
    if sigmoid:
        res = jax.nn.sigmoid(res)
    for j in range(res.shape[1] // LANES):
        r = res[:, j * LANES:(j + 1) * LANES]
        if rotary:
            half = ROPE_DIM // 2
            r = (r * cos_ref[...] + pltpu.roll(r, half, 1) * sa_ref[...]
                 + pltpu.roll(r, LANES - half, 1) * sb_ref[...])
        if heads:
            o_ref[0, j] = r.astype(o_ref.dtype)
        else:
            o_ref[:, j * LANES:(j + 1) * LANES] = r.astype(o_ref.dtype)


def _proj(a, w, colscale, *, seq, tm, tn, out_dtype, heads, sigmoid=False, rope=None, name):
    t, k = a.shape
    n = w.shape[1]
    sblk = seq // tm
    in_specs = [
        pl.BlockSpec((tm, k), lambda m, j: (m, 0)),
        pl.BlockSpec((k, tn), lambda m, j: (0, j)),
        pl.BlockSpec((1, tn), lambda m, j: (0, j)),
    ]
    args = [a, w, colscale]
    if rope is not None:
        in_specs += [pl.BlockSpec((tm, LANES), lambda m, j: (m % sblk, 0))] * 3
        args += list(rope)
    if heads:
        out_shape = jax.ShapeDtypeStruct((t // seq, n // LANES, seq, LANES), out_dtype)
        out_spec = pl.BlockSpec((1, tn // LANES, tm, LANES), lambda m, j: (m // sblk, j, m % sblk, 0))
    else:
        out_shape = jax.ShapeDtypeStruct((t, n), out_dtype)
        out_spec = pl.BlockSpec((tm, tn), lambda m, j: (m, j))
    return pl.pallas_call(
        functools.partial(_proj_kernel, rotary=rope is not None, sigmoid=sigmoid, heads=heads),
        out_shape=out_shape,
        grid=(t // tm, n // tn),
        in_specs=in_specs,
        out_specs=out_spec,
        compiler_params=_params(("parallel", "arbitrary")),
        name=name,
    )(*args)


def _rope_tables(seq):
    half = ROPE_DIM // 2
    inv_freq = 1.0 / (ROPE_THETA ** (jnp.arange(half, dtype=F32) / half))
    ang = jnp.arange(seq, dtype=F32)[:, None] * inv_freq[None, :]
    cos, sin = jnp.cos(ang), jnp.sin(ang)
    ones = jnp.ones((seq, LANES - ROPE_DIM), F32)
    zeros = jnp.zeros((seq, LANES - ROPE_DIM), F32)
    zh = jnp.zeros((seq, half), F32)
    cos_t = jnp.concatenate([cos, cos, ones], axis=1)
    sin_a = jnp.concatenate([zh, sin, zeros], axis=1)
    sin_b = jnp.concatenate([-sin, zh, zeros], axis=1)
    return cos_t, sin_a, sin_b


def _forget_kernel(fl_ref, b_ref, c_ref):
    z = fl_ref[...] + b_ref[...]
    logf = jnp.minimum(z, 0.0) - jnp.log1p(jnp.exp(-jnp.abs(z)))
    row = lax.broadcasted_iota(jnp.int32, (LANES, LANES), 0)
    col = lax.broadcasted_iota(jnp.int32, (LANES, LANES), 1)
    tri = (row <= col).astype(BF16)
    carry = jnp.zeros((logf.shape[0], 1), F32)
    for j in range(logf.shape[1] // LANES):
        xb = logf[:, j * LANES:(j + 1) * LANES]
        hi = xb.astype(BF16)
        r1 = xb - hi.astype(F32)
        mid = r1.astype(BF16)
        lo = (r1 - mid.astype(F32)).astype(BF16)
        cs = (jnp.dot(hi, tri, preferred_element_type=F32) + jnp.dot(mid, tri, preferred_element_type=F32)
              + jnp.dot(lo, tri, preferred_element_type=F32)) + carry
        c_ref[:, j * LANES:(j + 1) * LANES] = cs
        carry = cs[:, LANES - 1:LANES]


def _forget_cumsum(fl_t, b_forget):
    b, h, s = fl_t.shape
    return pl.pallas_call(
        _forget_kernel,
        out_shape=jax.ShapeDtypeStruct((b, h, s), F32),
        grid=(b,),
        in_specs=[pl.BlockSpec((None, h, s), lambda i: (i, 0, 0)), pl.BlockSpec((h, 1), lambda i: (0, 0))],
        out_specs=pl.BlockSpec((None, h, s), lambda i: (i, 0, 0)),
        compiler_params=_params(("parallel",)),
        name="forget_cumsum",
    )(fl_t, b_forget.reshape(h, 1).astype(F32))


def _attn_kernel(*refs, moba, tq):
    if moba:
        q_ref, k_ref, v_ref, o_ref, km_ref = refs
    else:
        q_ref, k_ref, v_ref, c_ref, o_ref = refs
    nq = q_ref.shape[0] // tq
    nt = (((1,), (1,)), ((), ()))
    row = lax.broadcasted_iota(jnp.int32, (tq, tq), 0)
    col = lax.broadcasted_iota(jnp.int32, (tq, tq), 1)
    causal = row >= col

    if moba:
        km_ref[...] = jnp.zeros_like(km_ref)
        for n in range(nq):
            kb = k_ref[n * tq:(n + 1) * tq, :].astype(F32)
            km_ref[n:n + 1, :] = jnp.mean(kb, axis=0, keepdims=True)
        kmean = km_ref[...].astype(BF16)
        lane = lax.broadcasted_iota(jnp.int32, (tq, LANES), 1)

    def scores(q, n):
        s = lax.dot_general(q, k_ref[n * tq:(n + 1) * tq, :], nt, preferred_element_type=F32)
        if not moba:
            s = s - c_ref[n]
        return s

    for qi in range(nq):
        q = q_ref[qi * tq:(qi + 1) * tq, :]
        s = jnp.where(causal, scores(q, qi), NEG_INF)
        m = jnp.max(s, axis=1, keepdims=True)
        p = jnp.exp(s - m)
        l = jnp.sum(p, axis=1, keepdims=True)
        acc = jnp.dot(p.astype(BF16), v_ref[qi * tq:(qi + 1) * tq, :], preferred_element_type=F32)
        if moba and qi > 0:
            gate = lax.dot_general(q, kmean, nt, preferred_element_type=F32)
            gate = jnp.where(lane < qi, gate, NEG_INF)
        for n in range(qi):
            s = scores(q, n)
            if moba:
                g_n = gate[:, n:n + 1]
                beats = (gate > g_n) | ((gate == g_n) & (lane < n))
                rank = jnp.sum(beats.astype(F32), axis=1, keepdims=True)
                s = jnp.where(rank < MOBA_TOPK, s, NEG_INF)
            m_new = jnp.maximum(m, jnp.max(s, axis=1, keepdims=True))
            alpha = jnp.exp(m - m_new)
            p = jnp.exp(s - m_new)
            l = alpha * l + jnp.sum(p, axis=1, keepdims=True)
            acc = alpha * acc + jnp.dot(p.astype(BF16), v_ref[n * tq:(n + 1) * tq, :],
                                        preferred_element_type=F32)
            m = m_new
        o_ref[qi * tq:(qi + 1) * tq, :] = (acc / l).astype(o_ref.dtype)


def _attention(q_arr, q_off, k_arr, k_off, v_arr, v_off, c5, *, moba, name):
    b, _, s, d = q_arr.shape
    tq = MOBA_BLOCK
    nq = s // tq
    in_specs = [
        pl.BlockSpec((None, None, s, d), lambda bi, h: (bi, q_off + h, 0, 0)),
        pl.BlockSpec((None, None, s, d), lambda bi, h: (bi, k_off + h, 0, 0)),
        pl.BlockSpec((None, None, s, d), lambda bi, h: (bi, v_off + h, 0, 0)),
    ]
    args = [q_arr, k_arr, v_arr]
    scratch = []
    if moba:
        scratch = [pltpu.VMEM((LANES, d), F32)]
    else:
        in_specs.append(pl.BlockSpec((None, None, nq, 1, tq), lambda bi, h: (bi, h, 0, 0, 0)))
        args.append(c5)
    return pl.pallas_call(
        functools.partial(_attn_kernel, moba=moba, tq=tq),
        out_shape=jax.ShapeDtypeStruct((b * s, N_HEADS * d), BF16),
        grid=(b, N_HEADS),
        in_specs=in_specs,
        out_specs=pl.BlockSpec((s, d), lambda bi, h: (bi, h)),
        scratch_shapes=scratch,
        compiler_params=_params(("parallel", "parallel")),
        name=name,
    )(*args)


def _mix_kernel(of_ref, om_ref, g_ref, x_ref, wf_ref, wm_ref, wo_ref, gn_ref, wr_ref, br_ref,
                x2_ref, h2_ref, lg_ref):
    d = x_ref.shape[1]
    yf = jnp.dot(of_ref[...], wf_ref[...], preferred_element_type=F32)
    ym = jnp.dot(om_ref[...], wm_ref[...], preferred_element_type=F32)
    mixed = g_ref[:, :d].astype(F32) * yf + g_ref[:, d:].astype(F32) * ym
    x2 = x_ref[...] + jnp.dot(mixed.astype(BF16), wo_ref[...], preferred_element_type=F32)
    x2_ref[...] = x2
    ms = jnp.mean(x2 * x2, axis=-1, keepdims=True)
    h2 = x2 * lax.rsqrt(ms + RMS_EPS) * gn_ref[...]
    _store_token_tiles(h2_ref, h2)
    lg_ref[...] = jnp.dot(h2.astype(BF16), wr_ref[...], preferred_element_type=F32) + br_ref[...]


def _mix(o_fox, o_moba, gates, x, w_f, w_m, w_o, g_ffn, w_r, b_r, tm):
    t, d = x.shape
    w = o_fox.shape[1]
    const = lambda i: (0, 0)
    resident = dict(pipeline_mode=pl.Buffered(1))
    return pl.pallas_call(
        _mix_kernel,
        out_shape=(jax.ShapeDtypeStruct((t, d), F32), jax.ShapeDtypeStruct((t * (d // LANES), LANES), F32),
                   jax.ShapeDtypeStruct((t, LANES), F32)),
        grid=(t // tm,),
        in_specs=[
            pl.BlockSpec((tm, w), lambda i: (i, 0)),
            pl.BlockSpec((tm, w), lambda i: (i, 0)),
            pl.BlockSpec((tm, 2 * d), lambda i: (i, 0)),
            pl.BlockSpec((tm, d), lambda i: (i, 0)),
            pl.BlockSpec((w, d), const, **resident),
            pl.BlockSpec((w, d), const, **resident),
            pl.BlockSpec((d, d), const, **resident),
            pl.BlockSpec((1, d), const),
            pl.BlockSpec((d, LANES), const),
            pl.BlockSpec((1, LANES), const),
        ],
        out_specs=(pl.BlockSpec((tm, d), lambda i: (i, 0)), pl.BlockSpec((tm * (d // LANES), LANES), lambda i: (i, 0)),
                   pl.BlockSpec((tm, LANES), lambda i: (i, 0))),
        compiler_params=_params(("parallel",)),
        name="mix_out_router",
    )(o_fox, o_moba, gates, x, w_f, w_m, w_o, g_ffn, w_r, b_r)


def _router_kernel(lg_ref, meta_ref, gate_ref, cnt_ref, carry_ref):
    @pl.when(pl.program_id(0) == 0)
    def _():
        carry_ref[...] = jnp.zeros_like(carry_ref)

    l = lg_ref[...]
    tr = l.shape[0]
    lane = lax.broadcasted_iota(jnp.int32, (tr, LANES), 1).astype(F32)
    vals, idxs = [], []
    for _ in range(TOP_K):
        m = jnp.max(l, axis=1, keepdims=True)
        ix = jnp.min(jnp.where(l == m, lane, float(LANES)), axis=1, keepdims=True)
        vals.append(m)
        idxs.append(ix)
        l = jnp.where(lane == ix, NEG_INF, l)
    ex = [jnp.exp(v - vals[0]) for v in vals]
    den = ex[0] + ex[1] + ex[2] + ex[3]
    onehot = jnp.zeros((tr, LANES), F32)
    for ix in idxs:
        onehot = onehot + (lane == ix).astype(F32)
    row = lax.broadcasted_iota(jnp.int32, (tr, tr), 0)
    col = lax.broadcasted_iota(jnp.int32, (tr, tr), 1)
    before = (col < row).astype(BF16)
    running = jnp.dot(before, onehot.astype(BF16), preferred_element_type=F32) + carry_ref[...]
    meta = jnp.zeros((tr, LANES), F32)
    gate = jnp.zeros((tr, LANES), F32)
    for k in range(TOP_K):
        pos = jnp.sum(jnp.where(lane == idxs[k], running, 0.0), axis=1, keepdims=True)
        meta = jnp.where(lane == float(k), idxs[k], meta)
        meta = jnp.where(lane == float(TOP_K + k), pos, meta)
        gate = jnp.where(lane == float(k), ex[k] / den, gate)
    meta_ref[...] = meta.astype(jnp.int32)
    gate_ref[...] = gate
    carry_ref[...] += jnp.sum(onehot, axis=0, keepdims=True)
    cnt_ref[...] = carry_ref[...]


def _router(logits, tr):
    t = logits.shape[0]
    return pl.pallas_call(
        _router_kernel,
        out_shape=(jax.ShapeDtypeStruct((t, LANES), jnp.int32), jax.ShapeDtypeStruct((t, LANES), F32),
                   jax.ShapeDtypeStruct((1, LANES), F32)),
        grid=(t // tr,),
        in_specs=[pl.BlockSpec((tr, LANES), lambda i: (i, 0))],
        out_specs=(pl.BlockSpec((tr, LANES), lambda i: (i, 0)), pl.BlockSpec((tr, LANES), lambda i: (i, 0)),
                   pl.BlockSpec((1, LANES), lambda i: (0, 0))),
        scratch_shapes=[pltpu.VMEM((1, LANES), F32)],
        compiler_params=_params(("arbitrary",)),
        name="router_topk",
    )(logits)


def _gather_kernel(nused_ref, tok_ref, tok_next_ref, h_hbm, o_ref, buf, sem, *, nj):
    r = pl.program_id(0)
    m = o_ref.shape[0]
    slot = r % 2

    def issue(tok, s):
        def body(i, c):
            _token_copy(h_hbm, tok[0, 0, i], buf.at[s], i, nj, sem.at[s]).start()
            return c

        lax.fori_loop(0, m, body, 0, unroll=8)

    @pl.when(r == 0)
    def _():
        issue(tok_ref, 0)

    @pl.when(r + 1 < nused_ref[0])
    def _():
        issue(tok_next_ref, 1 - slot)

    @pl.when(r < nused_ref[0])
    def _():
        pltpu.make_async_copy(h_hbm.at[pl.ds(0, m * nj)], buf.at[slot, pl.ds(0, m * nj)], sem.at[slot]).wait()
        for j in range(nj):
            o_ref[:, j * LANES:(j + 1) * LANES] = _load_token_cols(buf.at[slot], m, nj, j).astype(o_ref.dtype)

    @pl.when(r >= nused_ref[0])
    def _():
        o_ref[...] = jnp.zeros_like(o_ref)


def _gather_rows(h_tiles, row_tok, n_used, m, nj):
    n_blocks = row_tok.shape[0] // m
    tok3 = row_tok.reshape(n_blocks, 1, m)
    return pl.pallas_call(
        functools.partial(_gather_kernel, nj=nj),
        out_shape=jax.ShapeDtypeStruct((n_blocks * m, nj * LANES), BF16),
        grid_spec=pltpu.PrefetchScalarGridSpec(
            num_scalar_prefetch=1,
            grid=(n_blocks,),
            in_specs=[
                pl.BlockSpec((1, 1, m), lambda r, nu: (r, 0, 0), memory_space=pltpu.SMEM),
                pl.BlockSpec((1, 1, m), lambda r, nu: (jnp.minimum(r + 1, n_blocks - 1), 0, 0),
                             memory_space=pltpu.SMEM),
                pl.BlockSpec(memory_space=pl.ANY),
            ],
            out_specs=pl.BlockSpec((m, nj * LANES), lambda r, nu: (r, 0)),
            scratch_shapes=[pltpu.VMEM((2, m * _token_pitch(nj), LANES), F32), pltpu.SemaphoreType.DMA((2,))],
        ),
        compiler_params=_params(("arbitrary",), disable_bounds_checks=True),
        name="expert_row_gather",
    )(n_used, tok3, tok3, h_tiles)


def _group_start(be_ref, r):
    return jnp.logical_or(r == 0, be_ref[r] != be_ref[jnp.maximum(r - 1, 0)])


def _gmm1_kernel(be_ref, nused_ref, x_ref, w_ref, perm_ref, bg_ref, bl_ref, a_ref, wg_s, wl_s):
    r = pl.program_id(1)
    pw = perm_ref.shape[0]

    @pl.when(r < nused_ref[0])
    def _():
        @pl.when(_group_start(be_ref, r))
        def _():
            for c in range(w_ref.shape[1] // pw):
                wb = w_ref[:, c * pw:(c + 1) * pw].astype(BF16)
                sp = jnp.dot(wb, perm_ref[...], preferred_element_type=F32).astype(BF16)
                wg_s[:, c * (pw // 2):(c + 1) * (pw // 2)] = sp[:, :pw // 2]
                wl_s[:, c * (pw // 2):(c + 1) * (pw // 2)] = sp[:, pw // 2:]

        x = x_ref[...]
        ug = jnp.dot(x, wg_s[...], preferred_element_type=F32) + bg_ref[...]
        ul = jnp.dot(x, wl_s[...], preferred_element_type=F32) + bl_ref[...]
        glu = jnp.minimum(ug, SWIGLU_LIMIT)
        lin = jnp.clip(ul, -SWIGLU_LIMIT, SWIGLU_LIMIT)
        a_ref[...] = (glu * jax.nn.sigmoid(SWIGLU_ALPHA * glu) * (lin + 1.0)).astype(a_ref.dtype)

    @pl.when(r >= nused_ref[0])
    def _():
        a_ref[...] = jnp.zeros_like(a_ref)


def _deinterleave_perm(width):
    src = np.arange(width)
    dst = np.where(src % 2 == 0, src // 2, width // 2 + src // 2)
    p = np.zeros((width, width), np.float32)
    p[src, dst] = 1.0
    return jnp.asarray(p, BF16)


def _gmm1(xb, w1, b1g, b1l, block_e, n_used, m, tn):
    rows, d = xb.shape
    f = w1.shape[2] // 2
    n_blocks = rows // m
    pw = 2 * LANES

    def rc(r, nu):
        return jnp.minimum(r, nu[0] - 1)

    return pl.pallas_call(
        _gmm1_kernel,
        out_shape=jax.ShapeDtypeStruct((rows, f), BF16),
        grid_spec=pltpu.PrefetchScalarGridSpec(
            num_scalar_prefetch=2,
            grid=(f // tn, n_blocks),
            in_specs=[
                pl.BlockSpec((m, d), lambda j, r, be, nu: (rc(r, nu), 0)),
                pl.BlockSpec((None, d, 2 * tn), lambda j, r, be, nu: (be[rc(r, nu)], 0, j)),
                pl.BlockSpec((pw, pw), lambda j, r, be, nu: (0, 0)),
                pl.BlockSpec((None, 1, tn), lambda j, r, be, nu: (be[rc(r, nu)], 0, j)),
                pl.BlockSpec((None, 1, tn), lambda j, r, be, nu: (be[rc(r, nu)], 0, j)),
            ],
            out_specs=pl.BlockSpec((m, tn), lambda j, r, be, nu: (r, j)),
            scratch_shapes=[pltpu.VMEM((d, tn), BF16), pltpu.VMEM((d, tn), BF16)],
        ),
        compiler_params=_params(("arbitrary", "arbitrary")),
        name="expert_mlp1_swiglu",
    )(block_e, n_used, xb, w1, _deinterleave_perm(pw), b1g, b1l)


def _gmm2_kernel(be_ref, nused_ref, a_ref, w_ref, b_ref, y_ref, w_s):
    r = pl.program_id(0)

    @pl.when(r < nused_ref[0])
    def _():
        @pl.when(_group_start(be_ref, r))
        def _():
            w_s[...] = w_ref[...].astype(BF16)

        _store_token_tiles(y_ref, jnp.dot(a_ref[...], w_s[...], preferred_element_type=F32) + b_ref[...])

    @pl.when(r >= nused_ref[0])
    def _():
        y_ref[...] = jnp.zeros_like(y_ref)


def _gmm2(a, w2, b2, block_e, n_used, m):
    rows, f = a.shape
    d = w2.shape[2]
    nj = d // LANES
    n_blocks = rows // m

    def rc(r, nu):
        return jnp.minimum(r, nu[0] - 1)

    return pl.pallas_call(
        _gmm2_kernel,
        out_shape=jax.ShapeDtypeStruct((rows * nj, LANES), F32),
        grid_spec=pltpu.PrefetchScalarGridSpec(
            num_scalar_prefetch=2,
            grid=(n_blocks,),
            in_specs=[
                pl.BlockSpec((m, f), lambda r, be, nu: (rc(r, nu), 0)),
                pl.BlockSpec((None, f, d), lambda r, be, nu: (be[rc(r, nu)], 0, 0), pipeline_mode=pl.Buffered(1)),
                pl.BlockSpec((None, 1, d), lambda r, be, nu: (be[rc(r, nu)], 0, 0)),
            ],
            out_specs=pl.BlockSpec((m * nj, LANES), lambda r, be, nu: (r, 0)),
            scratch_shapes=[pltpu.VMEM((f, d), BF16)],
        ),
        compiler_params=_params(("arbitrary",)),
        name="expert_mlp2",
    )(block_e, n_used, a, w2, b2)


def _combine_kernel(dest_ref, dest_next_ref, y_hbm, gate_ref, x2_ref, g_ref, o_ref, buf, sem, *, final_norm):
    i = pl.program_id(0)
    tc, d = x2_ref.shape
    nj = d // LANES
    slot = i % 2

    def issue(dest, s):
        def body(t, c):
            for k in range(TOP_K):
                _token_copy(y_hbm, dest[0, 0, t * TOP_K + k], buf.at[s, k], t, nj, sem.at[s]).start()
            return c

        lax.fori_loop(0, tc, body, 0, unroll=2)

    @pl.when(i == 0)
    def _():
        issue(dest_ref, 0)

    @pl.when(i + 1 < pl.num_programs(0))
    def _():
        issue(dest_next_ref, 1 - slot)

    for k in range(TOP_K):
        pltpu.make_async_copy(y_hbm.at[pl.ds(0, tc * nj)], buf.at[slot, k, pl.ds(0, tc * nj)], sem.at[slot]).wait()
    cols = []
    for j in range(nj):
        acc = x2_ref[:, j * LANES:(j + 1) * LANES]
        for k in range(TOP_K):
            acc = acc + gate_ref[:, k:k + 1] * _load_token_cols(buf.at[slot, k], tc, nj, j)
        cols.append(acc)
    acc = jnp.concatenate(cols, axis=1)
    if final_norm:
        ms = jnp.mean(acc * acc, axis=-1, keepdims=True)
        acc = acc * lax.rsqrt(ms + RMS_EPS) * g_ref[...]
    o_ref[...] = acc


def _combine(y3, dest, gates, x2, g_final, tc, final_norm):
    t, d = x2.shape
    nt = t // tc
    dest3 = dest.reshape(nt, 1, tc * TOP_K)
    return pl.pallas_call(
        functools.partial(_combine_kernel, final_norm=final_norm),
        out_shape=jax.ShapeDtypeStruct((t, d), F32),
        grid=(nt,),
        in_specs=[
            pl.BlockSpec((1, 1, tc * TOP_K), lambda i: (i, 0, 0), memory_space=pltpu.SMEM),
            pl.BlockSpec((1, 1, tc * TOP_K), lambda i: (jnp.minimum(i + 1, nt - 1), 0, 0),
                         memory_space=pltpu.SMEM),
            pl.BlockSpec(memory_space=pl.ANY),
            pl.BlockSpec((tc, LANES), lambda i: (i, 0)),
            pl.BlockSpec((tc, d), lambda i: (i, 0)),
            pl.BlockSpec((1, d), lambda i: (0, 0)),
        ],
        out_specs=pl.BlockSpec((tc, d), lambda i: (i, 0)),
        scratch_shapes=[pltpu.VMEM((2, TOP_K, tc * _token_pitch(d // LANES), LANES), F32),
                        pltpu.SemaphoreType.DMA((2,))],
        compiler_params=_params(("arbitrary",), disable_bounds_checks=True),
        name="moe_combine_final_norm",
    )(dest3, dest3, y3, gates, x2, g_final.reshape(1, d))


def _layer(x, norm_mix_g, w_in, b_forget, w_branch_fox, w_branch_moba, w_out, norm_ffn_g,
           w_router, b_router, w_mlp1, b_mlp1, w_mlp2, b_mlp2):
    b, s, d = x.shape
    t = b * s
    width = N_HEADS * HEAD_DIM
    xt = x.reshape(t, d)
    scale = HEAD_DIM ** -0.5

    o = np.cumsum([0, width, width, width, N_HEADS, width, width, width, d, d])
    w_fox = w_in[:, o[0]:o[3]].astype(BF16)
    w_fl = jnp.pad(w_in[:, o[3]:o[4]], ((0, 0), (0, LANES - N_HEADS))).astype(BF16)
    w_mqk = w_in[:, o[4]:o[6]].astype(BF16)
    w_mv = w_in[:, o[6]:o[7]].astype(BF16)
    w_g = w_in[:, o[7]:o[9]].astype(BF16)
    q_scale = jnp.concatenate([jnp.full((1, width), scale, F32), jnp.ones((1, 2 * width), F32)], axis=1)

    tm = min(1024, s)
    h = _rmsnorm(xt, norm_mix_g, tm=min(512, t))
    fox = _proj(h, w_fox, q_scale, seq=s, tm=tm, tn=512, out_dtype=BF16, heads=True, name="proj_fox_qkv")
    mqk = _proj(h, w_mqk, q_scale[:, :2 * width], seq=s, tm=tm, tn=512, out_dtype=BF16, heads=True,
                rope=_rope_tables(s), name="proj_moba_qk")
    mv = _proj(h, w_mv, q_scale[:, width:2 * width], seq=s, tm=tm, tn=512, out_dtype=BF16, heads=True,
               name="proj_moba_v")
    gates = _proj(h, w_g, jnp.ones((1, 2 * d), F32), seq=s, tm=tm, tn=512, out_dtype=BF16, heads=False,
                  sigmoid=True, name="proj_gates")
    fl = _proj(h, w_fl, jnp.ones((1, LANES), F32), seq=s, tm=tm, tn=LANES, out_dtype=F32, heads=False,
               name="proj_forget")

    fl_t = fl[:, :N_HEADS].reshape(b, s, N_HEADS).transpose(0, 2, 1)
    c = _forget_cumsum(fl_t, b_forget)
    c5 = c.reshape(b, N_HEADS, s // MOBA_BLOCK, 1, MOBA_BLOCK)

    o_fox = _attention(fox, 0, fox, N_HEADS, fox, 2 * N_HEADS, c5, moba=False, name="fox_attention")
    o_moba = _attention(mqk, 0, mqk, N_HEADS, mv, 0, None, moba=True, name="moba_attention")

    w_r = jnp.pad(w_router, ((0, 0), (0, LANES - N_EXPERTS))).astype(BF16)
    b_r = jnp.concatenate([b_router.astype(F32), jnp.full((LANES - N_EXPERTS,), NEG_INF, F32)]).reshape(1, LANES)
    x2, h2, logits = _mix(o_fox, o_moba, gates, xt, w_branch_fox.astype(BF16), w_branch_moba.astype(BF16),
                          w_out.astype(BF16), norm_ffn_g.reshape(1, d), w_r, b_r, tm=min(256, t))

    meta, gate_w, cnt = _router(logits, tr=min(512, t))

    m = MOE_ROWS
    assign = t * TOP_K
    n_blocks = (assign + N_EXPERTS * (m - 1) + m - 1) // m
    counts = cnt[0, :N_EXPERTS].astype(jnp.int32)
    padded = (counts + m - 1) // m * m
    pend = jnp.cumsum(padded)
    pstart = pend - padded
    dest = pstart[meta[:, :TOP_K]] + meta[:, TOP_K:2 * TOP_K]
    n_used = (pend[-1] // m).astype(jnp.int32).reshape(1)
    block_e = jnp.minimum(jnp.searchsorted(pend, jnp.arange(n_blocks, dtype=jnp.int32) * m, side='right'),
                          N_EXPERTS - 1).astype(jnp.int32)
    row_tok = jnp.zeros((n_blocks * m,), jnp.int32).at[dest.reshape(-1)].set(
        jnp.arange(assign, dtype=jnp.int32) // TOP_K)

    xb = _gather_rows(h2, row_tok, n_used, m, d // LANES)
    f = w_mlp1.shape[2] // 2
    b1g = b_mlp1[:, 0::2].reshape(N_EXPERTS, 1, f)
    b1l = b_mlp1[:, 1::2].reshape(N_EXPERTS, 1, f)
    a = _gmm1(xb, w_mlp1, b1g, b1l, block_e, n_used, m, tn=min(512, f))
    y = _gmm2(a, w_mlp2, b_mlp2.reshape(N_EXPERTS, 1, d), block_e, n_used, m)
    return x2, y, dest, gate_w


def kernel(x, norm_mix_g, w_in, b_forget, w_branch_fox, w_branch_moba, w_out, norm_ffn_g, w_router, b_router,
           w_mlp1, b_mlp1, w_mlp2, b_mlp2, norm_final_g):
    depth = w_in.shape[0]
    b, s, d = x.shape
    for l in range(depth):
        x2, y, dest, gate_w = _layer(x, norm_mix_g[l], w_in[l], b_forget[l], w_branch_fox[l], w_branch_moba[l],
                                     w_out[l], norm_ffn_g[l], w_router[l], b_router[l], w_mlp1[l], b_mlp1[l],
                                     w_mlp2[l], b_mlp2[l])
        out = _combine(y, dest, gate_w, x2, norm_final_g, tc=min(128, b * s), final_norm=l == depth - 1)
        x = out.reshape(b, s, d)
    return x
```

```python
import functools
import math

import jax
import jax.numpy as jnp
import numpy as np
from jax import lax
from jax.experimental import pallas as pl
from jax.experimental.pallas import tpu as pltpu

HEAD_DIM = 128
N_HEADS = 8
MOBA_BLOCK = 256
MOBA_TOPK = 3
ROPE_THETA = 500000.0
ROPE_DIM = HEAD_DIM // 4
N_EXPERTS = 32
TOP_K = 4
SWIGLU_ALPHA = 1.702
SWIGLU_LIMIT = 7.0
RMS_EPS = 1e-5

LANES = 128
SUBLANES = 8
VMEM_LIMIT_BYTES = 56 * 1024 * 1024
MOE_ROWS = 512
ATTN_Q_ROWS = 256

F32 = jnp.float32
BF16 = jnp.bfloat16
U32 = jnp.uint32
NEG_INF = float("-inf")
MASKED = -1e30
LOG2E = math.log2(math.e)


def _params(sem, **kw):
    return pltpu.CompilerParams(dimension_semantics=sem, vmem_limit_bytes=VMEM_LIMIT_BYTES, **kw)


def _pack_pairs(lo, hi):
    lo_b = lax.bitcast_convert_type(lo.astype(BF16).astype(F32), U32) >> 16
    hi_b = lax.bitcast_convert_type(hi.astype(BF16).astype(F32), U32) & jnp.uint32(0xFFFF0000)
    return hi_b | lo_b


def _unpack_pairs(w):
    lo = lax.bitcast_convert_type(w << 16, F32)
    hi = lax.bitcast_convert_type(w & jnp.uint32(0xFFFF0000), F32)
    return lo, hi


def _store_token_tiles(ref, val):
    n, d = val.shape
    njp = d // (2 * LANES)
    for j in range(njp):
        lo = val[:, j * LANES:(j + 1) * LANES]
        hi = val[:, (njp + j) * LANES:(njp + j + 1) * LANES]
        ref[pl.ds(j, n, stride=njp), :] = _pack_pairs(lo, hi)


def _token_pitch(njp):
    p = -(-njp // 4)
    return 4 * (p if p % 2 else p + 1)


def _token_copy(src_hbm, row, dst_vmem, slot, njp, sem):
    return pltpu.make_async_copy(src_hbm.at[pl.ds(pl.multiple_of(row * njp, njp), njp)],
                                 dst_vmem.at[pl.ds(slot * _token_pitch(njp), njp)], sem)


def _load_token_cols(buf, n, njp, j):
    return _unpack_pairs(buf[pl.ds(j, n, stride=_token_pitch(njp)), :])


def _rmsnorm_kernel(x_ref, g_ref, o_ref):
    x = x_ref[...]
    ms = jnp.mean(x * x, axis=-1, keepdims=True)
    o_ref[...] = (x * lax.rsqrt(ms + RMS_EPS) * g_ref[...]).astype(o_ref.dtype)


def _rmsnorm(x, g, tm):
    t, d = x.shape
    return pl.pallas_call(
        _rmsnorm_kernel,
        out_shape=jax.ShapeDtypeStruct((t, d), BF16),
        grid=(t // tm,),
        in_specs=[pl.BlockSpec((tm, d), lambda i: (i, 0)), pl.BlockSpec((1, d), lambda i: (0, 0))],
        out_specs=pl.BlockSpec((tm, d), lambda i: (i, 0)),
        compiler_params=_params(("parallel",)),
        name="rmsnorm_mix",
    )(x, g.reshape(1, d))


def _proj_kernel(a_ref, w_ref, cs_ref, *rest, rotary, sigmoid, heads):
    if rotary:
        cos_ref, sa_ref, sb_ref, o_ref = rest
    else:
        (o_ref,) = rest
    res = jnp.dot(a_ref[...], w_ref[...], preferred_element_type=F32) * cs_ref[...]
    if sigmoid:
        res = jax.nn.sigmoid(res)
    for j in range(res.shape[1] // LANES):
        r = res[:, j * LANES:(j + 1) * LANES]
        if rotary:
            half = ROPE_DIM // 2
            r = (r * cos_ref[...] + pltpu.roll(r, half, 1) * sa_ref[...]
                 + pltpu.roll(r, LANES - half, 1) * sb_ref[...])
        if heads:
            o_ref[0, j] = r.astype(o_ref.dtype)
        else:
            o_ref[:, j * LANES:(j + 1) * LANES] = r.astype(o_ref.dtype)


def _proj(a, w, colscale, *, seq, tm, tn, out_dtype, heads, sigmoid=False, rope=None, name):
    t, k = a.shape
    n = w.shape[1]
    sblk = seq // tm
    in_specs = [
        pl.BlockSpec((tm, k), lambda m, j: (m, 0)),
        pl.BlockSpec((k, tn), lambda m, j: (0, j)),
        pl.BlockSpec((1, tn), lambda m, j: (0, j)),
    ]
    args = [a, w, colscale]
    if rope is not None:
        in_specs += [pl.BlockSpec((tm, LANES), lambda m, j: (m % sblk, 0))] * 3
        args += list(rope)
    if heads:
        out_shape = jax.ShapeDtypeStruct((t // seq, n // LANES, seq, LANES), out_dtype)
        out_spec = pl.BlockSpec((1, tn // LANES, tm, LANES), lambda m, j: (m // sblk, j, m % sblk, 0))
    else:
        out_shape = jax.ShapeDtypeStruct((t, n), out_dtype)
        out_spec = pl.BlockSpec((tm, tn), lambda m, j: (m, j))
    return pl.pallas_call(
        functools.partial(_proj_kernel, rotary=rope is not None, sigmoid=sigmoid, heads=heads),
        out_shape=out_shape,
        grid=(t // tm, n // tn),
        in_specs=in_specs,
        out_specs=out_spec,
        compiler_params=_params(("parallel", "arbitrary")),
        name=name,
    )(*args)


def _rope_tables(seq):
    half = ROPE_DIM // 2
    inv_freq = 1.0 / (ROPE_THETA ** (jnp.arange(half, dtype=F32) / half))
    ang = jnp.arange(seq, dtype=F32)[:, None] * inv_freq[None, :]
    cos, sin = jnp.cos(ang), jnp.sin(ang)
    ones = jnp.ones((seq, LANES - ROPE_DIM), F32)
    zeros = jnp.zeros((seq, LANES - ROPE_DIM), F32)
    zh = jnp.zeros((seq, half), F32)
    cos_t = jnp.concatenate([cos, cos, ones], axis=1)
    sin_a = jnp.concatenate([zh, sin, zeros], axis=1)
    sin_b = jnp.concatenate([-sin, zh, zeros], axis=1)
    return cos_t, sin_a, sin_b


def _forget_kernel(fl_ref, b_ref, c_ref):
    z = fl_ref[...] + b_ref[...]
    logf = jnp.minimum(z, 0.0) - jnp.log1p(jnp.exp(-jnp.abs(z)))
    row = lax.broadcasted_iota(jnp.int32, (LANES, LANES), 0)
    col = lax.broadcasted_iota(jnp.int32, (LANES, LANES), 1)
    tri = (row <= col).astype(BF16)
    carry = jnp.zeros((logf.shape[0], 1), F32)
    for j in range(logf.shape[1] // LANES):
        xb = logf[:, j * LANES:(j + 1) * LANES]
        hi = xb.astype(BF16)
        r1 = xb - hi.astype(F32)
        mid = r1.astype(BF16)
        lo = (r1 - mid.astype(F32)).astype(BF16)
        cs = (jnp.dot(hi, tri, preferred_element_type=F32) + jnp.dot(mid, tri, preferred_element_type=F32)
              + jnp.dot(lo, tri, preferred_element_type=F32)) + carry
        c_ref[:, j * LANES:(j + 1) * LANES] = cs * LOG2E
        carry = cs[:, LANES - 1:LANES]


def _forget_cumsum(fl_t, b_forget):
    b, h, s = fl_t.shape
    return pl.pallas_call(
        _forget_kernel,
        out_shape=jax.ShapeDtypeStruct((b, h, s), F32),
        grid=(b,),
        in_specs=[pl.BlockSpec((None, h, s), lambda i: (i, 0, 0)), pl.BlockSpec((h, 1), lambda i: (0, 0))],
        out_specs=pl.BlockSpec((None, h, s), lambda i: (i, 0, 0)),
        compiler_params=_params(("parallel",)),
        name="forget_cumsum",
    )(fl_t, b_forget.reshape(h, 1).astype(F32))


def _attn_kernel(*refs, moba, tq, tk):
    if moba:
        q_ref, k_ref, v_ref, o_ref, km_ref, kaug_ref = refs
    else:
        q_ref, k_ref, v_ref, c_ref, o_ref = refs
    nkb = q_ref.shape[0] // tk
    nt = (((1,), (1,)), ((), ()))
    row = lax.broadcasted_iota(jnp.int32, (tq, tk), 0)
    col = lax.broadcasted_iota(jnp.int32, (tq, tk), 1)

    if moba:
        lane = lax.broadcasted_iota(jnp.int32, (tk, LANES), 1)
        km_ref[...] = jnp.zeros_like(km_ref)
        for n in range(nkb):
            kb = k_ref[n * tk:(n + 1) * tk, :]
            km_ref[n:n + 1, :] = jnp.mean(kb.astype(F32), axis=0, keepdims=True)
            kaug_ref[n * tk:(n + 1) * tk, :LANES] = kb
            kaug_ref[n * tk:(n + 1) * tk, LANES:] = (lane == n).astype(BF16)
        kmean = km_ref[...].astype(BF16)
        blk = lax.broadcasted_iota(jnp.int32, (SUBLANES, tk), 0)

    def keys(n):
        return (kaug_ref if moba else k_ref)[n * tk:(n + 1) * tk, :]

    for nb in range(nkb):
        if moba and nb > 0:
            g = lax.dot_general(kmean, q_ref[nb * tk:(nb + 1) * tk, :], nt, preferred_element_type=F32)[:SUBLANES]
            past = blk < nb
            pen = jnp.zeros((SUBLANES, tk), F32)
            for n in range(nb):
                g_n = g[n:n + 1, :]
                beats = ((g > g_n) | ((g == g_n) & (blk < n))) & past
                rank = jnp.sum(beats.astype(F32), axis=0, keepdims=True)
                pen = jnp.where((blk == n) & (rank >= MOBA_TOPK), MASKED, pen)
            pen = jnp.concatenate([pen, jnp.zeros((LANES - SUBLANES, tk), F32)], axis=0)
            pen_cols = pen.T.astype(BF16)
        for h in range(tk // tq):
            r0 = nb * tk + h * tq
            q = q_ref[r0:r0 + tq, :]
            s = lax.dot_general(q, k_ref[nb * tk:(nb + 1) * tk, :], nt, preferred_element_type=F32)
            if not moba:
                s = s - c_ref[nb]
            s = jnp.where(row + h * tq >= col, s, NEG_INF)
            m = jnp.max(s, axis=1, keepdims=True)
            p = jnp.exp2(s - m)
            l = jnp.sum(p, axis=1, keepdims=True)
            acc = jnp.dot(p.astype(BF16), v_ref[nb * tk:(nb + 1) * tk, :], preferred_element_type=F32)
            if moba and nb > 0:
                q = jnp.concatenate([q, pen_cols[h * tq:(h + 1) * tq, :]], axis=1)
            for n in range(nb):
                s = lax.dot_general(q, keys(n), nt, preferred_element_type=F32)
                if not moba:
                    s = s - c_ref[n]
                m_new = jnp.maximum(m, jnp.max(s, axis=1, keepdims=True))
                alpha = jnp.exp2(m - m_new)
                p = jnp.exp2(s - m_new)
                l = alpha * l + jnp.sum(p, axis=1, keepdims=True)
                acc = alpha * acc + jnp.dot(p.astype(BF16), v_ref[n * tk:(n + 1) * tk, :],
                                            preferred_element_type=F32)
                m = m_new
            o_ref[r0:r0 + tq, :] = (acc / l).astype(o_ref.dtype)


def _attention(q_arr, q_off, k_arr, k_off, v_arr, v_off, c5, *, moba, name):
    b, _, s, d = q_arr.shape
    tk = MOBA_BLOCK
    tq = min(ATTN_Q_ROWS, tk)
    in_specs = [
        pl.BlockSpec((None, None, s, d), lambda bi, h: (bi, q_off + h, 0, 0)),
        pl.BlockSpec((None, None, s, d), lambda bi, h: (bi, k_off + h, 0, 0)),
        pl.BlockSpec((None, None, s, d), lambda bi, h: (bi, v_off + h, 0, 0)),
    ]
    args = [q_arr, k_arr, v_arr]
    scratch = []
    if moba:
        scratch = [pltpu.VMEM((LANES, d), F32), pltpu.VMEM((s, d + LANES), BF16)]
    else:
        in_specs.append(pl.BlockSpec((None, None, s // tk, 1, tk), lambda bi, h: (bi, h, 0, 0, 0)))
        args.append(c5)
    return pl.pallas_call(
        functools.partial(_attn_kernel, moba=moba, tq=tq, tk=tk),
        out_shape=jax.ShapeDtypeStruct((b * s, N_HEADS * d), BF16),
        grid=(b, N_HEADS),
        in_specs=in_specs,
        out_specs=pl.BlockSpec((s, d), lambda bi, h: (bi, h)),
        scratch_shapes=scratch,
        compiler_params=_params(("parallel", "parallel")),
        name=name,
    )(*args)


def _mix_kernel(of_ref, om_ref, g_ref, x_ref, wf_ref, wm_ref, wo_ref, gn_ref, wr_ref, br_ref,
                x2_ref, h2_ref, lg_ref):
    d = x_ref.shape[1]
    yf = jnp.dot(of_ref[...], wf_ref[...], preferred_element_type=F32)
    ym = jnp.dot(om_ref[...], wm_ref[...], preferred_element_type=F32)
    mixed = g_ref[:, :d].astype(F32) * yf + g_ref[:, d:].astype(F32) * ym
    x2 = x_ref[...] + jnp.dot(mixed.astype(BF16), wo_ref[...], preferred_element_type=F32)
    x2_ref[...] = x2
    ms = jnp.mean(x2 * x2, axis=-1, keepdims=True)
    h2 = x2 * lax.rsqrt(ms + RMS_EPS) * gn_ref[...]
    _store_token_tiles(h2_ref, h2)
    lg_ref[...] = jnp.dot(h2.astype(BF16), wr_ref[...], preferred_element_type=F32) + br_ref[...]


def _mix(o_fox, o_moba, gates, x, w_f, w_m, w_o, g_ffn, w_r, b_r, tm):
    t, d = x.shape
    w = o_fox.shape[1]
    njp = d // (2 * LANES)
    const = lambda i: (0, 0)
    resident = dict(pipeline_mode=pl.Buffered(1))
    return pl.pallas_call(
        _mix_kernel,
        out_shape=(jax.ShapeDtypeStruct((t, d), F32), jax.ShapeDtypeStruct((t * njp, LANES), U32),
                   jax.ShapeDtypeStruct((t, LANES), F32)),
        grid=(t // tm,),
        in_specs=[
            pl.BlockSpec((tm, w), lambda i: (i, 0)),
            pl.BlockSpec((tm, w), lambda i: (i, 0)),
            pl.BlockSpec((tm, 2 * d), lambda i: (i, 0)),
            pl.BlockSpec((tm, d), lambda i: (i, 0)),
            pl.BlockSpec((w, d), const, **resident),
            pl.BlockSpec((w, d), const, **resident),
            pl.BlockSpec((d, d), const, **resident),
            pl.BlockSpec((1, d), const),
            pl.BlockSpec((d, LANES), const),
            pl.BlockSpec((1, LANES), const),
        ],
        out_specs=(pl.BlockSpec((tm, d), lambda i: (i, 0)), pl.BlockSpec((tm * njp, LANES), lambda i: (i, 0)),
                   pl.BlockSpec((tm, LANES), lambda i: (i, 0))),
        compiler_params=_params(("parallel",)),
        name="mix_out_router",
    )(o_fox, o_moba, gates, x, w_f, w_m, w_o, g_ffn, w_r, b_r)


def _router_kernel(lg_ref, meta_ref, gate_ref, cnt_ref, carry_ref):
    @pl.when(pl.program_id(0) == 0)
    def _():
        carry_ref[...] = jnp.zeros_like(carry_ref)

    l = lg_ref[...]
    tr = l.shape[0]
    lane = lax.broadcasted_iota(jnp.int32, (tr, LANES), 1).astype(F32)
    vals, idxs = [], []
    for _ in range(TOP_K):
        m = jnp.max(l, axis=1, keepdims=True)
        ix = jnp.min(jnp.where(l == m, lane, float(LANES)), axis=1, keepdims=True)
        vals.append(m)
        idxs.append(ix)
        l = jnp.where(lane == ix, NEG_INF, l)
    ex = [jnp.exp(v - vals[0]) for v in vals]
    den = ex[0] + ex[1] + ex[2] + ex[3]
    onehot = jnp.zeros((tr, LANES), F32)
    for ix in idxs:
        onehot = onehot + (lane == ix).astype(F32)
    row = lax.broadcasted_iota(jnp.int32, (tr, tr), 0)
    col = lax.broadcasted_iota(jnp.int32, (tr, tr), 1)
    before = (col < row).astype(BF16)
    running = jnp.dot(before, onehot.astype(BF16), preferred_element_type=F32) + carry_ref[...]
    meta = jnp.zeros((tr, LANES), F32)
    gate = jnp.zeros((tr, LANES), F32)
    for k in range(TOP_K):
        pos = jnp.sum(jnp.where(lane == idxs[k], running, 0.0), axis=1, keepdims=True)
        meta = jnp.where(lane == float(k), idxs[k], meta)
        meta = jnp.where(lane == float(TOP_K + k), pos, meta)
        gate = jnp.where(lane == float(k), ex[k] / den, gate)
    meta_ref[...] = meta.astype(jnp.int32)
    gate_ref[...] = gate
    carry_ref[...] += jnp.sum(onehot, axis=0, keepdims=True)
    cnt_ref[...] = carry_ref[...]


def _router(logits, tr):
    t = logits.shape[0]
    return pl.pallas_call(
        _router_kernel,
        out_shape=(jax.ShapeDtypeStruct((t, LANES), jnp.int32), jax.ShapeDtypeStruct((t, LANES), F32),
                   jax.ShapeDtypeStruct((1, LANES), F32)),
        grid=(t // tr,),
        in_specs=[pl.BlockSpec((tr, LANES), lambda i: (i, 0))],
        out_specs=(pl.BlockSpec((tr, LANES), lambda i: (i, 0)), pl.BlockSpec((tr, LANES), lambda i: (i, 0)),
                   pl.BlockSpec((1, LANES), lambda i: (0, 0))),
        scratch_shapes=[pltpu.VMEM((1, LANES), F32)],
        compiler_params=_params(("arbitrary",)),
        name="router_topk",
    )(logits)


def _gather_kernel(nused_ref, tok_ref, tok_next_ref, h_hbm, o_ref, buf, sem, *, njp):
    r = pl.program_id(0)
    m = o_ref.shape[0]
    slot = r % 2

    def issue(tok, s):
        def body(i, c):
            _token_copy(h_hbm, tok[0, 0, 2 * i], buf.at[s], 2 * i, njp, sem.at[s]).start(priority=0)
            _token_copy(h_hbm, tok[0, 0, 2 * i + 1], buf.at[s], 2 * i + 1, njp, sem.at[s]).start(priority=1)
            return c

        lax.fori_loop(0, m // 2, body, 0, unroll=4)

    @pl.when(r == 0)
    def _():
        issue(tok_ref, 0)

    @pl.when(r + 1 < nused_ref[0])
    def _():
        issue(tok_next_ref, 1 - slot)

    @pl.when(r < nused_ref[0])
    def _():
        pltpu.make_async_copy(h_hbm.at[pl.ds(0, m * njp)], buf.at[slot, pl.ds(0, m * njp)], sem.at[slot]).wait()
        for j in range(njp):
            lo, hi = _load_token_cols(buf.at[slot], m, njp, j)
            o_ref[:, j * LANES:(j + 1) * LANES] = lo.astype(o_ref.dtype)
            o_ref[:, (njp + j) * LANES:(njp + j + 1) * LANES] = hi.astype(o_ref.dtype)

    @pl.when(r >= nused_ref[0])
    def _():
        o_ref[...] = jnp.zeros_like(o_ref)


def _gather_rows(h_tiles, row_tok, n_used, m, njp):
    n_blocks = row_tok.shape[0] // m
    tok3 = row_tok.reshape(n_blocks, 1, m)
    return pl.pallas_call(
        functools.partial(_gather_kernel, njp=njp),
        out_shape=jax.ShapeDtypeStruct((n_blocks * m, 2 * njp * LANES), BF16),
        grid_spec=pltpu.PrefetchScalarGridSpec(
            num_scalar_prefetch=1,
            grid=(n_blocks,),
            in_specs=[
                pl.BlockSpec((1, 1, m), lambda r, nu: (r, 0, 0), memory_space=pltpu.SMEM),
                pl.BlockSpec((1, 1, m), lambda r, nu: (jnp.minimum(r + 1, n_blocks - 1), 0, 0),
                             memory_space=pltpu.SMEM),
                pl.BlockSpec(memory_space=pl.ANY),
            ],
            out_specs=pl.BlockSpec((m, 2 * njp * LANES), lambda r, nu: (r, 0)),
            scratch_shapes=[pltpu.VMEM((2, m * _token_pitch(njp), LANES), U32), pltpu.SemaphoreType.DMA((2,))],
        ),
        compiler_params=_params(("arbitrary",), disable_bounds_checks=True),
        name="expert_row_gather",
    )(n_used, tok3, tok3, h_tiles)


def _group_start(be_ref, r):
    return jnp.logical_or(r == 0, be_ref[r] != be_ref[jnp.maximum(r - 1, 0)])


def _gmm1_kernel(be_ref, nused_ref, x_ref, w_ref, perm_ref, bg_ref, bl_ref, a_ref, wg_s, wl_s):
    r = pl.program_id(1)
    pw = perm_ref.shape[0]

    @pl.when(r < nused_ref[0])
    def _():
        @pl.when(_group_start(be_ref, r))
        def _():
            for c in range(w_ref.shape[1] // pw):
                wb = w_ref[:, c * pw:(c + 1) * pw].astype(BF16)
                sp = jnp.dot(wb, perm_ref[...], preferred_element_type=F32).astype(BF16)
                wg_s[:, c * (pw // 2):(c + 1) * (pw // 2)] = sp[:, :pw // 2]
                wl_s[:, c * (pw // 2):(c + 1) * (pw // 2)] = sp[:, pw // 2:]

        x = x_ref[...]
        ug = jnp.dot(x, wg_s[...], preferred_element_type=F32) + bg_ref[...]
        ul = jnp.dot(x, wl_s[...], preferred_element_type=F32) + bl_ref[...]
        glu = jnp.minimum(ug, SWIGLU_LIMIT)
        lin = jnp.clip(ul, -SWIGLU_LIMIT, SWIGLU_LIMIT)
        a_ref[...] = (glu * jax.nn.sigmoid(SWIGLU_ALPHA * glu) * (lin + 1.0)).astype(a_ref.dtype)

    @pl.when(r >= nused_ref[0])
    def _():
        a_ref[...] = jnp.zeros_like(a_ref)


def _deinterleave_perm(width):
    src = np.arange(width)
    dst = np.where(src % 2 == 0, src // 2, width // 2 + src // 2)
    p = np.zeros((width, width), np.float32)
    p[src, dst] = 1.0
    return jnp.asarray(p, BF16)


def _gmm1(xb, w1, b1g, b1l, block_e, n_used, m, tn):
    rows, d = xb.shape
    f = w1.shape[2] // 2
    n_blocks = rows // m
    pw = 2 * LANES

    def rc(r, nu):
        return jnp.minimum(r, nu[0] - 1)

    return pl.pallas_call(
        _gmm1_kernel,
        out_shape=jax.ShapeDtypeStruct((rows, f), BF16),
        grid_spec=pltpu.PrefetchScalarGridSpec(
            num_scalar_prefetch=2,
            grid=(f // tn, n_blocks),
            in_specs=[
                pl.BlockSpec((m, d), lambda j, r, be, nu: (rc(r, nu), 0)),
                pl.BlockSpec((None, d, 2 * tn), lambda j, r, be, nu: (be[rc(r, nu)], 0, j)),
                pl.BlockSpec((pw, pw), lambda j, r, be, nu: (0, 0)),
                pl.BlockSpec((None, 1, tn), lambda j, r, be, nu: (be[rc(r, nu)], 0, j)),
                pl.BlockSpec((None, 1, tn), lambda j, r, be, nu: (be[rc(r, nu)], 0, j)),
            ],
            out_specs=pl.BlockSpec((m, tn), lambda j, r, be, nu: (r, j)),
            scratch_shapes=[pltpu.VMEM((d, tn), BF16), pltpu.VMEM((d, tn), BF16)],
        ),
        compiler_params=_params(("arbitrary", "arbitrary")),
        name="expert_mlp1_swiglu",
    )(block_e, n_used, xb, w1, _deinterleave_perm(pw), b1g, b1l)


def _gmm2_kernel(be_ref, nused_ref, a_ref, w_ref, b_ref, y_ref, w_s):
    r = pl.program_id(0)

    @pl.when(r < nused_ref[0])
    def _():
        @pl.when(_group_start(be_ref, r))
        def _():
            w_s[...] = w_ref[...].astype(BF16)

        _store_token_tiles(y_ref, jnp.dot(a_ref[...], w_s[...], preferred_element_type=F32) + b_ref[...])

    @pl.when(r >= nused_ref[0])
    def _():
        y_ref[...] = jnp.zeros_like(y_ref)


def _gmm2(a, w2, b2, block_e, n_used, m):
    rows, f = a.shape
    d = w2.shape[2]
    njp = d // (2 * LANES)
    n_blocks = rows // m

    def rc(r, nu):
        return jnp.minimum(r, nu[0] - 1)

    return pl.pallas_call(
        _gmm2_kernel,
        out_shape=jax.ShapeDtypeStruct((rows * njp, LANES), U32),
        grid_spec=pltpu.PrefetchScalarGridSpec(
            num_scalar_prefetch=2,
            grid=(n_blocks,),
            in_specs=[
                pl.BlockSpec((m, f), lambda r, be, nu: (rc(r, nu), 0)),
                pl.BlockSpec((None, f, d), lambda r, be, nu: (be[rc(r, nu)], 0, 0), pipeline_mode=pl.Buffered(1)),
                pl.BlockSpec((None, 1, d), lambda r, be, nu: (be[rc(r, nu)], 0, 0)),
            ],
            out_specs=pl.BlockSpec((m * njp, LANES), lambda r, be, nu: (r, 0)),
            scratch_shapes=[pltpu.VMEM((f, d), BF16)],
        ),
        compiler_params=_params(("arbitrary",)),
        name="expert_mlp2",
    )(block_e, n_used, a, w2, b2)


def _combine_kernel(dest_ref, dest_next_ref, y_hbm, gate_ref, x2_ref, g_ref, o_ref, buf, sem, *, final_norm):
    i = pl.program_id(0)
    tc, d = x2_ref.shape
    njp = d // (2 * LANES)
    slot = i % 2

    def issue(dest, s):
        def body(t, c):
            for k in range(TOP_K):
                _token_copy(y_hbm, dest[0, 0, t * TOP_K + k], buf.at[s, k], t, njp, sem.at[s]).start(priority=k % 2)
            return c

        lax.fori_loop(0, tc, body, 0, unroll=2)

    @pl.when(i == 0)
    def _():
        issue(dest_ref, 0)

    @pl.when(i + 1 < pl.num_programs(0))
    def _():
        issue(dest_next_ref, 1 - slot)

    for k in range(TOP_K):
        pltpu.make_async_copy(y_hbm.at[pl.ds(0, tc * njp)], buf.at[slot, k, pl.ds(0, tc * njp)],
                              sem.at[slot]).wait()
    lo_cols, hi_cols = [], []
    for j in range(njp):
        acc_lo = x2_ref[:, j * LANES:(j + 1) * LANES]
        acc_hi = x2_ref[:, (njp + j) * LANES:(njp + j + 1) * LANES]
        for k in range(TOP_K):
            lo, hi = _load_token_cols(buf.at[slot, k], tc, njp, j)
            g = gate_ref[:, k:k + 1]
            acc_lo = acc_lo + g * lo
            acc_hi = acc_hi + g * hi
        lo_cols.append(acc_lo)
        hi_cols.append(acc_hi)
    acc = jnp.concatenate(lo_cols + hi_cols, axis=1)
    if final_norm:
        ms = jnp.mean(acc * acc, axis=-1, keepdims=True)
        acc = acc * lax.rsqrt(ms + RMS_EPS) * g_ref[...]
    o_ref[...] = acc


def _combine(y_tiles, dest, gates, x2, g_final, tc, final_norm):
    t, d = x2.shape
    njp = d // (2 * LANES)
    nt = t // tc
    dest3 = dest.reshape(nt, 1, tc * TOP_K)
    return pl.pallas_call(
        functools.partial(_combine_kernel, final_norm=final_norm),
        out_shape=jax.ShapeDtypeStruct((t, d), F32),
        grid=(nt,),
        in_specs=[
            pl.BlockSpec((1, 1, tc * TOP_K), lambda i: (i, 0, 0), memory_space=pltpu.SMEM),
            pl.BlockSpec((1, 1, tc * TOP_K), lambda i: (jnp.minimum(i + 1, nt - 1), 0, 0),
                         memory_space=pltpu.SMEM),
            pl.BlockSpec(memory_space=pl.ANY),
            pl.BlockSpec((tc, LANES), lambda i: (i, 0)),
            pl.BlockSpec((tc, d), lambda i: (i, 0)),
            pl.BlockSpec((1, d), lambda i: (0, 0)),
        ],
        out_specs=pl.BlockSpec((tc, d), lambda i: (i, 0)),
        scratch_shapes=[pltpu.VMEM((2, TOP_K, tc * _token_pitch(njp), LANES), U32),
                        pltpu.SemaphoreType.DMA((2,))],
        compiler_params=_params(("arbitrary",), disable_bounds_checks=True),
        name="moe_combine_final_norm",
    )(dest3, dest3, y_tiles, gates, x2, g_final.reshape(1, d))


def _layer(x, norm_mix_g, w_in, b_forget, w_branch_fox, w_branch_moba, w_out, norm_ffn_g,
           w_router, b_router, w_mlp1, b_mlp1, w_mlp2, b_mlp2):
    b, s, d = x.shape
    t = b * s
    width = N_HEADS * HEAD_DIM
    xt = x.reshape(t, d)
    q_scale = LOG2E * HEAD_DIM ** -0.5

    o = np.cumsum([0, width, width, width, N_HEADS, width, width, width, d, d])
    w_fox = w_in[:, o[0]:o[3]].astype(BF16)
    w_fl = jnp.pad(w_in[:, o[3]:o[4]], ((0, 0), (0, LANES - N_HEADS))).astype(BF16)
    w_mqk = w_in[:, o[4]:o[6]].astype(BF16)
    w_mv = w_in[:, o[6]:o[7]].astype(BF16)
    w_g = w_in[:, o[7]:o[9]].astype(BF16)
    colscale = jnp.concatenate([jnp.full((1, width), q_scale, F32), jnp.ones((1, 2 * width), F32)], axis=1)

    tm = min(1024, s)
    h = _rmsnorm(xt, norm_mix_g, tm=min(512, t))
    fox = _proj(h, w_fox, colscale, seq=s, tm=tm, tn=512, out_dtype=BF16, heads=True, name="proj_fox_qkv")
    mqk = _proj(h, w_mqk, colscale[:, :2 * width], seq=s, tm=tm, tn=512, out_dtype=BF16, heads=True,
                rope=_rope_tables(s), name="proj_moba_qk")
    mv = _proj(h, w_mv, colscale[:, width:2 * width], seq=s, tm=tm, tn=512, out_dtype=BF16, heads=True,
               name="proj_moba_v")
    gates = _proj(h, w_g, jnp.ones((1, 2 * d), F32), seq=s, tm=tm, tn=512, out_dtype=BF16, heads=False,
                  sigmoid=True, name="proj_gates")
    fl = _proj(h, w_fl, jnp.ones((1, LANES), F32), seq=s, tm=tm, tn=LANES, out_dtype=F32, heads=False,
               name="proj_forget")

    fl_t = fl[:, :N_HEADS].reshape(b, s, N_HEADS).transpose(0, 2, 1)
    c = _forget_cumsum(fl_t, b_forget)
    c5 = c.reshape(b, N_HEADS, s // MOBA_BLOCK, 1, MOBA_BLOCK)

    o_fox = _attention(fox, 0, fox, N_HEADS, fox, 2 * N_HEADS, c5, moba=False, name="fox_attention")
    o_moba = _attention(mqk, 0, mqk, N_HEADS, mv, 0, None, moba=True, name="moba_attention")

    w_r = jnp.pad(w_router, ((0, 0), (0, LANES - N_EXPERTS))).astype(BF16)
    b_r = jnp.concatenate([b_router.astype(F32), jnp.full((LANES - N_EXPERTS,), NEG_INF, F32)]).reshape(1, LANES)
    x2, h2, logits = _mix(o_fox, o_moba, gates, xt, w_branch_fox.astype(BF16), w_branch_moba.astype(BF16),
                          w_out.astype(BF16), norm_ffn_g.reshape(1, d), w_r, b_r, tm=min(256, t))

    meta, gate_w, cnt = _router(logits, tr=min(512, t))

    m = MOE_ROWS
    assign = t * TOP_K
    n_blocks = (assign + N_EXPERTS * (m - 1) + m - 1) // m
    counts = cnt[0, :N_EXPERTS].astype(jnp.int32)
    padded = (counts + m - 1) // m * m
    pend = jnp.cumsum(padded)
    pstart = pend - padded
    experts = jnp.arange(N_EXPERTS, dtype=jnp.int32)
    e_sel = meta[:, :TOP_K]
    dest = meta[:, TOP_K:2 * TOP_K] + jnp.sum(jnp.where(e_sel[..., None] == experts, pstart, 0), axis=-1)
    n_used = (pend[-1] // m).astype(jnp.int32).reshape(1)
    first_row = jnp.arange(n_blocks, dtype=jnp.int32) * m
    block_e = jnp.minimum(jnp.sum((pend[None, :] <= first_row[:, None]).astype(jnp.int32), axis=1), N_EXPERTS - 1)
    row_tok = jnp.zeros((n_blocks * m,), jnp.int32).at[dest.reshape(-1)].set(
        jnp.arange(assign, dtype=jnp.int32) // TOP_K)

    xb = _gather_rows(h2, row_tok, n_used, m, d // (2 * LANES))
    f = w_mlp1.shape[2] // 2
    b1g = b_mlp1[:, 0::2].reshape(N_EXPERTS, 1, f)
    b1l = b_mlp1[:, 1::2].reshape(N_EXPERTS, 1, f)
    a = _gmm1(xb, w_mlp1, b1g, b1l, block_e, n_used, m, tn=min(512, f))
    y = _gmm2(a, w_mlp2, b_mlp2.reshape(N_EXPERTS, 1, d), block_e, n_used, m)
    return x2, y, dest, gate_w


def kernel(x, norm_mix_g, w_in, b_forget, w_branch_fox, w_branch_moba, w_out, norm_ffn_g, w_router, b_router,
           w_mlp1, b_mlp1, w_mlp2, b_mlp2, norm_final_g):
    depth = w_in.shape[0]
    b, s, d = x.shape
    for l in range(depth):
        x2, y, dest, gate_w = _layer(x, norm_mix_g[l], w_in[l], b_forget[l], w_branch_fox[l], w_branch_moba[l],
                                     w_out[l], norm_ffn_g[l], w_router[l], b_router[l], w_mlp1[l], b_mlp1[l],
                                     w_mlp2[l], b_mlp2[l])
        out = _combine(y, dest, gate_w, x2, norm_final_g, tc=min(128, b * s), final_norm=l == depth - 1)
        x = out.reshape(b, s, d)
    return x
```

```python
import functools
import math

import jax
import jax.numpy as jnp
import numpy as np
from jax import lax
from jax.experimental import pallas as pl
from jax.experimental.pallas import tpu as pltpu

HEAD_DIM = 128
N_HEADS = 8
MOBA_BLOCK = 256
MOBA_TOPK = 3
ROPE_THETA = 500000.0
ROPE_DIM = HEAD_DIM // 4
N_EXPERTS = 32
TOP_K = 4
SWIGLU_ALPHA = 1.702
SWIGLU_LIMIT = 7.0
RMS_EPS = 1e-5

LANES = 128
SUBLANES = 8
MXU_COLS = 256
VMEM_LIMIT_BYTES = 56 * 1024 * 1024
MOE_ROWS = 512
ATTN_Q_ROWS = 256

F32 = jnp.float32
BF16 = jnp.bfloat16
U32 = jnp.uint32
NEG_INF = float("-inf")
MASKED = -1e30
LOG2E = math.log2(math.e)


def _params(sem, **kw):
    return pltpu.CompilerParams(dimension_semantics=sem, vmem_limit_bytes=VMEM_LIMIT_BYTES, **kw)


def _pack_pairs(lo, hi):
    lo_b = lax.bitcast_convert_type(lo.astype(BF16).astype(F32), U32) >> 16
    hi_b = lax.bitcast_convert_type(hi.astype(BF16).astype(F32), U32) & jnp.uint32(0xFFFF0000)
    return hi_b | lo_b


def _unpack_pairs(w):
    lo = lax.bitcast_convert_type(w << 16, F32)
    hi = lax.bitcast_convert_type(w & jnp.uint32(0xFFFF0000), F32)
    return lo, hi


def _store_token_tiles(ref, val):
    n, d = val.shape
    njp = d // (2 * LANES)
    for j in range(njp):
        lo = val[:, j * LANES:(j + 1) * LANES]
        hi = val[:, (njp + j) * LANES:(njp + j + 1) * LANES]
        ref[pl.ds(j, n, stride=njp), :] = _pack_pairs(lo, hi)


def _token_pitch(njp):
    p = -(-njp // 4)
    return 4 * (p if p % 2 else p + 1)


def _token_copy(src_hbm, row, dst_vmem, slot, njp, sem):
    return pltpu.make_async_copy(src_hbm.at[pl.ds(pl.multiple_of(row * njp, njp), njp)],
                                 dst_vmem.at[pl.ds(slot * _token_pitch(njp), njp)], sem)


def _load_token_cols(buf, n, njp, j):
    return _unpack_pairs(buf[pl.ds(j, n, stride=_token_pitch(njp)), :])


def _rmsnorm_kernel(x_ref, g_ref, o_ref):
    x = x_ref[...]
    ms = jnp.mean(x * x, axis=-1, keepdims=True)
    o_ref[...] = (x * lax.rsqrt(ms + RMS_EPS) * g_ref[...]).astype(o_ref.dtype)


def _rmsnorm(x, g, tm):
    t, d = x.shape
    return pl.pallas_call(
        _rmsnorm_kernel,
        out_shape=jax.ShapeDtypeStruct((t, d), BF16),
        grid=(t // tm,),
        in_specs=[pl.BlockSpec((tm, d), lambda i: (i, 0)), pl.BlockSpec((1, d), lambda i: (0, 0))],
        out_specs=pl.BlockSpec((tm, d), lambda i: (i, 0)),
        compiler_params=_params(("parallel",)),
        name="rmsnorm_mix",
    )(x, g.reshape(1, d))


def _proj_kernel(a_ref, w_ref, cs_ref, *rest, rotary, sigmoid, heads):
    if rotary:
        cos_ref, sa_ref, sb_ref, o_ref = rest
    else:
        (o_ref,) = rest
    res = jnp.dot(a_ref[...], w_ref[...], preferred_element_type=F32) * cs_ref[...]
    if sigmoid:
        res = jax.nn.sigmoid(res)
    for j in range(res.shape[1] // LANES):
        r = res[:, j * LANES:(j + 1) * LANES]
        if rotary:
            half = ROPE_DIM // 2
            r = (r * cos_ref[...] + pltpu.roll(r, half, 1) * sa_ref[...]
                 + pltpu.roll(r, LANES - half, 1) * sb_ref[...])
        if heads:
            o_ref[0, j] = r.astype(o_ref.dtype)
        else:
            o_ref[:, j * LANES:(j + 1) * LANES] = r.astype(o_ref.dtype)


def _proj(a, w, colscale, *, seq, tm, tn, out_dtype, heads, sigmoid=False, rope=None, name):
    t, k = a.shape
    n = w.shape[1]
    sblk = seq // tm
    in_specs = [
        pl.BlockSpec((tm, k), lambda m, j: (m, 0)),
        pl.BlockSpec((k, tn), lambda m, j: (0, j)),
        pl.BlockSpec((1, tn), lambda m, j: (0, j)),
    ]
    args = [a, w, colscale]
    if rope is not None:
        in_specs += [pl.BlockSpec((tm, LANES), lambda m, j: (m % sblk, 0))] * 3
        args += list(rope)
    if heads:
        out_shape = jax.ShapeDtypeStruct((t // seq, n // LANES, seq, LANES), out_dtype)
        out_spec = pl.BlockSpec((1, tn // LANES, tm, LANES), lambda m, j: (m // sblk, j, m % sblk, 0))
    else:
        out_shape = jax.ShapeDtypeStruct((t, n), out_dtype)
        out_spec = pl.BlockSpec((tm, tn), lambda m, j: (m, j))
    return pl.pallas_call(
        functools.partial(_proj_kernel, rotary=rope is not None, sigmoid=sigmoid, heads=heads),
        out_shape=out_shape,
        grid=(t // tm, n // tn),
        in_specs=in_specs,
        out_specs=out_spec,
        compiler_params=_params(("parallel", "arbitrary")),
        name=name,
    )(*args)


def _rope_tables(seq):
    half = ROPE_DIM // 2
    inv_freq = 1.0 / (ROPE_THETA ** (jnp.arange(half, dtype=F32) / half))
    ang = jnp.arange(seq, dtype=F32)[:, None] * inv_freq[None, :]
    cos, sin = jnp.cos(ang), jnp.sin(ang)
    ones = jnp.ones((seq, LANES - ROPE_DIM), F32)
    zeros = jnp.zeros((seq, LANES - ROPE_DIM), F32)
    zh = jnp.zeros((seq, half), F32)
    cos_t = jnp.concatenate([cos, cos, ones], axis=1)
    sin_a = jnp.concatenate([zh, sin, zeros], axis=1)
    sin_b = jnp.concatenate([-sin, zh, zeros], axis=1)
    return cos_t, sin_a, sin_b


def _forget_kernel(fl_ref, b_ref, c_ref):
    z = fl_ref[...] + b_ref[...]
    logf = jnp.minimum(z, 0.0) - jnp.log1p(jnp.exp(-jnp.abs(z)))
    row = lax.broadcasted_iota(jnp.int32, (LANES, LANES), 0)
    col = lax.broadcasted_iota(jnp.int32, (LANES, LANES), 1)
    tri = (row <= col).astype(BF16)
    carry = jnp.zeros((logf.shape[0], 1), F32)
    for j in range(logf.shape[1] // LANES):
        xb = logf[:, j * LANES:(j + 1) * LANES]
        hi = xb.astype(BF16)
        r1 = xb - hi.astype(F32)
        mid = r1.astype(BF16)
        lo = (r1 - mid.astype(F32)).astype(BF16)
        cs = (jnp.dot(hi, tri, preferred_element_type=F32) + jnp.dot(mid, tri, preferred_element_type=F32)
              + jnp.dot(lo, tri, preferred_element_type=F32)) + carry
        c_ref[:, j * LANES:(j + 1) * LANES] = cs * LOG2E
        carry = cs[:, LANES - 1:LANES]


def _forget_cumsum(fl_t, b_forget):
    b, h, s = fl_t.shape
    return pl.pallas_call(
        _forget_kernel,
        out_shape=jax.ShapeDtypeStruct((b, h, s), F32),
        grid=(b,),
        in_specs=[pl.BlockSpec((None, h, s), lambda i: (i, 0, 0)), pl.BlockSpec((h, 1), lambda i: (0, 0))],
        out_specs=pl.BlockSpec((None, h, s), lambda i: (i, 0, 0)),
        compiler_params=_params(("parallel",)),
        name="forget_cumsum",
    )(fl_t, b_forget.reshape(h, 1).astype(F32))


def _attn_kernel(*refs, moba, tq, tk):
    if moba:
        q_ref, k_ref, v_ref, o_ref, km_ref, kaug_ref = refs
    else:
        q_ref, k_ref, v_ref, c_ref, o_ref = refs
    nkb = q_ref.shape[0] // tk
    nt = (((1,), (1,)), ((), ()))
    row = lax.broadcasted_iota(jnp.int32, (tq, tk), 0)
    col = lax.broadcasted_iota(jnp.int32, (tq, tk), 1)

    if moba:
        lane = lax.broadcasted_iota(jnp.int32, (tk, LANES), 1)
        km_ref[...] = jnp.zeros_like(km_ref)
        for n in range(nkb):
            kb = k_ref[n * tk:(n + 1) * tk, :]
            km_ref[n:n + 1, :] = jnp.mean(kb.astype(F32), axis=0, keepdims=True)
            kaug_ref[n * tk:(n + 1) * tk, :LANES] = kb
            kaug_ref[n * tk:(n + 1) * tk, LANES:] = (lane == n).astype(BF16)
        kmean = km_ref[...].astype(BF16)
        blk = lax.broadcasted_iota(jnp.int32, (SUBLANES, tk), 0)

    def keys(n):
        return (kaug_ref if moba else k_ref)[n * tk:(n + 1) * tk, :]

    for nb in range(nkb):
        if moba and nb > 0:
            g = lax.dot_general(kmean, q_ref[nb * tk:(nb + 1) * tk, :], nt, preferred_element_type=F32)[:SUBLANES]
            past = blk < nb
            pen = jnp.zeros((SUBLANES, tk), F32)
            for n in range(nb):
                g_n = g[n:n + 1, :]
                beats = ((g > g_n) | ((g == g_n) & (blk < n))) & past
                rank = jnp.sum(beats.astype(F32), axis=0, keepdims=True)
                pen = jnp.where((blk == n) & (rank >= MOBA_TOPK), MASKED, pen)
            pen = jnp.concatenate([pen, jnp.zeros((LANES - SUBLANES, tk), F32)], axis=0)
            pen_cols = pen.T.astype(BF16)
        for h in range(tk // tq):
            r0 = nb * tk + h * tq
            q = q_ref[r0:r0 + tq, :]
            s = lax.dot_general(q, k_ref[nb * tk:(nb + 1) * tk, :], nt, preferred_element_type=F32)
            if not moba:
                s = s - c_ref[nb]
            s = jnp.where(row + h * tq >= col, s, NEG_INF)
            m = jnp.max(s, axis=1, keepdims=True)
            p = jnp.exp2(s - m)
            l = jnp.sum(p, axis=1, keepdims=True)
            acc = jnp.dot(p.astype(BF16), v_ref[nb * tk:(nb + 1) * tk, :], preferred_element_type=F32)
            if moba and nb > 0:
                q = jnp.concatenate([q, pen_cols[h * tq:(h + 1) * tq, :]], axis=1)
            for n in range(nb):
                s = lax.dot_general(q, keys(n), nt, preferred_element_type=F32)
                if not moba:
                    s = s - c_ref[n]
                m_new = jnp.maximum(m, jnp.max(s, axis=1, keepdims=True))
                alpha = jnp.exp2(m - m_new)
                p = jnp.exp2(s - m_new)
                l = alpha * l + jnp.sum(p, axis=1, keepdims=True)
                acc = alpha * acc + jnp.dot(p.astype(BF16), v_ref[n * tk:(n + 1) * tk, :],
                                            preferred_element_type=F32)
                m = m_new
            o_ref[r0:r0 + tq, :] = (acc / l).astype(o_ref.dtype)


def _attention(q_arr, q_off, k_arr, k_off, v_arr, v_off, c5, *, moba, name):
    b, _, s, d = q_arr.shape
    tk = MOBA_BLOCK
    tq = min(ATTN_Q_ROWS, tk)
    in_specs = [
        pl.BlockSpec((None, None, s, d), lambda bi, h: (bi, q_off + h, 0, 0)),
        pl.BlockSpec((None, None, s, d), lambda bi, h: (bi, k_off + h, 0, 0)),
        pl.BlockSpec((None, None, s, d), lambda bi, h: (bi, v_off + h, 0, 0)),
    ]
    args = [q_arr, k_arr, v_arr]
    scratch = []
    if moba:
        scratch = [pltpu.VMEM((LANES, d), F32), pltpu.VMEM((s, d + LANES), BF16)]
    else:
        in_specs.append(pl.BlockSpec((None, None, s // tk, 1, tk), lambda bi, h: (bi, h, 0, 0, 0)))
        args.append(c5)
    return pl.pallas_call(
        functools.partial(_attn_kernel, moba=moba, tq=tq, tk=tk),
        out_shape=jax.ShapeDtypeStruct((b * s, N_HEADS * d), BF16),
        grid=(b, N_HEADS),
        in_specs=in_specs,
        out_specs=pl.BlockSpec((s, d), lambda bi, h: (bi, h)),
        scratch_shapes=scratch,
        compiler_params=_params(("parallel", "parallel")),
        name=name,
    )(*args)


def _mix_kernel(of_ref, om_ref, g_ref, x_ref, wf_ref, wm_ref, wo_ref, gn_ref, wr_ref, br_ref,
                x2_ref, h2_ref, lg_ref):
    d = x_ref.shape[1]
    yf = jnp.dot(of_ref[...], wf_ref[...], preferred_element_type=F32)
    ym = jnp.dot(om_ref[...], wm_ref[...], preferred_element_type=F32)
    mixed = g_ref[:, :d].astype(F32) * yf + g_ref[:, d:].astype(F32) * ym
    x2 = x_ref[...] + jnp.dot(mixed.astype(BF16), wo_ref[...], preferred_element_type=F32)
    x2_ref[...] = x2
    ms = jnp.mean(x2 * x2, axis=-1, keepdims=True)
    h2 = x2 * lax.rsqrt(ms + RMS_EPS) * gn_ref[...]
    _store_token_tiles(h2_ref, h2)
    lg_ref[...] = jnp.dot(h2.astype(BF16), wr_ref[...], preferred_element_type=F32) + br_ref[...]


def _mix(o_fox, o_moba, gates, x, w_f, w_m, w_o, g_ffn, w_r, b_r, tm):
    t, d = x.shape
    w = o_fox.shape[1]
    njp = d // (2 * LANES)
    const = lambda i: (0, 0)
    resident = dict(pipeline_mode=pl.Buffered(1))
    return pl.pallas_call(
        _mix_kernel,
        out_shape=(jax.ShapeDtypeStruct((t, d), F32), jax.ShapeDtypeStruct((t * njp, LANES), U32),
                   jax.ShapeDtypeStruct((t, LANES), F32)),
        grid=(t // tm,),
        in_specs=[
            pl.BlockSpec((tm, w), lambda i: (i, 0)),
            pl.BlockSpec((tm, w), lambda i: (i, 0)),
            pl.BlockSpec((tm, 2 * d), lambda i: (i, 0)),
            pl.BlockSpec((tm, d), lambda i: (i, 0)),
            pl.BlockSpec((w, d), const, **resident),
            pl.BlockSpec((w, d), const, **resident),
            pl.BlockSpec((d, d), const, **resident),
            pl.BlockSpec((1, d), const),
            pl.BlockSpec((d, LANES), const),
            pl.BlockSpec((1, LANES), const),
        ],
        out_specs=(pl.BlockSpec((tm, d), lambda i: (i, 0)), pl.BlockSpec((tm * njp, LANES), lambda i: (i, 0)),
                   pl.BlockSpec((tm, LANES), lambda i: (i, 0))),
        compiler_params=_params(("parallel",)),
        name="mix_out_router",
    )(o_fox, o_moba, gates, x, w_f, w_m, w_o, g_ffn, w_r, b_r)


def _router_kernel(lg_ref, meta_ref, gate_ref, cnt_ref, carry_ref):
    @pl.when(pl.program_id(0) == 0)
    def _():
        carry_ref[...] = jnp.zeros_like(carry_ref)

    l = lg_ref[...]
    tr = l.shape[0]
    lane = lax.broadcasted_iota(jnp.int32, (tr, LANES), 1).astype(F32)
    vals, idxs = [], []
    for _ in range(TOP_K):
        m = jnp.max(l, axis=1, keepdims=True)
        ix = jnp.min(jnp.where(l == m, lane, float(LANES)), axis=1, keepdims=True)
        vals.append(m)
        idxs.append(ix)
        l = jnp.where(lane == ix, NEG_INF, l)
    ex = [jnp.exp(v - vals[0]) for v in vals]
    den = ex[0] + ex[1] + ex[2] + ex[3]
    onehot = jnp.zeros((tr, LANES), F32)
    for ix in idxs:
        onehot = onehot + (lane == ix).astype(F32)
    row = lax.broadcasted_iota(jnp.int32, (tr, tr), 0)
    col = lax.broadcasted_iota(jnp.int32, (tr, tr), 1)
    before = (col < row).astype(BF16)
    running = jnp.dot(before, onehot.astype(BF16), preferred_element_type=F32) + carry_ref[...]
    meta = jnp.zeros((tr, LANES), F32)
    gate = jnp.zeros((tr, LANES), F32)
    for k in range(TOP_K):
        pos = jnp.sum(jnp.where(lane == idxs[k], running, 0.0), axis=1, keepdims=True)
        meta = jnp.where(lane == float(k), idxs[k], meta)
        meta = jnp.where(lane == float(TOP_K + k), pos, meta)
        gate = jnp.where(lane == float(k), ex[k] / den, gate)
    meta_ref[...] = meta.astype(jnp.int32)
    gate_ref[...] = gate
    carry_ref[...] += jnp.sum(onehot, axis=0, keepdims=True)
    cnt_ref[...] = carry_ref[...]


def _router(logits, tr):
    t = logits.shape[0]
    return pl.pallas_call(
        _router_kernel,
        out_shape=(jax.ShapeDtypeStruct((t, LANES), jnp.int32), jax.ShapeDtypeStruct((t, LANES), F32),
                   jax.ShapeDtypeStruct((1, LANES), F32)),
        grid=(t // tr,),
        in_specs=[pl.BlockSpec((tr, LANES), lambda i: (i, 0))],
        out_specs=(pl.BlockSpec((tr, LANES), lambda i: (i, 0)), pl.BlockSpec((tr, LANES), lambda i: (i, 0)),
                   pl.BlockSpec((1, LANES), lambda i: (0, 0))),
        scratch_shapes=[pltpu.VMEM((1, LANES), F32)],
        compiler_params=_params(("arbitrary",)),
        name="router_topk",
    )(logits)


def _gather_kernel(nused_ref, tok_ref, tok_next_ref, h_hbm, o_ref, buf, sem, *, njp):
    r = pl.program_id(0)
    m = o_ref.shape[0]
    slot = r % 2

    def issue(tok, s):
        def body(i, c):
            _token_copy(h_hbm, tok[0, 0, 2 * i], buf.at[s], 2 * i, njp, sem.at[s]).start(priority=0)
            _token_copy(h_hbm, tok[0, 0, 2 * i + 1], buf.at[s], 2 * i + 1, njp, sem.at[s]).start(priority=1)
            return c

        lax.fori_loop(0, m // 2, body, 0, unroll=4)

    @pl.when(r == 0)
    def _():
        issue(tok_ref, 0)

    @pl.when(r + 1 < nused_ref[0])
    def _():
        issue(tok_next_ref, 1 - slot)

    @pl.when(r < nused_ref[0])
    def _():
        pltpu.make_async_copy(h_hbm.at[pl.ds(0, m * njp)], buf.at[slot, pl.ds(0, m * njp)], sem.at[slot]).wait()
        for j in range(njp):
            lo, hi = _load_token_cols(buf.at[slot], m, njp, j)
            o_ref[:, j * LANES:(j + 1) * LANES] = lo.astype(o_ref.dtype)
            o_ref[:, (njp + j) * LANES:(njp + j + 1) * LANES] = hi.astype(o_ref.dtype)

    @pl.when(r >= nused_ref[0])
    def _():
        o_ref[...] = jnp.zeros_like(o_ref)


def _gather_rows(h_tiles, row_tok, n_used, m, njp):
    n_blocks = row_tok.shape[0] // m
    tok3 = row_tok.reshape(n_blocks, 1, m)
    return pl.pallas_call(
        functools.partial(_gather_kernel, njp=njp),
        out_shape=jax.ShapeDtypeStruct((n_blocks * m, 2 * njp * LANES), BF16),
        grid_spec=pltpu.PrefetchScalarGridSpec(
            num_scalar_prefetch=1,
            grid=(n_blocks,),
            in_specs=[
                pl.BlockSpec((1, 1, m), lambda r, nu: (r, 0, 0), memory_space=pltpu.SMEM),
                pl.BlockSpec((1, 1, m), lambda r, nu: (jnp.minimum(r + 1, n_blocks - 1), 0, 0),
                             memory_space=pltpu.SMEM),
                pl.BlockSpec(memory_space=pl.ANY),
            ],
            out_specs=pl.BlockSpec((m, 2 * njp * LANES), lambda r, nu: (r, 0)),
            scratch_shapes=[pltpu.VMEM((2, m * _token_pitch(njp), LANES), U32), pltpu.SemaphoreType.DMA((2,))],
        ),
        compiler_params=_params(("arbitrary",), disable_bounds_checks=True),
        name="expert_row_gather",
    )(n_used, tok3, tok3, h_tiles)


def _group_start(be_ref, r):
    return jnp.logical_or(r == 0, be_ref[r] != be_ref[jnp.maximum(r - 1, 0)])


def _gmm1_kernel(be_ref, nne_ref, nused_ref, x_ref, w_hbm, perm_ref, bg_ref, bl_ref, a_ref,
                 stage, wg_s, wl_s, sem):
    j = pl.program_id(0)
    r = pl.program_id(1)
    pw = perm_ref.shape[0]
    tw = stage.shape[1]

    def fetch(jj, e):
        return pltpu.make_async_copy(w_hbm.at[e, :, pl.ds(pl.multiple_of(jj * tw, tw), tw)], stage, sem)

    @pl.when((j == 0) & (r == 0))
    def _():
        fetch(0, be_ref[0]).start()

    @pl.when(r < nused_ref[0])
    def _():
        @pl.when(_group_start(be_ref, r))
        def _():
            fetch(j, be_ref[r]).wait()
            for c in range(tw // pw):
                wb = stage[:, c * pw:(c + 1) * pw].astype(BF16)
                sp = jnp.dot(wb, perm_ref[...], preferred_element_type=F32).astype(BF16)
                wg_s[:, c * (pw // 2):(c + 1) * (pw // 2)] = sp[:, :pw // 2]
                wl_s[:, c * (pw // 2):(c + 1) * (pw // 2)] = sp[:, pw // 2:]
            nxt = nne_ref[be_ref[r]]

            @pl.when(nxt >= 0)
            def _():
                fetch(j, nxt).start()

            @pl.when((nxt < 0) & (j + 1 < pl.num_programs(0)))
            def _():
                fetch(j + 1, be_ref[0]).start()

        x = x_ref[...]
        ug = jnp.dot(x, wg_s[...], preferred_element_type=F32) + bg_ref[...]
        ul = jnp.dot(x, wl_s[...], preferred_element_type=F32) + bl_ref[...]
        glu = jnp.minimum(ug, SWIGLU_LIMIT)
        lin = jnp.clip(ul, -SWIGLU_LIMIT, SWIGLU_LIMIT)
        a_ref[...] = (glu * jax.nn.sigmoid(SWIGLU_ALPHA * glu) * (lin + 1.0)).astype(a_ref.dtype)

    @pl.when(r >= nused_ref[0])
    def _():
        a_ref[...] = jnp.zeros_like(a_ref)


def _deinterleave_perm(width):
    src = np.arange(width)
    dst = np.where(src % 2 == 0, src // 2, width // 2 + src // 2)
    p = np.zeros((width, width), np.float32)
    p[src, dst] = 1.0
    return jnp.asarray(p, BF16)


def _gmm1(xb, w1, b1g, b1l, block_e, next_expert, n_used, m, tn):
    rows, d = xb.shape
    f = w1.shape[2] // 2
    n_blocks = rows // m
    pw = 2 * LANES

    def rc(r, nu):
        return jnp.minimum(r, nu[0] - 1)

    return pl.pallas_call(
        _gmm1_kernel,
        out_shape=jax.ShapeDtypeStruct((rows, f), BF16),
        grid_spec=pltpu.PrefetchScalarGridSpec(
            num_scalar_prefetch=3,
            grid=(f // tn, n_blocks),
            in_specs=[
                pl.BlockSpec((m, d), lambda j, r, be, nne, nu: (rc(r, nu), 0)),
                pl.BlockSpec(memory_space=pl.ANY),
                pl.BlockSpec((pw, pw), lambda j, r, be, nne, nu: (0, 0)),
                pl.BlockSpec((None, 1, tn), lambda j, r, be, nne, nu: (be[rc(r, nu)], 0, j)),
                pl.BlockSpec((None, 1, tn), lambda j, r, be, nne, nu: (be[rc(r, nu)], 0, j)),
            ],
            out_specs=pl.BlockSpec((m, tn), lambda j, r, be, nne, nu: (r, j)),
            scratch_shapes=[pltpu.VMEM((d, 2 * tn), F32), pltpu.VMEM((d, tn), BF16), pltpu.VMEM((d, tn), BF16),
                            pltpu.SemaphoreType.DMA(())],
        ),
        compiler_params=_params(("arbitrary", "arbitrary")),
        name="expert_mlp1_swiglu",
    )(block_e, next_expert, n_used, xb, w1, _deinterleave_perm(pw), b1g, b1l)


def _gmm2_kernel(be_ref, nne_ref, nused_ref, a_ref, w_hbm, b_ref, y_ref, stage, w_s, sem):
    r = pl.program_id(0)

    def fetch(e):
        return pltpu.make_async_copy(w_hbm.at[e], stage, sem)

    @pl.when(r == 0)
    def _():
        fetch(be_ref[0]).start()

    @pl.when(r < nused_ref[0])
    def _():
        @pl.when(_group_start(be_ref, r))
        def _():
            fetch(be_ref[r]).wait()
            w_s[...] = stage[...].astype(BF16)
            nxt = nne_ref[be_ref[r]]

            @pl.when(nxt >= 0)
            def _():
                fetch(nxt).start()

        a = a_ref[...]
        n = a.shape[0]
        half = w_s.shape[1] // 2
        njp = half // LANES
        cw = min(MXU_COLS, half)
        for c in range(half // cw):
            lo_cols = slice(c * cw, (c + 1) * cw)
            hi_cols = slice(half + c * cw, half + (c + 1) * cw)
            y_lo = jnp.dot(a, w_s[:, lo_cols], preferred_element_type=F32) + b_ref[:, lo_cols]
            y_hi = jnp.dot(a, w_s[:, hi_cols], preferred_element_type=F32) + b_ref[:, hi_cols]
            for jj in range(cw // LANES):
                lanes = slice(jj * LANES, (jj + 1) * LANES)
                y_ref[pl.ds(c * (cw // LANES) + jj, n, stride=njp), :] = _pack_pairs(y_lo[:, lanes], y_hi[:, lanes])

    @pl.when(r >= nused_ref[0])
    def _():
        y_ref[...] = jnp.zeros_like(y_ref)


def _gmm2(a, w2, b2, block_e, next_expert, n_used, m):
    rows, f = a.shape
    d = w2.shape[2]
    njp = d // (2 * LANES)
    n_blocks = rows // m

    def rc(r, nu):
        return jnp.minimum(r, nu[0] - 1)

    return pl.pallas_call(
        _gmm2_kernel,
        out_shape=jax.ShapeDtypeStruct((rows * njp, LANES), U32),
        grid_spec=pltpu.PrefetchScalarGridSpec(
            num_scalar_prefetch=3,
            grid=(n_blocks,),
            in_specs=[
                pl.BlockSpec((m, f), lambda r, be, nne, nu: (rc(r, nu), 0)),
                pl.BlockSpec(memory_space=pl.ANY),
                pl.BlockSpec((None, 1, d), lambda r, be, nne, nu: (be[rc(r, nu)], 0, 0)),
            ],
            out_specs=pl.BlockSpec((m * njp, LANES), lambda r, be, nne, nu: (r, 0)),
            scratch_shapes=[pltpu.VMEM((f, d), F32), pltpu.VMEM((f, d), BF16), pltpu.SemaphoreType.DMA(())],
        ),
        compiler_params=_params(("arbitrary",)),
        name="expert_mlp2",
    )(block_e, next_expert, n_used, a, w2, b2)


def _combine_kernel(dest_ref, dest_next_ref, y_hbm, gate_ref, x2_ref, g_ref, o_ref, buf, sem, *, final_norm):
    i = pl.program_id(0)
    tc, d = x2_ref.shape
    njp = d // (2 * LANES)
    slot = i % 2

    def issue(dest, s):
        def body(t, c):
            for k in range(TOP_K):
                _token_copy(y_hbm, dest[0, 0, t * TOP_K + k], buf.at[s, k], t, njp, sem.at[s]).start(priority=k % 2)
            return c

        lax.fori_loop(0, tc, body, 0, unroll=2)

    @pl.when(i == 0)
    def _():
        issue(dest_ref, 0)

    @pl.when(i + 1 < pl.num_programs(0))
    def _():
        issue(dest_next_ref, 1 - slot)

    for k in range(TOP_K):
        pltpu.make_async_copy(y_hbm.at[pl.ds(0, tc * njp)], buf.at[slot, k, pl.ds(0, tc * njp)],
                              sem.at[slot]).wait()
    lo_cols, hi_cols = [], []
    for j in range(njp):
        acc_lo = x2_ref[:, j * LANES:(j + 1) * LANES]
        acc_hi = x2_ref[:, (njp + j) * LANES:(njp + j + 1) * LANES]
        for k in range(TOP_K):
            lo, hi = _load_token_cols(buf.at[slot, k], tc, njp, j)
            g = gate_ref[:, k:k + 1]
            acc_lo = acc_lo + g * lo
            acc_hi = acc_hi + g * hi
        lo_cols.append(acc_lo)
        hi_cols.append(acc_hi)
    acc = jnp.concatenate(lo_cols + hi_cols, axis=1)
    if final_norm:
        ms = jnp.mean(acc * acc, axis=-1, keepdims=True)
        acc = acc * lax.rsqrt(ms + RMS_EPS) * g_ref[...]
    o_ref[...] = acc


def _combine(y_tiles, dest, gates, x2, g_final, tc, final_norm):
    t, d = x2.shape
    njp = d // (2 * LANES)
    nt = t // tc
    dest3 = dest.reshape(nt, 1, tc * TOP_K)
    return pl.pallas_call(
        functools.partial(_combine_kernel, final_norm=final_norm),
        out_shape=jax.ShapeDtypeStruct((t, d), F32),
        grid=(nt,),
        in_specs=[
            pl.BlockSpec((1, 1, tc * TOP_K), lambda i: (i, 0, 0), memory_space=pltpu.SMEM),
            pl.BlockSpec((1, 1, tc * TOP_K), lambda i: (jnp.minimum(i + 1, nt - 1), 0, 0),
                         memory_space=pltpu.SMEM),
            pl.BlockSpec(memory_space=pl.ANY),
            pl.BlockSpec((tc, LANES), lambda i: (i, 0)),
            pl.BlockSpec((tc, d), lambda i: (i, 0)),
            pl.BlockSpec((1, d), lambda i: (0, 0)),
        ],
        out_specs=pl.BlockSpec((tc, d), lambda i: (i, 0)),
        scratch_shapes=[pltpu.VMEM((2, TOP_K, tc * _token_pitch(njp), LANES), U32),
                        pltpu.SemaphoreType.DMA((2,))],
        compiler_params=_params(("arbitrary",), disable_bounds_checks=True),
        name="moe_combine_final_norm",
    )(dest3, dest3, y_tiles, gates, x2, g_final.reshape(1, d))


def _layer(x, norm_mix_g, w_in, b_forget, w_branch_fox, w_branch_moba, w_out, norm_ffn_g,
           w_router, b_router, w_mlp1, b_mlp1, w_mlp2, b_mlp2):
    b, s, d = x.shape
    t = b * s
    width = N_HEADS * HEAD_DIM
    xt = x.reshape(t, d)
    q_scale = LOG2E * HEAD_DIM ** -0.5

    o = np.cumsum([0, width, width, width, N_HEADS, width, width, width, d, d])
    w_fox = w_in[:, o[0]:o[3]].astype(BF16)
    w_fl = jnp.pad(w_in[:, o[3]:o[4]], ((0, 0), (0, LANES - N_HEADS))).astype(BF16)
    w_mqk = w_in[:, o[4]:o[6]].astype(BF16)
    w_mv = w_in[:, o[6]:o[7]].astype(BF16)
    w_g = w_in[:, o[7]:o[9]].astype(BF16)
    colscale = jnp.concatenate([jnp.full((1, width), q_scale, F32), jnp.ones((1, 2 * width), F32)], axis=1)

    tm = min(1024, s)
    h = _rmsnorm(xt, norm_mix_g, tm=min(512, t))
    fox = _proj(h, w_fox, colscale, seq=s, tm=tm, tn=512, out_dtype=BF16, heads=True, name="proj_fox_qkv")
    mqk = _proj(h, w_mqk, colscale[:, :2 * width], seq=s, tm=tm, tn=512, out_dtype=BF16, heads=True,
                rope=_rope_tables(s), name="proj_moba_qk")
    mv = _proj(h, w_mv, colscale[:, width:2 * width], seq=s, tm=tm, tn=512, out_dtype=BF16, heads=True,
               name="proj_moba_v")
    gates = _proj(h, w_g, jnp.ones((1, 2 * d), F32), seq=s, tm=tm, tn=512, out_dtype=BF16, heads=False,
                  sigmoid=True, name="proj_gates")
    fl = _proj(h, w_fl, jnp.ones((1, LANES), F32), seq=s, tm=tm, tn=LANES, out_dtype=F32, heads=False,
               name="proj_forget")

    fl_t = fl[:, :N_HEADS].reshape(b, s, N_HEADS).transpose(0, 2, 1)
    c = _forget_cumsum(fl_t, b_forget)
    c5 = c.reshape(b, N_HEADS, s // MOBA_BLOCK, 1, MOBA_BLOCK)

    o_fox = _attention(fox, 0, fox, N_HEADS, fox, 2 * N_HEADS, c5, moba=False, name="fox_attention")
    o_moba = _attention(mqk, 0, mqk, N_HEADS, mv, 0, None, moba=True, name="moba_attention")

    w_r = jnp.pad(w_router, ((0, 0), (0, LANES - N_EXPERTS))).astype(BF16)
    b_r = jnp.concatenate([b_router.astype(F32), jnp.full((LANES - N_EXPERTS,), NEG_INF, F32)]).reshape(1, LANES)
    x2, h2, logits = _mix(o_fox, o_moba, gates, xt, w_branch_fox.astype(BF16), w_branch_moba.astype(BF16),
                          w_out.astype(BF16), norm_ffn_g.reshape(1, d), w_r, b_r, tm=min(256, t))

    meta, gate_w, cnt = _router(logits, tr=min(512, t))

    m = MOE_ROWS
    assign = t * TOP_K
    n_blocks = (assign + N_EXPERTS * (m - 1) + m - 1) // m
    counts = cnt[0, :N_EXPERTS].astype(jnp.int32)
    padded = (counts + m - 1) // m * m
    pend = jnp.cumsum(padded)
    pstart = pend - padded
    experts = jnp.arange(N_EXPERTS, dtype=jnp.int32)
    e_sel = meta[:, :TOP_K]
    dest = meta[:, TOP_K:2 * TOP_K] + jnp.sum(jnp.where(e_sel[..., None] == experts, pstart, 0), axis=-1)
    n_used = (pend[-1] // m).astype(jnp.int32).reshape(1)
    first_row = jnp.arange(n_blocks, dtype=jnp.int32) * m
    block_e = jnp.minimum(jnp.sum((pend[None, :] <= first_row[:, None]).astype(jnp.int32), axis=1), N_EXPERTS - 1)
    row_tok = jnp.zeros((n_blocks * m,), jnp.int32).at[dest.reshape(-1)].set(
        jnp.arange(assign, dtype=jnp.int32) // TOP_K, unique_indices=True, mode="promise_in_bounds")

    xb = _gather_rows(h2, row_tok, n_used, m, d // (2 * LANES))
    f = w_mlp1.shape[2] // 2
    b1g = b_mlp1[:, 0::2].reshape(N_EXPERTS, 1, f)
    b1l = b_mlp1[:, 1::2].reshape(N_EXPERTS, 1, f)
    later = (experts[None, :] > experts[:, None]) & (counts[None, :] > 0)
    next_expert = jnp.min(jnp.where(later, experts[None, :], N_EXPERTS), axis=1)
    next_expert = jnp.where(next_expert == N_EXPERTS, -1, next_expert).astype(jnp.int32)
    a = _gmm1(xb, w_mlp1, b1g, b1l, block_e, next_expert, n_used, m, tn=min(512, f))
    y = _gmm2(a, w_mlp2, b_mlp2.reshape(N_EXPERTS, 1, d), block_e, next_expert, n_used, m)
    return x2, y, dest, gate_w


def kernel(x, norm_mix_g, w_in, b_forget, w_branch_fox, w_branch_moba, w_out, norm_ffn_g, w_router, b_router,
           w_mlp1, b_mlp1, w_mlp2, b_mlp2, norm_final_g):
    depth = w_in.shape[0]
    b, s, d = x.shape
    for l in range(depth):
        x2, y, dest, gate_w = _layer(x, norm_mix_g[l], w_in[l], b_forget[l], w_branch_fox[l], w_branch_moba[l],
                                     w_out[l], norm_ffn_g[l], w_router[l], b_router[l], w_mlp1[l], b_mlp1[l],
                                     w_mlp2[l], b_mlp2[l])
        out = _combine(y, dest, gate_w, x2, norm_final_g, tc=min(128, b * s), final_norm=l == depth - 1)
        x = out.reshape(b, s, d)
    return x
```

```python
import functools
import math

import jax
import jax.numpy as jnp
import numpy as np
from jax import lax
from jax.experimental import pallas as pl
from jax.experimental.pallas import tpu as pltpu

HEAD_DIM = 128
N_HEADS = 8
MOBA_BLOCK = 256
MOBA_TOPK = 3
ROPE_THETA = 500000.0
ROPE_DIM = HEAD_DIM // 4
N_EXPERTS = 32
TOP_K = 4
SWIGLU_ALPHA = 1.702
SWIGLU_LIMIT = 7.0
RMS_EPS = 1e-5

LANES = 128
SUBLANES = 8
MXU_COLS = 256
VMEM_LIMIT_BYTES = 56 * 1024 * 1024
MOE_ROWS = 512
ATTN_Q_ROWS = 256

F32 = jnp.float32
BF16 = jnp.bfloat16
U32 = jnp.uint32
NEG_INF = float("-inf")
MASKED = -1e30
LOG2E = math.log2(math.e)


def _params(sem, **kw):
    return pltpu.CompilerParams(dimension_semantics=sem, vmem_limit_bytes=VMEM_LIMIT_BYTES, **kw)


def _pack_pairs(lo, hi):
    lo_b = lax.bitcast_convert_type(lo.astype(BF16).astype(F32), U32) >> 16
    hi_b = lax.bitcast_convert_type(hi.astype(BF16).astype(F32), U32) & jnp.uint32(0xFFFF0000)
    return hi_b | lo_b


def _unpack_pairs(w):
    lo = lax.bitcast_convert_type(w << 16, F32)
    hi = lax.bitcast_convert_type(w & jnp.uint32(0xFFFF0000), F32)
    return lo, hi


def _store_token_tiles(ref, val):
    n, d = val.shape
    njp = d // (2 * LANES)
    for j in range(njp):
        lo = val[:, j * LANES:(j + 1) * LANES]
        hi = val[:, (njp + j) * LANES:(njp + j + 1) * LANES]
        ref[pl.ds(j, n, stride=njp), :] = _pack_pairs(lo, hi)


def _token_pitch(njp):
    p = -(-njp // 4)
    return 4 * (p if p % 2 else p + 1)


def _token_copy(src_hbm, row, dst_vmem, slot, njp, sem):
    return pltpu.make_async_copy(src_hbm.at[pl.ds(pl.multiple_of(row * njp, njp), njp)],
                                 dst_vmem.at[pl.ds(slot * _token_pitch(njp), njp)], sem)


def _load_token_cols(buf, n, njp, j):
    return _unpack_pairs(buf[pl.ds(j, n, stride=_token_pitch(njp)), :])


def _rmsnorm_kernel(x_ref, g_ref, o_ref):
    x = x_ref[...]
    ms = jnp.mean(x * x, axis=-1, keepdims=True)
    o_ref[...] = (x * lax.rsqrt(ms + RMS_EPS) * g_ref[...]).astype(o_ref.dtype)


def _rmsnorm(x, g, tm):
    t, d = x.shape
    return pl.pallas_call(
        _rmsnorm_kernel,
        out_shape=jax.ShapeDtypeStruct((t, d), BF16),
        grid=(t // tm,),
        in_specs=[pl.BlockSpec((tm, d), lambda i: (i, 0)), pl.BlockSpec((1, d), lambda i: (0, 0))],
        out_specs=pl.BlockSpec((tm, d), lambda i: (i, 0)),
        compiler_params=_params(("parallel",)),
        name="rmsnorm_mix",
    )(x, g.reshape(1, d))


def _proj_kernel(a_ref, w_ref, cs_ref, *rest, rotary, sigmoid, heads):
    if rotary:
        cos_ref, sa_ref, sb_ref, o_ref = rest
    else:
        (o_ref,) = rest
    res = jnp.dot(a_ref[...], w_ref[...], preferred_element_type=F32) * cs_ref[...]
    if sigmoid:
        res = jax.nn.sigmoid(res)
    for j in range(res.shape[1] // LANES):
        r = res[:, j * LANES:(j + 1) * LANES]
        if rotary:
            half = ROPE_DIM // 2
            r = (r * cos_ref[...] + pltpu.roll(r, half, 1) * sa_ref[...]
                 + pltpu.roll(r, LANES - half, 1) * sb_ref[...])
        if heads:
            o_ref[0, j] = r.astype(o_ref.dtype)
        else:
            o_ref[:, j * LANES:(j + 1) * LANES] = r.astype(o_ref.dtype)


def _proj(a, w, colscale, *, seq, tm, tn, out_dtype, heads, sigmoid=False, rope=None, name):
    t, k = a.shape
    n = w.shape[1]
    sblk = seq // tm
    in_specs = [
        pl.BlockSpec((tm, k), lambda m, j: (m, 0)),
        pl.BlockSpec((k, tn), lambda m, j: (0, j)),
        pl.BlockSpec((1, tn), lambda m, j: (0, j)),
    ]
    args = [a, w, colscale]
    if rope is not None:
        in_specs += [pl.BlockSpec((tm, LANES), lambda m, j: (m % sblk, 0))] * 3
        args += list(rope)
    if heads:
        out_shape = jax.ShapeDtypeStruct((t // seq, n // LANES, seq, LANES), out_dtype)
        out_spec = pl.BlockSpec((1, tn // LANES, tm, LANES), lambda m, j: (m // sblk, j, m % sblk, 0))
    else:
        out_shape = jax.ShapeDtypeStruct((t, n), out_dtype)
        out_spec = pl.BlockSpec((tm, tn), lambda m, j: (m, j))
    return pl.pallas_call(
        functools.partial(_proj_kernel, rotary=rope is not None, sigmoid=sigmoid, heads=heads),
        out_shape=out_shape,
        grid=(t // tm, n // tn),
        in_specs=in_specs,
        out_specs=out_spec,
        compiler_params=_params(("parallel", "arbitrary")),
        name=name,
    )(*args)


def _rope_tables(seq):
    half = ROPE_DIM // 2
    inv_freq = 1.0 / (ROPE_THETA ** (jnp.arange(half, dtype=F32) / half))
    ang = jnp.arange(seq, dtype=F32)[:, None] * inv_freq[None, :]
    cos, sin = jnp.cos(ang), jnp.sin(ang)
    ones = jnp.ones((seq, LANES - ROPE_DIM), F32)
    zeros = jnp.zeros((seq, LANES - ROPE_DIM), F32)
    zh = jnp.zeros((seq, half), F32)
    cos_t = jnp.concatenate([cos, cos, ones], axis=1)
    sin_a = jnp.concatenate([zh, sin, zeros], axis=1)
    sin_b = jnp.concatenate([-sin, zh, zeros], axis=1)
    return cos_t, sin_a, sin_b


def _forget_kernel(fl_ref, b_ref, c_ref):
    z = fl_ref[...] + b_ref[...]
    logf = jnp.minimum(z, 0.0) - jnp.log1p(jnp.exp(-jnp.abs(z)))
    row = lax.broadcasted_iota(jnp.int32, (LANES, LANES), 0)
    col = lax.broadcasted_iota(jnp.int32, (LANES, LANES), 1)
    tri = (row <= col).astype(BF16)
    carry = jnp.zeros((logf.shape[0], 1), F32)
    for j in range(logf.shape[1] // LANES):
        xb = logf[:, j * LANES:(j + 1) * LANES]
        hi = xb.astype(BF16)
        r1 = xb - hi.astype(F32)
        mid = r1.astype(BF16)
        lo = (r1 - mid.astype(F32)).astype(BF16)
        cs = (jnp.dot(hi, tri, preferred_element_type=F32) + jnp.dot(mid, tri, preferred_element_type=F32)
              + jnp.dot(lo, tri, preferred_element_type=F32)) + carry
        c_ref[:, j * LANES:(j + 1) * LANES] = cs * LOG2E
        carry = cs[:, LANES - 1:LANES]


def _forget_cumsum(fl_t, b_forget):
    b, h, s = fl_t.shape
    return pl.pallas_call(
        _forget_kernel,
        out_shape=jax.ShapeDtypeStruct((b, h, s), F32),
        grid=(b,),
        in_specs=[pl.BlockSpec((None, h, s), lambda i: (i, 0, 0)), pl.BlockSpec((h, 1), lambda i: (0, 0))],
        out_specs=pl.BlockSpec((None, h, s), lambda i: (i, 0, 0)),
        compiler_params=_params(("parallel",)),
        name="forget_cumsum",
    )(fl_t, b_forget.reshape(h, 1).astype(F32))


def _attn_kernel(*refs, moba, tq, tk):
    if moba:
        q_ref, k_ref, v_ref, o_ref, km_ref, kaug_ref = refs
    else:
        q_ref, k_ref, v_ref, c_ref, o_ref = refs
    nkb = q_ref.shape[0] // tk
    nt = (((1,), (1,)), ((), ()))
    row = lax.broadcasted_iota(jnp.int32, (tq, tk), 0)
    col = lax.broadcasted_iota(jnp.int32, (tq, tk), 1)

    if moba:
        lane = lax.broadcasted_iota(jnp.int32, (tk, LANES), 1)
        km_ref[...] = jnp.zeros_like(km_ref)
        for n in range(nkb):
            kb = k_ref[n * tk:(n + 1) * tk, :]
            km_ref[n:n + 1, :] = jnp.mean(kb.astype(F32), axis=0, keepdims=True)
            kaug_ref[n * tk:(n + 1) * tk, :LANES] = kb
            kaug_ref[n * tk:(n + 1) * tk, LANES:] = (lane == n).astype(BF16)
        kmean = km_ref[...].astype(BF16)
        blk = lax.broadcasted_iota(jnp.int32, (SUBLANES, tk), 0)

    def keys(n):
        return (kaug_ref if moba else k_ref)[n * tk:(n + 1) * tk, :]

    for nb in range(nkb):
        if moba and nb > 0:
            g = lax.dot_general(kmean, q_ref[nb * tk:(nb + 1) * tk, :], nt, preferred_element_type=F32)[:SUBLANES]
            past = blk < nb
            pen = jnp.zeros((SUBLANES, tk), F32)
            for n in range(nb):
                g_n = g[n:n + 1, :]
                beats = ((g > g_n) | ((g == g_n) & (blk < n))) & past
                rank = jnp.sum(beats.astype(F32), axis=0, keepdims=True)
                pen = jnp.where((blk == n) & (rank >= MOBA_TOPK), MASKED, pen)
            pen = jnp.concatenate([pen, jnp.zeros((LANES - SUBLANES, tk), F32)], axis=0)
            pen_cols = pen.T.astype(BF16)
        for h in range(tk // tq):
            r0 = nb * tk + h * tq
            q = q_ref[r0:r0 + tq, :]
            s = lax.dot_general(q, k_ref[nb * tk:(nb + 1) * tk, :], nt, preferred_element_type=F32)
            if not moba:
                s = s - c_ref[nb]
            s = jnp.where(row + h * tq >= col, s, NEG_INF)
            m = jnp.max(s, axis=1, keepdims=True)
            p = jnp.exp2(s - m)
            l = jnp.sum(p, axis=1, keepdims=True)
            acc = jnp.dot(p.astype(BF16), v_ref[nb * tk:(nb + 1) * tk, :], preferred_element_type=F32)
            if moba and nb > 0:
                q = jnp.concatenate([q, pen_cols[h * tq:(h + 1) * tq, :]], axis=1)
            for n in range(nb):
                s = lax.dot_general(q, keys(n), nt, preferred_element_type=F32)
                if not moba:
                    s = s - c_ref[n]
                m_new = jnp.maximum(m, jnp.max(s, axis=1, keepdims=True))
                alpha = jnp.exp2(m - m_new)
                p = jnp.exp2(s - m_new)
                l = alpha * l + jnp.sum(p, axis=1, keepdims=True)
                acc = alpha * acc + jnp.dot(p.astype(BF16), v_ref[n * tk:(n + 1) * tk, :],
                                            preferred_element_type=F32)
                m = m_new
            o_ref[r0:r0 + tq, :] = (acc / l).astype(o_ref.dtype)


def _attention(q_arr, q_off, k_arr, k_off, v_arr, v_off, c5, *, moba, name):
    b, _, s, d = q_arr.shape
    tk = MOBA_BLOCK
    tq = min(ATTN_Q_ROWS, tk)
    in_specs = [
        pl.BlockSpec((None, None, s, d), lambda bi, h: (bi, q_off + h, 0, 0)),
        pl.BlockSpec((None, None, s, d), lambda bi, h: (bi, k_off + h, 0, 0)),
        pl.BlockSpec((None, None, s, d), lambda bi, h: (bi, v_off + h, 0, 0)),
    ]
    args = [q_arr, k_arr, v_arr]
    scratch = []
    if moba:
        scratch = [pltpu.VMEM((LANES, d), F32), pltpu.VMEM((s, d + LANES), BF16)]
    else:
        in_specs.append(pl.BlockSpec((None, None, s // tk, 1, tk), lambda bi, h: (bi, h, 0, 0, 0)))
        args.append(c5)
    return pl.pallas_call(
        functools.partial(_attn_kernel, moba=moba, tq=tq, tk=tk),
        out_shape=jax.ShapeDtypeStruct((b * s, N_HEADS * d), BF16),
        grid=(b, N_HEADS),
        in_specs=in_specs,
        out_specs=pl.BlockSpec((s, d), lambda bi, h: (bi, h)),
        scratch_shapes=scratch,
        compiler_params=_params(("parallel", "parallel")),
        name=name,
    )(*args)


def _mix_kernel(of_ref, om_ref, g_ref, x_ref, wf_ref, wm_ref, wo_ref, gn_ref, wr_ref, br_ref,
                x2_ref, h2_ref, lg_ref):
    d = x_ref.shape[1]
    yf = jnp.dot(of_ref[...], wf_ref[...], preferred_element_type=F32)
    ym = jnp.dot(om_ref[...], wm_ref[...], preferred_element_type=F32)
    mixed = g_ref[:, :d].astype(F32) * yf + g_ref[:, d:].astype(F32) * ym
    x2 = x_ref[...] + jnp.dot(mixed.astype(BF16), wo_ref[...], preferred_element_type=F32)
    x2_ref[...] = x2
    ms = jnp.mean(x2 * x2, axis=-1, keepdims=True)
    h2 = x2 * lax.rsqrt(ms + RMS_EPS) * gn_ref[...]
    _store_token_tiles(h2_ref, h2)
    lg_ref[...] = jnp.dot(h2.astype(BF16), wr_ref[...], preferred_element_type=F32) + br_ref[...]


def _mix(o_fox, o_moba, gates, x, w_f, w_m, w_o, g_ffn, w_r, b_r, tm):
    t, d = x.shape
    w = o_fox.shape[1]
    njp = d // (2 * LANES)
    const = lambda i: (0, 0)
    resident = dict(pipeline_mode=pl.Buffered(1))
    return pl.pallas_call(
        _mix_kernel,
        out_shape=(jax.ShapeDtypeStruct((t, d), F32), jax.ShapeDtypeStruct((t * njp, LANES), U32),
                   jax.ShapeDtypeStruct((t, LANES), F32)),
        grid=(t // tm,),
        in_specs=[
            pl.BlockSpec((tm, w), lambda i: (i, 0)),
            pl.BlockSpec((tm, w), lambda i: (i, 0)),
            pl.BlockSpec((tm, 2 * d), lambda i: (i, 0)),
            pl.BlockSpec((tm, d), lambda i: (i, 0)),
            pl.BlockSpec((w, d), const, **resident),
            pl.BlockSpec((w, d), const, **resident),
            pl.BlockSpec((d, d), const, **resident),
            pl.BlockSpec((1, d), const),
            pl.BlockSpec((d, LANES), const),
            pl.BlockSpec((1, LANES), const),
        ],
        out_specs=(pl.BlockSpec((tm, d), lambda i: (i, 0)), pl.BlockSpec((tm * njp, LANES), lambda i: (i, 0)),
                   pl.BlockSpec((tm, LANES), lambda i: (i, 0))),
        compiler_params=_params(("parallel",)),
        name="mix_out_router",
    )(o_fox, o_moba, gates, x, w_f, w_m, w_o, g_ffn, w_r, b_r)


def _router_kernel(lg_ref, meta_ref, gate_ref, cnt_ref, carry_ref):
    @pl.when(pl.program_id(0) == 0)
    def _():
        carry_ref[...] = jnp.zeros_like(carry_ref)

    l = lg_ref[...]
    tr = l.shape[0]
    lane = lax.broadcasted_iota(jnp.int32, (tr, LANES), 1).astype(F32)
    vals, idxs = [], []
    for _ in range(TOP_K):
        m = jnp.max(l, axis=1, keepdims=True)
        ix = jnp.min(jnp.where(l == m, lane, float(LANES)), axis=1, keepdims=True)
        vals.append(m)
        idxs.append(ix)
        l = jnp.where(lane == ix, NEG_INF, l)
    ex = [jnp.exp(v - vals[0]) for v in vals]
    den = ex[0] + ex[1] + ex[2] + ex[3]
    onehot = jnp.zeros((tr, LANES), F32)
    for ix in idxs:
        onehot = onehot + (lane == ix).astype(F32)
    row = lax.broadcasted_iota(jnp.int32, (tr, tr), 0)
    col = lax.broadcasted_iota(jnp.int32, (tr, tr), 1)
    before = (col < row).astype(BF16)
    running = jnp.dot(before, onehot.astype(BF16), preferred_element_type=F32) + carry_ref[...]
    meta = jnp.zeros((tr, LANES), F32)
    gate = jnp.zeros((tr, LANES), F32)
    for k in range(TOP_K):
        pos = jnp.sum(jnp.where(lane == idxs[k], running, 0.0), axis=1, keepdims=True)
        meta = jnp.where(lane == float(k), idxs[k], meta)
        meta = jnp.where(lane == float(TOP_K + k), pos, meta)
        gate = jnp.where(lane == float(k), ex[k] / den, gate)
    meta_ref[...] = meta.astype(jnp.int32)
    gate_ref[...] = gate
    carry_ref[...] += jnp.sum(onehot, axis=0, keepdims=True)
    cnt_ref[...] = carry_ref[...]


def _router(logits, tr):
    t = logits.shape[0]
    return pl.pallas_call(
        _router_kernel,
        out_shape=(jax.ShapeDtypeStruct((t, LANES), jnp.int32), jax.ShapeDtypeStruct((t, LANES), F32),
                   jax.ShapeDtypeStruct((1, LANES), F32)),
        grid=(t // tr,),
        in_specs=[pl.BlockSpec((tr, LANES), lambda i: (i, 0))],
        out_specs=(pl.BlockSpec((tr, LANES), lambda i: (i, 0)), pl.BlockSpec((tr, LANES), lambda i: (i, 0)),
                   pl.BlockSpec((1, LANES), lambda i: (0, 0))),
        scratch_shapes=[pltpu.VMEM((1, LANES), F32)],
        compiler_params=_params(("arbitrary",)),
        name="router_topk",
    )(logits)


def _gather_kernel(nused_ref, tok_ref, tok_next_ref, h_hbm, o_ref, buf, sem, *, njp):
    r = pl.program_id(0)
    m = o_ref.shape[0]
    slot = r % 2

    def issue(tok, s):
        def body(i, c):
            _token_copy(h_hbm, tok[0, 0, 2 * i], buf.at[s], 2 * i, njp, sem.at[s]).start(priority=0)
            _token_copy(h_hbm, tok[0, 0, 2 * i + 1], buf.at[s], 2 * i + 1, njp, sem.at[s]).start(priority=1)
            return c

        lax.fori_loop(0, m // 2, body, 0, unroll=4)

    @pl.when(r == 0)
    def _():
        issue(tok_ref, 0)

    @pl.when(r + 1 < nused_ref[0])
    def _():
        issue(tok_next_ref, 1 - slot)

    @pl.when(r < nused_ref[0])
    def _():
        pltpu.make_async_copy(h_hbm.at[pl.ds(0, m * njp)], buf.at[slot, pl.ds(0, m * njp)], sem.at[slot]).wait()
        for j in range(njp):
            lo, hi = _load_token_cols(buf.at[slot], m, njp, j)
            o_ref[:, j * LANES:(j + 1) * LANES] = lo.astype(o_ref.dtype)
            o_ref[:, (njp + j) * LANES:(njp + j + 1) * LANES] = hi.astype(o_ref.dtype)

    @pl.when(r >= nused_ref[0])
    def _():
        o_ref[...] = jnp.zeros_like(o_ref)


def _gather_rows(h_tiles, row_tok, n_used, m, njp):
    n_blocks = row_tok.shape[0] // m
    tok3 = row_tok.reshape(n_blocks, 1, m)
    return pl.pallas_call(
        functools.partial(_gather_kernel, njp=njp),
        out_shape=jax.ShapeDtypeStruct((n_blocks * m, 2 * njp * LANES), BF16),
        grid_spec=pltpu.PrefetchScalarGridSpec(
            num_scalar_prefetch=1,
            grid=(n_blocks,),
            in_specs=[
                pl.BlockSpec((1, 1, m), lambda r, nu: (r, 0, 0), memory_space=pltpu.SMEM),
                pl.BlockSpec((1, 1, m), lambda r, nu: (jnp.minimum(r + 1, n_blocks - 1), 0, 0),
                             memory_space=pltpu.SMEM),
                pl.BlockSpec(memory_space=pl.ANY),
            ],
            out_specs=pl.BlockSpec((m, 2 * njp * LANES), lambda r, nu: (r, 0)),
            scratch_shapes=[pltpu.VMEM((2, m * _token_pitch(njp), LANES), U32), pltpu.SemaphoreType.DMA((2,))],
        ),
        compiler_params=_params(("arbitrary",), disable_bounds_checks=True),
        name="expert_row_gather",
    )(n_used, tok3, tok3, h_tiles)


def _group_start(be_ref, r):
    return jnp.logical_or(r == 0, be_ref[r] != be_ref[jnp.maximum(r - 1, 0)])


def _gmm1_kernel(be_ref, nne_ref, nused_ref, x_ref, w_hbm, perm_ref, bg_ref, bl_ref, a_ref,
                 stage, wg_s, wl_s, sem):
    j = pl.program_id(0)
    r = pl.program_id(1)
    pw = perm_ref.shape[0]
    tw = stage.shape[1]

    def fetch(jj, e):
        return pltpu.make_async_copy(w_hbm.at[e, :, pl.ds(pl.multiple_of(jj * tw, tw), tw)], stage, sem)

    @pl.when((j == 0) & (r == 0))
    def _():
        fetch(0, be_ref[0]).start()

    @pl.when(r < nused_ref[0])
    def _():
        @pl.when(_group_start(be_ref, r))
        def _():
            fetch(j, be_ref[r]).wait()
            for c in range(tw // pw):
                wb = stage[:, c * pw:(c + 1) * pw].astype(BF16)
                sp = jnp.dot(wb, perm_ref[...], preferred_element_type=F32).astype(BF16)
                wg_s[:, c * (pw // 2):(c + 1) * (pw // 2)] = sp[:, :pw // 2]
                wl_s[:, c * (pw // 2):(c + 1) * (pw // 2)] = sp[:, pw // 2:]
            nxt = nne_ref[be_ref[r]]

            @pl.when(nxt >= 0)
            def _():
                fetch(j, nxt).start()

            @pl.when((nxt < 0) & (j + 1 < pl.num_programs(0)))
            def _():
                fetch(j + 1, be_ref[0]).start()

        x = x_ref[...]
        ug = jnp.dot(x, wg_s[...], preferred_element_type=F32) + bg_ref[...]
        ul = jnp.dot(x, wl_s[...], preferred_element_type=F32) + bl_ref[...]
        glu = jnp.minimum(ug, SWIGLU_LIMIT)
        lin = jnp.clip(ul, -SWIGLU_LIMIT, SWIGLU_LIMIT)
        a_ref[...] = (glu * jax.nn.sigmoid(SWIGLU_ALPHA * glu) * (lin + 1.0)).astype(a_ref.dtype)

    @pl.when(r >= nused_ref[0])
    def _():
        a_ref[...] = jnp.zeros_like(a_ref)


def _deinterleave_perm(width):
    src = np.arange(width)
    dst = np.where(src % 2 == 0, src // 2, width // 2 + src // 2)
    p = np.zeros((width, width), np.float32)
    p[src, dst] = 1.0
    return jnp.asarray(p, BF16)


def _gmm1(xb, w1, b1g, b1l, block_e, next_expert, n_used, m, tn):
    rows, d = xb.shape
    f = w1.shape[2] // 2
    n_blocks = rows // m
    pw = 2 * LANES

    def rc(r, nu):
        return jnp.minimum(r, nu[0] - 1)

    return pl.pallas_call(
        _gmm1_kernel,
        out_shape=jax.ShapeDtypeStruct((rows, f), BF16),
        grid_spec=pltpu.PrefetchScalarGridSpec(
            num_scalar_prefetch=3,
            grid=(f // tn, n_blocks),
            in_specs=[
                pl.BlockSpec((m, d), lambda j, r, be, nne, nu: (rc(r, nu), 0)),
                pl.BlockSpec(memory_space=pl.ANY),
                pl.BlockSpec((pw, pw), lambda j, r, be, nne, nu: (0, 0)),
                pl.BlockSpec((None, 1, tn), lambda j, r, be, nne, nu: (be[rc(r, nu)], 0, j)),
                pl.BlockSpec((None, 1, tn), lambda j, r, be, nne, nu: (be[rc(r, nu)], 0, j)),
            ],
            out_specs=pl.BlockSpec((m, tn), lambda j, r, be, nne, nu: (r, j)),
            scratch_shapes=[pltpu.VMEM((d, 2 * tn), F32), pltpu.VMEM((d, tn), BF16), pltpu.VMEM((d, tn), BF16),
                            pltpu.SemaphoreType.DMA(())],
        ),
        compiler_params=_params(("arbitrary", "arbitrary")),
        name="expert_mlp1_swiglu",
    )(block_e, next_expert, n_used, xb, w1, _deinterleave_perm(pw), b1g, b1l)


def _gmm2_kernel(be_ref, nne_ref, nused_ref, a_ref, w_hbm, b_ref, y_ref, stage, w_s, sem):
    r = pl.program_id(0)

    def fetch(e):
        return pltpu.make_async_copy(w_hbm.at[e], stage, sem)

    @pl.when(r == 0)
    def _():
        fetch(be_ref[0]).start()

    @pl.when(r < nused_ref[0])
    def _():
        @pl.when(_group_start(be_ref, r))
        def _():
            fetch(be_ref[r]).wait()
            w_s[...] = stage[...].astype(BF16)
            nxt = nne_ref[be_ref[r]]

            @pl.when(nxt >= 0)
            def _():
                fetch(nxt).start()

        a = a_ref[...]
        n = a.shape[0]
        half = w_s.shape[1] // 2
        njp = half // LANES
        cw = min(MXU_COLS, half)
        for c in range(half // cw):
            lo_cols = slice(c * cw, (c + 1) * cw)
            hi_cols = slice(half + c * cw, half + (c + 1) * cw)
            y_lo = jnp.dot(a, w_s[:, lo_cols], preferred_element_type=F32) + b_ref[:, lo_cols]
            y_hi = jnp.dot(a, w_s[:, hi_cols], preferred_element_type=F32) + b_ref[:, hi_cols]
            for jj in range(cw // LANES):
                lanes = slice(jj * LANES, (jj + 1) * LANES)
                y_ref[pl.ds(c * (cw // LANES) + jj, n, stride=njp), :] = _pack_pairs(y_lo[:, lanes], y_hi[:, lanes])

    @pl.when(r >= nused_ref[0])
    def _():
        y_ref[...] = jnp.zeros_like(y_ref)


def _gmm2(a, w2, b2, block_e, next_expert, n_used, m):
    rows, f = a.shape
    d = w2.shape[2]
    njp = d // (2 * LANES)
    n_blocks = rows // m

    def rc(r, nu):
        return jnp.minimum(r, nu[0] - 1)

    return pl.pallas_call(
        _gmm2_kernel,
        out_shape=jax.ShapeDtypeStruct((rows * njp, LANES), U32),
        grid_spec=pltpu.PrefetchScalarGridSpec(
            num_scalar_prefetch=3,
            grid=(n_blocks,),
            in_specs=[
                pl.BlockSpec((m, f), lambda r, be, nne, nu: (rc(r, nu), 0)),
                pl.BlockSpec(memory_space=pl.ANY),
                pl.BlockSpec((None, 1, d), lambda r, be, nne, nu: (be[rc(r, nu)], 0, 0)),
            ],
            out_specs=pl.BlockSpec((m * njp, LANES), lambda r, be, nne, nu: (r, 0)),
            scratch_shapes=[pltpu.VMEM((f, d), F32), pltpu.VMEM((f, d), BF16), pltpu.SemaphoreType.DMA(())],
        ),
        compiler_params=_params(("arbitrary",)),
        name="expert_mlp2",
    )(block_e, next_expert, n_used, a, w2, b2)


def _combine_kernel(dest_ref, dest_next_ref, y_hbm, gate_ref, x2_ref, g_ref, o_ref, buf, sem, *, final_norm):
    i = pl.program_id(0)
    tc, d = x2_ref.shape
    njp = d // (2 * LANES)
    slot = i % 2

    def issue(dest, s):
        def body(t, c):
            for k in range(TOP_K):
                _token_copy(y_hbm, dest[0, 0, t * TOP_K + k], buf.at[s, k], t, njp, sem.at[s]).start(priority=k % 2)
            return c

        lax.fori_loop(0, tc, body, 0, unroll=2)

    @pl.when(i == 0)
    def _():
        issue(dest_ref, 0)

    @pl.when(i + 1 < pl.num_programs(0))
    def _():
        issue(dest_next_ref, 1 - slot)

    for k in range(TOP_K):
        pltpu.make_async_copy(y_hbm.at[pl.ds(0, tc * njp)], buf.at[slot, k, pl.ds(0, tc * njp)],
                              sem.at[slot]).wait()
    lo_cols, hi_cols = [], []
    for j in range(njp):
        acc_lo = x2_ref[:, j * LANES:(j + 1) * LANES]
        acc_hi = x2_ref[:, (njp + j) * LANES:(njp + j + 1) * LANES]
        for k in range(TOP_K):
            lo, hi = _load_token_cols(buf.at[slot, k], tc, njp, j)
            g = gate_ref[:, k:k + 1]
            acc_lo = acc_lo + g * lo
            acc_hi = acc_hi + g * hi
        lo_cols.append(acc_lo)
        hi_cols.append(acc_hi)
    acc = jnp.concatenate(lo_cols + hi_cols, axis=1)
    if final_norm:
        ms = jnp.mean(acc * acc, axis=-1, keepdims=True)
        acc = acc * lax.rsqrt(ms + RMS_EPS) * g_ref[...]
    o_ref[...] = acc


def _combine(y_tiles, dest, gates, x2, g_final, tc, final_norm):
    t, d = x2.shape
    njp = d // (2 * LANES)
    nt = t // tc
    dest3 = dest.reshape(nt, 1, tc * TOP_K)
    return pl.pallas_call(
        functools.partial(_combine_kernel, final_norm=final_norm),
        out_shape=jax.ShapeDtypeStruct((t, d), F32),
        grid=(nt,),
        in_specs=[
            pl.BlockSpec((1, 1, tc * TOP_K), lambda i: (i, 0, 0), memory_space=pltpu.SMEM),
            pl.BlockSpec((1, 1, tc * TOP_K), lambda i: (jnp.minimum(i + 1, nt - 1), 0, 0),
                         memory_space=pltpu.SMEM),
            pl.BlockSpec(memory_space=pl.ANY),
            pl.BlockSpec((tc, LANES), lambda i: (i, 0)),
            pl.BlockSpec((tc, d), lambda i: (i, 0)),
            pl.BlockSpec((1, d), lambda i: (0, 0)),
        ],
        out_specs=pl.BlockSpec((tc, d), lambda i: (i, 0)),
        scratch_shapes=[pltpu.VMEM((2, TOP_K, tc * _token_pitch(njp), LANES), U32),
                        pltpu.SemaphoreType.DMA((2,))],
        compiler_params=_params(("arbitrary",), disable_bounds_checks=True),
        name="moe_combine_final_norm",
    )(dest3, dest3, y_tiles, gates, x2, g_final.reshape(1, d))


def _layer(x, norm_mix_g, w_in, b_forget, w_branch_fox, w_branch_moba, w_out, norm_ffn_g,
           w_router, b_router, w_mlp1, b_mlp1, w_mlp2, b_mlp2):
    b, s, d = x.shape
    t = b * s
    width = N_HEADS * HEAD_DIM
    xt = x.reshape(t, d)
    q_scale = LOG2E * HEAD_DIM ** -0.5

    o = np.cumsum([0, width, width, width, N_HEADS, width, width, width, d, d])
    w_fox = w_in[:, o[0]:o[3]].astype(BF16)
    w_fl = jnp.pad(w_in[:, o[3]:o[4]], ((0, 0), (0, LANES - N_HEADS))).astype(BF16)
    w_mqk = w_in[:, o[4]:o[6]].astype(BF16)
    w_mv = w_in[:, o[6]:o[7]].astype(BF16)
    w_g = w_in[:, o[7]:o[9]].astype(BF16)
    colscale = jnp.concatenate([jnp.full((1, width), q_scale, F32), jnp.ones((1, 2 * width), F32)], axis=1)

    tm = min(2048, s)
    h = _rmsnorm(xt, norm_mix_g, tm=min(512, t))
    fox = _proj(h, w_fox, colscale, seq=s, tm=tm, tn=512, out_dtype=BF16, heads=True, name="proj_fox_qkv")
    mqk = _proj(h, w_mqk, colscale[:, :2 * width], seq=s, tm=tm, tn=512, out_dtype=BF16, heads=True,
                rope=_rope_tables(s), name="proj_moba_qk")
    mv = _proj(h, w_mv, colscale[:, width:2 * width], seq=s, tm=tm, tn=512, out_dtype=BF16, heads=True,
               name="proj_moba_v")
    gates = _proj(h, w_g, jnp.ones((1, 2 * d), F32), seq=s, tm=tm, tn=512, out_dtype=BF16, heads=False,
                  sigmoid=True, name="proj_gates")
    fl = _proj(h, w_fl, jnp.ones((1, LANES), F32), seq=s, tm=tm, tn=LANES, out_dtype=F32, heads=False,
               name="proj_forget")

    fl_t = fl[:, :N_HEADS].reshape(b, s, N_HEADS).transpose(0, 2, 1)
    c = _forget_cumsum(fl_t, b_forget)
    c5 = c.reshape(b, N_HEADS, s // MOBA_BLOCK, 1, MOBA_BLOCK)

    o_fox = _attention(fox, 0, fox, N_HEADS, fox, 2 * N_HEADS, c5, moba=False, name="fox_attention")
    o_moba = _attention(mqk, 0, mqk, N_HEADS, mv, 0, None, moba=True, name="moba_attention")

    w_r = jnp.pad(w_router, ((0, 0), (0, LANES - N_EXPERTS))).astype(BF16)
    b_r = jnp.concatenate([b_router.astype(F32), jnp.full((LANES - N_EXPERTS,), NEG_INF, F32)]).reshape(1, LANES)
    x2, h2, logits = _mix(o_fox, o_moba, gates, xt, w_branch_fox.astype(BF16), w_branch_moba.astype(BF16),
                          w_out.astype(BF16), norm_ffn_g.reshape(1, d), w_r, b_r, tm=min(256, t))

    meta, gate_w, cnt = _router(logits, tr=min(512, t))

    m = MOE_ROWS
    assign = t * TOP_K
    n_blocks = (assign + N_EXPERTS * (m - 1) + m - 1) // m
    counts = cnt[0, :N_EXPERTS].astype(jnp.int32)
    padded = (counts + m - 1) // m * m
    pend = jnp.cumsum(padded)
    pstart = pend - padded
    experts = jnp.arange(N_EXPERTS, dtype=jnp.int32)
    e_sel = meta[:, :TOP_K]
    dest = meta[:, TOP_K:2 * TOP_K] + jnp.sum(jnp.where(e_sel[..., None] == experts, pstart, 0), axis=-1)
    n_used = (pend[-1] // m).astype(jnp.int32).reshape(1)
    first_row = jnp.arange(n_blocks, dtype=jnp.int32) * m
    block_e = jnp.minimum(jnp.sum((pend[None, :] <= first_row[:, None]).astype(jnp.int32), axis=1), N_EXPERTS - 1)
    row_tok = jnp.zeros((n_blocks * m,), jnp.int32).at[dest.reshape(-1)].set(
        jnp.arange(assign, dtype=jnp.int32) // TOP_K, unique_indices=True, mode="promise_in_bounds")

    xb = _gather_rows(h2, row_tok, n_used, m, d // (2 * LANES))
    f = w_mlp1.shape[2] // 2
    b1g = b_mlp1[:, 0::2].reshape(N_EXPERTS, 1, f)
    b1l = b_mlp1[:, 1::2].reshape(N_EXPERTS, 1, f)
    later = (experts[None, :] > experts[:, None]) & (counts[None, :] > 0)
    next_expert = jnp.min(jnp.where(later, experts[None, :], N_EXPERTS), axis=1)
    next_expert = jnp.where(next_expert == N_EXPERTS, -1, next_expert).astype(jnp.int32)
    a = _gmm1(xb, w_mlp1, b1g, b1l, block_e, next_expert, n_used, m, tn=min(1024, f))
    y = _gmm2(a, w_mlp2, b_mlp2.reshape(N_EXPERTS, 1, d), block_e, next_expert, n_used, m)
    return x2, y, dest, gate_w


def kernel(x, norm_mix_g, w_in, b_forget, w_branch_fox, w_branch_moba, w_out, norm_ffn_g, w_router, b_router,
           w_mlp1, b_mlp1, w_mlp2, b_mlp2, norm_final_g):
    depth = w_in.shape[0]
    b, s, d = x.shape
    for l in range(depth):
        x2, y, dest, gate_w = _layer(x, norm_mix_g[l], w_in[l], b_forget[l], w_branch_fox[l], w_branch_moba[l],
                                     w_out[l], norm_ffn_g[l], w_router[l], b_router[l], w_mlp1[l], b_mlp1[l],
                                     w_mlp2[l], b_mlp2[l])
        out = _combine(y, dest, gate_w, x2, norm_final_g, tc=min(128, b * s), final_norm=l == depth - 1)
        x = out.reshape(b, s, d)
    return x
```

```python
import functools
import math

import jax
import jax.numpy as jnp
import numpy as np
from jax import lax
from jax.experimental import pallas as pl
from jax.experimental.pallas import tpu as pltpu

HEAD_DIM = 128
N_HEADS = 8
MOBA_BLOCK = 256
MOBA_TOPK = 3
ROPE_THETA = 500000.0
ROPE_DIM = HEAD_DIM // 4
N_EXPERTS = 32
TOP_K = 4
SWIGLU_ALPHA = 1.702
SWIGLU_LIMIT = 7.0
RMS_EPS = 1e-5

LANES = 128
SUBLANES = 8
MXU_COLS = 256
VMEM_LIMIT_BYTES = 56 * 1024 * 1024
MOE_ROWS = 512
ATTN_Q_ROWS = 256

F32 = jnp.float32
BF16 = jnp.bfloat16
U32 = jnp.uint32
NEG_INF = float("-inf")
MASKED = -1e30
LOG2E = math.log2(math.e)


def _params(sem, **kw):
    return pltpu.CompilerParams(dimension_semantics=sem, vmem_limit_bytes=VMEM_LIMIT_BYTES, **kw)


def _pack_pairs(lo, hi):
    lo_b = lax.bitcast_convert_type(lo.astype(BF16).astype(F32), U32) >> 16
    hi_b = lax.bitcast_convert_type(hi.astype(BF16).astype(F32), U32) & jnp.uint32(0xFFFF0000)
    return hi_b | lo_b


def _unpack_pairs(w):
    lo = lax.bitcast_convert_type(w << 16, F32)
    hi = lax.bitcast_convert_type(w & jnp.uint32(0xFFFF0000), F32)
    return lo, hi


def _store_token_tiles(ref, val):
    n, d = val.shape
    njp = d // (2 * LANES)
    for j in range(njp):
        lo = val[:, j * LANES:(j + 1) * LANES]
        hi = val[:, (njp + j) * LANES:(njp + j + 1) * LANES]
        ref[pl.ds(j, n, stride=njp), :] = _pack_pairs(lo, hi)


def _token_pitch(njp):
    p = -(-njp // 4)
    return 4 * (p if p % 2 else p + 1)


def _token_copy(src_hbm, row, dst_vmem, slot, njp, sem):
    return pltpu.make_async_copy(src_hbm.at[pl.ds(pl.multiple_of(row * njp, njp), njp)],
                                 dst_vmem.at[pl.ds(slot * _token_pitch(njp), njp)], sem)


def _load_token_cols(buf, n, njp, j):
    return _unpack_pairs(buf[pl.ds(j, n, stride=_token_pitch(njp)), :])


def _rmsnorm_kernel(x_ref, g_ref, o_ref):
    x = x_ref[...]
    ms = jnp.mean(x * x, axis=-1, keepdims=True)
    o_ref[...] = (x * lax.rsqrt(ms + RMS_EPS) * g_ref[...]).astype(o_ref.dtype)


def _rmsnorm(x, g, tm):
    t, d = x.shape
    return pl.pallas_call(
        _rmsnorm_kernel,
        out_shape=jax.ShapeDtypeStruct((t, d), BF16),
        grid=(t // tm,),
        in_specs=[pl.BlockSpec((tm, d), lambda i: (i, 0)), pl.BlockSpec((1, d), lambda i: (0, 0))],
        out_specs=pl.BlockSpec((tm, d), lambda i: (i, 0)),
        compiler_params=_params(("parallel",)),
        name="rmsnorm_mix",
    )(x, g.reshape(1, d))


def _proj_kernel(a_ref, w_ref, cs_ref, *rest, rotary, sigmoid, heads):
    if rotary:
        cos_ref, sa_ref, sb_ref, o_ref = rest
    else:
        (o_ref,) = rest
    res = jnp.dot(a_ref[...], w_ref[...], preferred_element_type=F32) * cs_ref[...]
    if sigmoid:
        res = jax.nn.sigmoid(res)
    for j in range(res.shape[1] // LANES):
        r = res[:, j * LANES:(j + 1) * LANES]
        if rotary:
            half = ROPE_DIM // 2
            r = (r * cos_ref[...] + pltpu.roll(r, half, 1) * sa_ref[...]
                 + pltpu.roll(r, LANES - half, 1) * sb_ref[...])
        if heads:
            o_ref[0, j] = r.astype(o_ref.dtype)
        else:
            o_ref[:, j * LANES:(j + 1) * LANES] = r.astype(o_ref.dtype)


def _proj(a, w, colscale, *, seq, tm, tn, out_dtype, heads, sigmoid=False, rope=None, name):
    t, k = a.shape
    n = w.shape[1]
    sblk = seq // tm
    in_specs = [
        pl.BlockSpec((tm, k), lambda m, j: (m, 0)),
        pl.BlockSpec((k, tn), lambda m, j: (0, j)),
        pl.BlockSpec((1, tn), lambda m, j: (0, j)),
    ]
    args = [a, w, colscale]
    if rope is not None:
        in_specs += [pl.BlockSpec((tm, LANES), lambda m, j: (m % sblk, 0))] * 3
        args += list(rope)
    if heads:
        out_shape = jax.ShapeDtypeStruct((t // seq, n // LANES, seq, LANES), out_dtype)
        out_spec = pl.BlockSpec((1, tn // LANES, tm, LANES), lambda m, j: (m // sblk, j, m % sblk, 0))
    else:
        out_shape = jax.ShapeDtypeStruct((t, n), out_dtype)
        out_spec = pl.BlockSpec((tm, tn), lambda m, j: (m, j))
    return pl.pallas_call(
        functools.partial(_proj_kernel, rotary=rope is not None, sigmoid=sigmoid, heads=heads),
        out_shape=out_shape,
        grid=(t // tm, n // tn),
        in_specs=in_specs,
        out_specs=out_spec,
        compiler_params=_params(("parallel", "arbitrary")),
        name=name,
    )(*args)


def _rope_tables(seq):
    half = ROPE_DIM // 2
    inv_freq = 1.0 / (ROPE_THETA ** (jnp.arange(half, dtype=F32) / half))
    ang = jnp.arange(seq, dtype=F32)[:, None] * inv_freq[None, :]
    cos, sin = jnp.cos(ang), jnp.sin(ang)
    ones = jnp.ones((seq, LANES - ROPE_DIM), F32)
    zeros = jnp.zeros((seq, LANES - ROPE_DIM), F32)
    zh = jnp.zeros((seq, half), F32)
    cos_t = jnp.concatenate([cos, cos, ones], axis=1)
    sin_a = jnp.concatenate([zh, sin, zeros], axis=1)
    sin_b = jnp.concatenate([-sin, zh, zeros], axis=1)
    return cos_t, sin_a, sin_b


def _forget_kernel(fl_ref, b_ref, c_ref):
    z = fl_ref[...] + b_ref[...]
    logf = jnp.minimum(z, 0.0) - jnp.log1p(jnp.exp(-jnp.abs(z)))
    row = lax.broadcasted_iota(jnp.int32, (LANES, LANES), 0)
    col = lax.broadcasted_iota(jnp.int32, (LANES, LANES), 1)
    tri = (row <= col).astype(BF16)
    carry = jnp.zeros((logf.shape[0], 1), F32)
    for j in range(logf.shape[1] // LANES):
        xb = logf[:, j * LANES:(j + 1) * LANES]
        hi = xb.astype(BF16)
        r1 = xb - hi.astype(F32)
        mid = r1.astype(BF16)
        lo = (r1 - mid.astype(F32)).astype(BF16)
        cs = (jnp.dot(hi, tri, preferred_element_type=F32) + jnp.dot(mid, tri, preferred_element_type=F32)
              + jnp.dot(lo, tri, preferred_element_type=F32)) + carry
        c_ref[:, j * LANES:(j + 1) * LANES] = cs * LOG2E
        carry = cs[:, LANES - 1:LANES]


def _forget_cumsum(fl_t, b_forget):
    b, h, s = fl_t.shape
    return pl.pallas_call(
        _forget_kernel,
        out_shape=jax.ShapeDtypeStruct((b, h, s), F32),
        grid=(b,),
        in_specs=[pl.BlockSpec((None, h, s), lambda i: (i, 0, 0)), pl.BlockSpec((h, 1), lambda i: (0, 0))],
        out_specs=pl.BlockSpec((None, h, s), lambda i: (i, 0, 0)),
        compiler_params=_params(("parallel",)),
        name="forget_cumsum",
    )(fl_t, b_forget.reshape(h, 1).astype(F32))


def _attn_kernel(*refs, moba, tq, tk):
    if moba:
        q_ref, k_ref, v_ref, o_ref, km_ref, kaug_ref = refs
    else:
        q_ref, k_ref, v_ref, c_ref, o_ref = refs
    nkb = q_ref.shape[0] // tk
    nt = (((1,), (1,)), ((), ()))
    row = lax.broadcasted_iota(jnp.int32, (tq, tk), 0)
    col = lax.broadcasted_iota(jnp.int32, (tq, tk), 1)

    if moba:
        lane = lax.broadcasted_iota(jnp.int32, (tk, LANES), 1)
        km_ref[...] = jnp.zeros_like(km_ref)
        for n in range(nkb):
            kb = k_ref[n * tk:(n + 1) * tk, :]
            km_ref[n:n + 1, :] = jnp.mean(kb.astype(F32), axis=0, keepdims=True)
            kaug_ref[n * tk:(n + 1) * tk, :LANES] = kb
            kaug_ref[n * tk:(n + 1) * tk, LANES:] = (lane == n).astype(BF16)
        kmean = km_ref[...].astype(BF16)
        blk = lax.broadcasted_iota(jnp.int32, (SUBLANES, tk), 0)

    def keys(n):
        return (kaug_ref if moba else k_ref)[n * tk:(n + 1) * tk, :]

    for nb in range(nkb):
        if moba and nb > 0:
            g = lax.dot_general(kmean, q_ref[nb * tk:(nb + 1) * tk, :], nt, preferred_element_type=F32)[:SUBLANES]
            past = blk < nb
            pen = jnp.zeros((SUBLANES, tk), F32)
            for n in range(nb):
                g_n = g[n:n + 1, :]
                beats = ((g > g_n) | ((g == g_n) & (blk < n))) & past
                rank = jnp.sum(beats.astype(F32), axis=0, keepdims=True)
                pen = jnp.where((blk == n) & (rank >= MOBA_TOPK), MASKED, pen)
            pen = jnp.concatenate([pen, jnp.zeros((LANES - SUBLANES, tk), F32)], axis=0)
            pen_cols = pen.T.astype(BF16)
        for h in range(tk // tq):
            r0 = nb * tk + h * tq
            q = q_ref[r0:r0 + tq, :]
            s = lax.dot_general(q, k_ref[nb * tk:(nb + 1) * tk, :], nt, preferred_element_type=F32)
            if not moba:
                s = s - c_ref[nb]
            s = jnp.where(row + h * tq >= col, s, NEG_INF)
            m = jnp.max(s, axis=1, keepdims=True)
            p = jnp.exp2(s - m)
            l = jnp.sum(p, axis=1, keepdims=True)
            acc = jnp.dot(p.astype(BF16), v_ref[nb * tk:(nb + 1) * tk, :], preferred_element_type=F32)
            if moba and nb > 0:
                q = jnp.concatenate([q, pen_cols[h * tq:(h + 1) * tq, :]], axis=1)
            for n in range(nb):
                s = lax.dot_general(q, keys(n), nt, preferred_element_type=F32)
                if not moba:
                    s = s - c_ref[n]
                m_new = jnp.maximum(m, jnp.max(s, axis=1, keepdims=True))
                alpha = jnp.exp2(m - m_new)
                p = jnp.exp2(s - m_new)
                l = alpha * l + jnp.sum(p, axis=1, keepdims=True)
                acc = alpha * acc + jnp.dot(p.astype(BF16), v_ref[n * tk:(n + 1) * tk, :],
                                            preferred_element_type=F32)
                m = m_new
            o_ref[r0:r0 + tq, :] = (acc / l).astype(o_ref.dtype)


def _attention(q_arr, q_off, k_arr, k_off, v_arr, v_off, c5, *, moba, name):
    b, _, s, d = q_arr.shape
    tk = MOBA_BLOCK
    tq = min(ATTN_Q_ROWS, tk)
    in_specs = [
        pl.BlockSpec((None, None, s, d), lambda bi, h: (bi, q_off + h, 0, 0)),
        pl.BlockSpec((None, None, s, d), lambda bi, h: (bi, k_off + h, 0, 0)),
        pl.BlockSpec((None, None, s, d), lambda bi, h: (bi, v_off + h, 0, 0)),
    ]
    args = [q_arr, k_arr, v_arr]
    scratch = []
    if moba:
        scratch = [pltpu.VMEM((LANES, d), F32), pltpu.VMEM((s, d + LANES), BF16)]
    else:
        in_specs.append(pl.BlockSpec((None, None, s // tk, 1, tk), lambda bi, h: (bi, h, 0, 0, 0)))
        args.append(c5)
    return pl.pallas_call(
        functools.partial(_attn_kernel, moba=moba, tq=tq, tk=tk),
        out_shape=jax.ShapeDtypeStruct((b * s, N_HEADS * d), BF16),
        grid=(b, N_HEADS),
        in_specs=in_specs,
        out_specs=pl.BlockSpec((s, d), lambda bi, h: (bi, h)),
        scratch_shapes=scratch,
        compiler_params=_params(("parallel", "parallel")),
        name=name,
    )(*args)


def _mix_kernel(of_ref, om_ref, g_ref, x_ref, wf_ref, wm_ref, wo_ref, gn_ref, wr_ref, br_ref,
                x2_ref, h2_ref, lg_ref):
    d = x_ref.shape[1]
    yf = jnp.dot(of_ref[...], wf_ref[...], preferred_element_type=F32)
    ym = jnp.dot(om_ref[...], wm_ref[...], preferred_element_type=F32)
    mixed = g_ref[:, :d].astype(F32) * yf + g_ref[:, d:].astype(F32) * ym
    x2 = x_ref[...] + jnp.dot(mixed.astype(BF16), wo_ref[...], preferred_element_type=F32)
    x2_ref[...] = x2
    ms = jnp.mean(x2 * x2, axis=-1, keepdims=True)
    h2 = x2 * lax.rsqrt(ms + RMS_EPS) * gn_ref[...]
    _store_token_tiles(h2_ref, h2)
    lg_ref[...] = jnp.dot(h2.astype(BF16), wr_ref[...], preferred_element_type=F32) + br_ref[...]


def _mix(o_fox, o_moba, gates, x, w_f, w_m, w_o, g_ffn, w_r, b_r, tm):
    t, d = x.shape
    w = o_fox.shape[1]
    njp = d // (2 * LANES)
    const = lambda i: (0, 0)
    resident = dict(pipeline_mode=pl.Buffered(1))
    return pl.pallas_call(
        _mix_kernel,
        out_shape=(jax.ShapeDtypeStruct((t, d), F32), jax.ShapeDtypeStruct((t * njp, LANES), U32),
                   jax.ShapeDtypeStruct((t, LANES), F32)),
        grid=(t // tm,),
        in_specs=[
            pl.BlockSpec((tm, w), lambda i: (i, 0)),
            pl.BlockSpec((tm, w), lambda i: (i, 0)),
            pl.BlockSpec((tm, 2 * d), lambda i: (i, 0)),
            pl.BlockSpec((tm, d), lambda i: (i, 0)),
            pl.BlockSpec((w, d), const, **resident),
            pl.BlockSpec((w, d), const, **resident),
            pl.BlockSpec((d, d), const, **resident),
            pl.BlockSpec((1, d), const),
            pl.BlockSpec((d, LANES), const),
            pl.BlockSpec((1, LANES), const),
        ],
        out_specs=(pl.BlockSpec((tm, d), lambda i: (i, 0)), pl.BlockSpec((tm * njp, LANES), lambda i: (i, 0)),
                   pl.BlockSpec((tm, LANES), lambda i: (i, 0))),
        compiler_params=_params(("parallel",)),
        name="mix_out_router",
    )(o_fox, o_moba, gates, x, w_f, w_m, w_o, g_ffn, w_r, b_r)


def _router_kernel(lg_ref, meta_ref, gate_ref, cnt_ref, carry_ref):
    @pl.when(pl.program_id(0) == 0)
    def _():
        carry_ref[...] = jnp.zeros_like(carry_ref)

    l = lg_ref[...]
    tr = l.shape[0]
    lane = lax.broadcasted_iota(jnp.int32, (tr, LANES), 1).astype(F32)
    vals, idxs = [], []
    for _ in range(TOP_K):
        m = jnp.max(l, axis=1, keepdims=True)
        ix = jnp.min(jnp.where(l == m, lane, float(LANES)), axis=1, keepdims=True)
        vals.append(m)
        idxs.append(ix)
        l = jnp.where(lane == ix, NEG_INF, l)
    ex = [jnp.exp(v - vals[0]) for v in vals]
    den = ex[0] + ex[1] + ex[2] + ex[3]
    onehot = jnp.zeros((tr, LANES), F32)
    for ix in idxs:
        onehot = onehot + (lane == ix).astype(F32)
    row = lax.broadcasted_iota(jnp.int32, (tr, tr), 0)
    col = lax.broadcasted_iota(jnp.int32, (tr, tr), 1)
    before = (col < row).astype(BF16)
    running = jnp.dot(before, onehot.astype(BF16), preferred_element_type=F32) + carry_ref[...]
    meta = jnp.zeros((tr, LANES), F32)
    gate = jnp.zeros((tr, LANES), F32)
    for k in range(TOP_K):
        pos = jnp.sum(jnp.where(lane == idxs[k], running, 0.0), axis=1, keepdims=True)
        meta = jnp.where(lane == float(k), idxs[k], meta)
        meta = jnp.where(lane == float(TOP_K + k), pos, meta)
        gate = jnp.where(lane == float(k), ex[k] / den, gate)
    meta_ref[...] = meta.astype(jnp.int32)
    gate_ref[...] = gate
    carry_ref[...] += jnp.sum(onehot, axis=0, keepdims=True)
    cnt_ref[...] = carry_ref[...]


def _router(logits, tr):
    t = logits.shape[0]
    return pl.pallas_call(
        _router_kernel,
        out_shape=(jax.ShapeDtypeStruct((t, LANES), jnp.int32), jax.ShapeDtypeStruct((t, LANES), F32),
                   jax.ShapeDtypeStruct((1, LANES), F32)),
        grid=(t // tr,),
        in_specs=[pl.BlockSpec((tr, LANES), lambda i: (i, 0))],
        out_specs=(pl.BlockSpec((tr, LANES), lambda i: (i, 0)), pl.BlockSpec((tr, LANES), lambda i: (i, 0)),
                   pl.BlockSpec((1, LANES), lambda i: (0, 0))),
        scratch_shapes=[pltpu.VMEM((1, LANES), F32)],
        compiler_params=_params(("arbitrary",)),
        name="router_topk",
    )(logits)


def _gather_kernel(nused_ref, tok_ref, tok_next_ref, h_hbm, o_ref, buf, sem, *, njp):
    r = pl.program_id(0)
    m = o_ref.shape[0]
    slot = r % 2

    def issue(tok, s):
        def body(i, c):
            _token_copy(h_hbm, tok[0, 0, 2 * i], buf.at[s], 2 * i, njp, sem.at[s]).start(priority=0)
            _token_copy(h_hbm, tok[0, 0, 2 * i + 1], buf.at[s], 2 * i + 1, njp, sem.at[s]).start(priority=1)
            return c

        lax.fori_loop(0, m // 2, body, 0, unroll=4)

    @pl.when(r == 0)
    def _():
        issue(tok_ref, 0)

    @pl.when(r + 1 < nused_ref[0])
    def _():
        issue(tok_next_ref, 1 - slot)

    @pl.when(r < nused_ref[0])
    def _():
        pltpu.make_async_copy(h_hbm.at[pl.ds(0, m * njp)], buf.at[slot, pl.ds(0, m * njp)], sem.at[slot]).wait()
        for j in range(njp):
            lo, hi = _load_token_cols(buf.at[slot], m, njp, j)
            o_ref[:, j * LANES:(j + 1) * LANES] = lo.astype(o_ref.dtype)
            o_ref[:, (njp + j) * LANES:(njp + j + 1) * LANES] = hi.astype(o_ref.dtype)

    @pl.when(r >= nused_ref[0])
    def _():
        o_ref[...] = jnp.zeros_like(o_ref)


def _gather_rows(h_tiles, row_tok, n_used, m, njp):
    n_blocks = row_tok.shape[0] // m
    tok3 = row_tok.reshape(n_blocks, 1, m)
    return pl.pallas_call(
        functools.partial(_gather_kernel, njp=njp),
        out_shape=jax.ShapeDtypeStruct((n_blocks * m, 2 * njp * LANES), BF16),
        grid_spec=pltpu.PrefetchScalarGridSpec(
            num_scalar_prefetch=1,
            grid=(n_blocks,),
            in_specs=[
                pl.BlockSpec((1, 1, m), lambda r, nu: (r, 0, 0), memory_space=pltpu.SMEM),
                pl.BlockSpec((1, 1, m), lambda r, nu: (jnp.minimum(r + 1, n_blocks - 1), 0, 0),
                             memory_space=pltpu.SMEM),
                pl.BlockSpec(memory_space=pl.ANY),
            ],
            out_specs=pl.BlockSpec((m, 2 * njp * LANES), lambda r, nu: (r, 0)),
            scratch_shapes=[pltpu.VMEM((2, m * _token_pitch(njp), LANES), U32), pltpu.SemaphoreType.DMA((2,))],
        ),
        compiler_params=_params(("arbitrary",), disable_bounds_checks=True),
        name="expert_row_gather",
    )(n_used, tok3, tok3, h_tiles)


def _group_start(be_ref, r):
    return jnp.logical_or(r == 0, be_ref[r] != be_ref[jnp.maximum(r - 1, 0)])


def _gmm1_kernel(be_ref, nne_ref, nused_ref, x_ref, w_hbm, perm_ref, bg_ref, bl_ref, a_ref,
                 stage, wg_s, wl_s, sem):
    j = pl.program_id(0)
    r = pl.program_id(1)
    pw = perm_ref.shape[0]
    tw = stage.shape[1]

    def fetch(jj, e):
        return pltpu.make_async_copy(w_hbm.at[e, :, pl.ds(pl.multiple_of(jj * tw, tw), tw)], stage, sem)

    @pl.when((j == 0) & (r == 0))
    def _():
        fetch(0, be_ref[0]).start()

    @pl.when(r < nused_ref[0])
    def _():
        @pl.when(_group_start(be_ref, r))
        def _():
            fetch(j, be_ref[r]).wait()
            for c in range(tw // pw):
                wb = stage[:, c * pw:(c + 1) * pw].astype(BF16)
                sp = jnp.dot(wb, perm_ref[...], preferred_element_type=F32).astype(BF16)
                wg_s[:, c * (pw // 2):(c + 1) * (pw // 2)] = sp[:, :pw // 2]
                wl_s[:, c * (pw // 2):(c + 1) * (pw // 2)] = sp[:, pw // 2:]
            nxt = nne_ref[be_ref[r]]

            @pl.when(nxt >= 0)
            def _():
                fetch(j, nxt).start()

            @pl.when((nxt < 0) & (j + 1 < pl.num_programs(0)))
            def _():
                fetch(j + 1, be_ref[0]).start()

        x = x_ref[...]
        ug = jnp.dot(x, wg_s[...], preferred_element_type=F32) + bg_ref[...]
        ul = jnp.dot(x, wl_s[...], preferred_element_type=F32) + bl_ref[...]
        glu = jnp.minimum(ug, SWIGLU_LIMIT)
        lin = jnp.clip(ul, -SWIGLU_LIMIT, SWIGLU_LIMIT)
        a_ref[...] = (glu * jax.nn.sigmoid(SWIGLU_ALPHA * glu) * (lin + 1.0)).astype(a_ref.dtype)

    @pl.when(r >= nused_ref[0])
    def _():
        a_ref[...] = jnp.zeros_like(a_ref)


def _deinterleave_perm(width):
    src = np.arange(width)
    dst = np.where(src % 2 == 0, src // 2, width // 2 + src // 2)
    p = np.zeros((width, width), np.float32)
    p[src, dst] = 1.0
    return jnp.asarray(p, BF16)


def _gmm1(xb, w1, b1g, b1l, block_e, next_expert, n_used, m, tn):
    rows, d = xb.shape
    f = w1.shape[2] // 2
    n_blocks = rows // m
    pw = 2 * LANES

    def rc(r, nu):
        return jnp.minimum(r, nu[0] - 1)

    return pl.pallas_call(
        _gmm1_kernel,
        out_shape=jax.ShapeDtypeStruct((rows, f), BF16),
        grid_spec=pltpu.PrefetchScalarGridSpec(
            num_scalar_prefetch=3,
            grid=(f // tn, n_blocks),
            in_specs=[
                pl.BlockSpec((m, d), lambda j, r, be, nne, nu: (rc(r, nu), 0)),
                pl.BlockSpec(memory_space=pl.ANY),
                pl.BlockSpec((pw, pw), lambda j, r, be, nne, nu: (0, 0)),
                pl.BlockSpec((None, 1, tn), lambda j, r, be, nne, nu: (be[rc(r, nu)], 0, j)),
                pl.BlockSpec((None, 1, tn), lambda j, r, be, nne, nu: (be[rc(r, nu)], 0, j)),
            ],
            out_specs=pl.BlockSpec((m, tn), lambda j, r, be, nne, nu: (r, j)),
            scratch_shapes=[pltpu.VMEM((d, 2 * tn), F32), pltpu.VMEM((d, tn), BF16), pltpu.VMEM((d, tn), BF16),
                            pltpu.SemaphoreType.DMA(())],
        ),
        compiler_params=_params(("arbitrary", "arbitrary")),
        name="expert_mlp1_swiglu",
    )(block_e, next_expert, n_used, xb, w1, _deinterleave_perm(pw), b1g, b1l)


def _gmm2_kernel(be_ref, nne_ref, nused_ref, a_ref, w_hbm, b_ref, y_ref, stage, w_s, sem):
    r = pl.program_id(0)

    def fetch(e):
        return pltpu.make_async_copy(w_hbm.at[e], stage, sem)

    @pl.when(r == 0)
    def _():
        fetch(be_ref[0]).start()

    @pl.when(r < nused_ref[0])
    def _():
        @pl.when(_group_start(be_ref, r))
        def _():
            fetch(be_ref[r]).wait()
            w_s[...] = stage[...].astype(BF16)
            nxt = nne_ref[be_ref[r]]

            @pl.when(nxt >= 0)
            def _():
                fetch(nxt).start()

        a = a_ref[...]
        n = a.shape[0]
        half = w_s.shape[1] // 2
        njp = half // LANES
        cw = min(MXU_COLS, half)
        for c in range(half // cw):
            lo_cols = slice(c * cw, (c + 1) * cw)
            hi_cols = slice(half + c * cw, half + (c + 1) * cw)
            y_lo = jnp.dot(a, w_s[:, lo_cols], preferred_element_type=F32) + b_ref[:, lo_cols]
            y_hi = jnp.dot(a, w_s[:, hi_cols], preferred_element_type=F32) + b_ref[:, hi_cols]
            for jj in range(cw // LANES):
                lanes = slice(jj * LANES, (jj + 1) * LANES)
                y_ref[pl.ds(c * (cw // LANES) + jj, n, stride=njp), :] = _pack_pairs(y_lo[:, lanes], y_hi[:, lanes])

    @pl.when(r >= nused_ref[0])
    def _():
        y_ref[...] = jnp.zeros_like(y_ref)


def _gmm2(a, w2, b2, block_e, next_expert, n_used, m):
    rows, f = a.shape
    d = w2.shape[2]
    njp = d // (2 * LANES)
    n_blocks = rows // m

    def rc(r, nu):
        return jnp.minimum(r, nu[0] - 1)

    return pl.pallas_call(
        _gmm2_kernel,
        out_shape=jax.ShapeDtypeStruct((rows * njp, LANES), U32),
        grid_spec=pltpu.PrefetchScalarGridSpec(
            num_scalar_prefetch=3,
            grid=(n_blocks,),
            in_specs=[
                pl.BlockSpec((m, f), lambda r, be, nne, nu: (rc(r, nu), 0)),
                pl.BlockSpec(memory_space=pl.ANY),
                pl.BlockSpec((None, 1, d), lambda r, be, nne, nu: (be[rc(r, nu)], 0, 0)),
            ],
            out_specs=pl.BlockSpec((m * njp, LANES), lambda r, be, nne, nu: (r, 0)),
            scratch_shapes=[pltpu.VMEM((f, d), F32), pltpu.VMEM((f, d), BF16), pltpu.SemaphoreType.DMA(())],
        ),
        compiler_params=_params(("arbitrary",)),
        name="expert_mlp2",
    )(block_e, next_expert, n_used, a, w2, b2)


def _combine_kernel(dest_ref, dest_next_ref, y_hbm, gate_ref, x2_ref, g_ref, o_ref, buf, sem, *, final_norm):
    i = pl.program_id(0)
    tc, d = x2_ref.shape
    njp = d // (2 * LANES)
    slot = i % 2

    def issue(dest, s):
        def body(t, c):
            for k in range(TOP_K):
                _token_copy(y_hbm, dest[0, 0, t * TOP_K + k], buf.at[s, k], t, njp, sem.at[s]).start(priority=k % 2)
            return c

        lax.fori_loop(0, tc, body, 0, unroll=2)

    @pl.when(i == 0)
    def _():
        issue(dest_ref, 0)

    @pl.when(i + 1 < pl.num_programs(0))
    def _():
        issue(dest_next_ref, 1 - slot)

    for k in range(TOP_K):
        pltpu.make_async_copy(y_hbm.at[pl.ds(0, tc * njp)], buf.at[slot, k, pl.ds(0, tc * njp)],
                              sem.at[slot]).wait()
    lo_cols, hi_cols = [], []
    for j in range(njp):
        acc_lo = x2_ref[:, j * LANES:(j + 1) * LANES]
        acc_hi = x2_ref[:, (njp + j) * LANES:(njp + j + 1) * LANES]
        for k in range(TOP_K):
            lo, hi = _load_token_cols(buf.at[slot, k], tc, njp, j)
            g = gate_ref[:, k:k + 1]
            acc_lo = acc_lo + g * lo
            acc_hi = acc_hi + g * hi
        lo_cols.append(acc_lo)
        hi_cols.append(acc_hi)
    acc = jnp.concatenate(lo_cols + hi_cols, axis=1)
    if final_norm:
        ms = jnp.mean(acc * acc, axis=-1, keepdims=True)
        acc = acc * lax.rsqrt(ms + RMS_EPS) * g_ref[...]
    o_ref[...] = acc


def _combine(y_tiles, dest, gates, x2, g_final, tc, final_norm):
    t, d = x2.shape
    njp = d // (2 * LANES)
    nt = t // tc
    dest3 = dest.reshape(nt, 1, tc * TOP_K)
    return pl.pallas_call(
        functools.partial(_combine_kernel, final_norm=final_norm),
        out_shape=jax.ShapeDtypeStruct((t, d), F32),
        grid=(nt,),
        in_specs=[
            pl.BlockSpec((1, 1, tc * TOP_K), lambda i: (i, 0, 0), memory_space=pltpu.SMEM),
            pl.BlockSpec((1, 1, tc * TOP_K), lambda i: (jnp.minimum(i + 1, nt - 1), 0, 0),
                         memory_space=pltpu.SMEM),
            pl.BlockSpec(memory_space=pl.ANY),
            pl.BlockSpec((tc, LANES), lambda i: (i, 0)),
            pl.BlockSpec((tc, d), lambda i: (i, 0)),
            pl.BlockSpec((1, d), lambda i: (0, 0)),
        ],
        out_specs=pl.BlockSpec((tc, d), lambda i: (i, 0)),
        scratch_shapes=[pltpu.VMEM((2, TOP_K, tc * _token_pitch(njp), LANES), U32),
                        pltpu.SemaphoreType.DMA((2,))],
        compiler_params=_params(("arbitrary",), disable_bounds_checks=True),
        name="moe_combine_final_norm",
    )(dest3, dest3, y_tiles, gates, x2, g_final.reshape(1, d))


def _layer(x, norm_mix_g, w_in, b_forget, w_branch_fox, w_branch_moba, w_out, norm_ffn_g,
           w_router, b_router, w_mlp1, b_mlp1, w_mlp2, b_mlp2):
    b, s, d = x.shape
    t = b * s
    width = N_HEADS * HEAD_DIM
    xt = x.reshape(t, d)
    q_scale = LOG2E * HEAD_DIM ** -0.5

    o = np.cumsum([0, width, width, width, N_HEADS, width, width, width, d, d])
    w_fox = w_in[:, o[0]:o[3]].astype(BF16)
    w_fl = jnp.pad(w_in[:, o[3]:o[4]], ((0, 0), (0, LANES - N_HEADS))).astype(BF16)
    w_mqk = w_in[:, o[4]:o[6]].astype(BF16)
    w_mv = w_in[:, o[6]:o[7]].astype(BF16)
    w_g = w_in[:, o[7]:o[9]].astype(BF16)
    colscale = jnp.concatenate([jnp.full((1, width), q_scale, F32), jnp.ones((1, 2 * width), F32)], axis=1)

    tm = min(2048, s)
    h = _rmsnorm(xt, norm_mix_g, tm=min(512, t))
    fox = _proj(h, w_fox, colscale, seq=s, tm=tm, tn=512, out_dtype=BF16, heads=True, name="proj_fox_qkv")
    mqk = _proj(h, w_mqk, colscale[:, :2 * width], seq=s, tm=tm, tn=512, out_dtype=BF16, heads=True,
                rope=_rope_tables(s), name="proj_moba_qk")
    mv = _proj(h, w_mv, colscale[:, width:2 * width], seq=s, tm=tm, tn=512, out_dtype=BF16, heads=True,
               name="proj_moba_v")
    gates = _proj(h, w_g, jnp.ones((1, 2 * d), F32), seq=s, tm=tm, tn=512, out_dtype=BF16, heads=False,
                  sigmoid=True, name="proj_gates")
    fl = _proj(h, w_fl, jnp.ones((1, LANES), F32), seq=s, tm=tm, tn=LANES, out_dtype=F32, heads=False,
               name="proj_forget")

    fl_t = fl[:, :N_HEADS].reshape(b, s, N_HEADS).transpose(0, 2, 1)
    c = _forget_cumsum(fl_t, b_forget)
    c5 = c.reshape(b, N_HEADS, s // MOBA_BLOCK, 1, MOBA_BLOCK)

    o_fox = _attention(fox, 0, fox, N_HEADS, fox, 2 * N_HEADS, c5, moba=False, name="fox_attention")
    o_moba = _attention(mqk, 0, mqk, N_HEADS, mv, 0, None, moba=True, name="moba_attention")

    w_r = jnp.pad(w_router, ((0, 0), (0, LANES - N_EXPERTS))).astype(BF16)
    b_r = jnp.concatenate([b_router.astype(F32), jnp.full((LANES - N_EXPERTS,), NEG_INF, F32)]).reshape(1, LANES)
    x2, h2, logits = _mix(o_fox, o_moba, gates, xt, w_branch_fox.astype(BF16), w_branch_moba.astype(BF16),
                          w_out.astype(BF16), norm_ffn_g.reshape(1, d), w_r, b_r, tm=min(256, t))

    meta, gate_w, cnt = _router(logits, tr=min(512, t))

    m = MOE_ROWS
    assign = t * TOP_K
    n_blocks = (assign + N_EXPERTS * (m - 1) + m - 1) // m
    counts = cnt[0, :N_EXPERTS].astype(jnp.int32)
    padded = (counts + m - 1) // m * m
    pend = jnp.cumsum(padded)
    pstart = pend - padded
    experts = jnp.arange(N_EXPERTS, dtype=jnp.int32)
    e_sel = meta[:, :TOP_K]
    dest = meta[:, TOP_K:2 * TOP_K] + jnp.sum(jnp.where(e_sel[..., None] == experts, pstart, 0), axis=-1)
    n_used = (pend[-1] // m).astype(jnp.int32).reshape(1)
    first_row = jnp.arange(n_blocks, dtype=jnp.int32) * m
    block_e = jnp.minimum(jnp.sum((pend[None, :] <= first_row[:, None]).astype(jnp.int32), axis=1), N_EXPERTS - 1)
    keys = jnp.sort((e_sel * t + jnp.arange(t, dtype=jnp.int32)[:, None]).reshape(-1))
    start = jnp.cumsum(counts) - counts
    of_block = block_e[:, None] == experts[None, :]
    blk_start = jnp.sum(jnp.where(of_block, start, 0), axis=1)
    blk_count = jnp.sum(jnp.where(of_block, counts, 0), axis=1)
    blk_first = first_row - jnp.sum(jnp.where(of_block, pstart, 0), axis=1)
    in_expert = jnp.minimum(blk_first[:, None] + jnp.arange(m, dtype=jnp.int32)[None, :], blk_count[:, None] - 1)
    src = jnp.clip(blk_start[:, None] + in_expert, 0, assign - 1)
    row_tok = (keys[src.reshape(-1)] % t).astype(jnp.int32)

    xb = _gather_rows(h2, row_tok, n_used, m, d // (2 * LANES))
    f = w_mlp1.shape[2] // 2
    b1g = b_mlp1[:, 0::2].reshape(N_EXPERTS, 1, f)
    b1l = b_mlp1[:, 1::2].reshape(N_EXPERTS, 1, f)
    later = (experts[None, :] > experts[:, None]) & (counts[None, :] > 0)
    next_expert = jnp.min(jnp.where(later, experts[None, :], N_EXPERTS), axis=1)
    next_expert = jnp.where(next_expert == N_EXPERTS, -1, next_expert).astype(jnp.int32)
    a = _gmm1(xb, w_mlp1, b1g, b1l, block_e, next_expert, n_used, m, tn=min(1024, f))
    y = _gmm2(a, w_mlp2, b_mlp2.reshape(N_EXPERTS, 1, d), block_e, next_expert, n_used, m)
    return x2, y, dest, gate_w


def kernel(x, norm_mix_g, w_in, b_forget, w_branch_fox, w_branch_moba, w_out, norm_ffn_g, w_router, b_router,
           w_mlp1, b_mlp1, w_mlp2, b_mlp2, norm_final_g):
    depth = w_in.shape[0]
    b, s, d = x.shape
    for l in range(depth):
        x2, y, dest, gate_w = _layer(x, norm_mix_g[l], w_in[l], b_forget[l], w_branch_fox[l], w_branch_moba[l],
                                     w_out[l], norm_ffn_g[l], w_router[l], b_router[l], w_mlp1[l], b_mlp1[l],
                                     w_mlp2[l], b_mlp2[l])
        out = _combine(y, dest, gate_w, x2, norm_final_g, tc=min(128, b * s), final_norm=l == depth - 1)
        x = out.reshape(b, s, d)
    return x
```

```python
import functools
import math

import jax
import jax.numpy as jnp
import numpy as np
from jax import lax
from jax.experimental import pallas as pl
from jax.experimental.pallas import tpu as pltpu

HEAD_DIM = 128
N_HEADS = 8
MOBA_BLOCK = 256
MOBA_TOPK = 3
ROPE_THETA = 500000.0
ROPE_DIM = HEAD_DIM // 4
N_EXPERTS = 32
TOP_K = 4
SWIGLU_ALPHA = 1.702
SWIGLU_LIMIT = 7.0
RMS_EPS = 1e-5

LANES = 128
SUBLANES = 8
MXU_COLS = 256
VMEM_LIMIT_BYTES = 56 * 1024 * 1024
MOE_ROWS = 512
ATTN_Q_ROWS = 256
ATTN_PAST_BLOCKS = 1

F32 = jnp.float32
BF16 = jnp.bfloat16
U32 = jnp.uint32
NEG_INF = float("-inf")
MASKED = -1e30
LOG2E = math.log2(math.e)


def _params(sem, **kw):
    return pltpu.CompilerParams(dimension_semantics=sem, vmem_limit_bytes=VMEM_LIMIT_BYTES, **kw)


def _pack_pairs(lo, hi):
    lo_b = lax.bitcast_convert_type(lo.astype(BF16).astype(F32), U32) >> 16
    hi_b = lax.bitcast_convert_type(hi.astype(BF16).astype(F32), U32) & jnp.uint32(0xFFFF0000)
    return hi_b | lo_b


def _unpack_pairs(w):
    lo = lax.bitcast_convert_type(w << 16, F32)
    hi = lax.bitcast_convert_type(w & jnp.uint32(0xFFFF0000), F32)
    return lo, hi


def _store_token_tiles(ref, val):
    n, d = val.shape
    njp = d // (2 * LANES)
    for j in range(njp):
        lo = val[:, j * LANES:(j + 1) * LANES]
        hi = val[:, (njp + j) * LANES:(njp + j + 1) * LANES]
        ref[pl.ds(j, n, stride=njp), :] = _pack_pairs(lo, hi)


def _token_pitch(njp):
    p = -(-njp // 4)
    return 4 * (p if p % 2 else p + 1)


def _token_copy(src_hbm, row, dst_vmem, slot, njp, sem):
    return pltpu.make_async_copy(src_hbm.at[pl.ds(pl.multiple_of(row * njp, njp), njp)],
                                 dst_vmem.at[pl.ds(slot * _token_pitch(njp), njp)], sem)


def _load_token_cols(buf, n, njp, j):
    return _unpack_pairs(buf[pl.ds(j, n, stride=_token_pitch(njp)), :])


def _rmsnorm_kernel(x_ref, g_ref, o_ref):
    x = x_ref[...]
    ms = jnp.mean(x * x, axis=-1, keepdims=True)
    o_ref[...] = (x * lax.rsqrt(ms + RMS_EPS) * g_ref[...]).astype(o_ref.dtype)


def _rmsnorm(x, g, tm):
    t, d = x.shape
    return pl.pallas_call(
        _rmsnorm_kernel,
        out_shape=jax.ShapeDtypeStruct((t, d), BF16),
        grid=(t // tm,),
        in_specs=[pl.BlockSpec((tm, d), lambda i: (i, 0)), pl.BlockSpec((1, d), lambda i: (0, 0))],
        out_specs=pl.BlockSpec((tm, d), lambda i: (i, 0)),
        compiler_params=_params(("parallel",)),
        name="rmsnorm_mix",
    )(x, g.reshape(1, d))


def _proj_kernel(a_ref, w_ref, cs_ref, *rest, rotary, sigmoid, heads):
    if rotary:
        cos_ref, sa_ref, sb_ref, o_ref = rest
    else:
        (o_ref,) = rest
    res = jnp.dot(a_ref[...], w_ref[...], preferred_element_type=F32) * cs_ref[...]
    if sigmoid:
        res = jax.nn.sigmoid(res)
    for j in range(res.shape[1] // LANES):
        r = res[:, j * LANES:(j + 1) * LANES]
        if rotary:
            half = ROPE_DIM // 2
            r = (r * cos_ref[...] + pltpu.roll(r, half, 1) * sa_ref[...]
                 + pltpu.roll(r, LANES - half, 1) * sb_ref[...])
        if heads:
            o_ref[0, j] = r.astype(o_ref.dtype)
        else:
            o_ref[:, j * LANES:(j + 1) * LANES] = r.astype(o_ref.dtype)


def _proj(a, w, colscale, *, seq, tm, tn, out_dtype, heads, sigmoid=False, rope=None, name):
    t, k = a.shape
    n = w.shape[1]
    sblk = seq // tm
    in_specs = [
        pl.BlockSpec((tm, k), lambda m, j: (m, 0)),
        pl.BlockSpec((k, tn), lambda m, j: (0, j)),
        pl.BlockSpec((1, tn), lambda m, j: (0, j)),
    ]
    args = [a, w, colscale]
    if rope is not None:
        in_specs += [pl.BlockSpec((tm, LANES), lambda m, j: (m % sblk, 0))] * 3
        args += list(rope)
    if heads:
        out_shape = jax.ShapeDtypeStruct((t // seq, n // LANES, seq, LANES), out_dtype)
        out_spec = pl.BlockSpec((1, tn // LANES, tm, LANES), lambda m, j: (m // sblk, j, m % sblk, 0))
    else:
        out_shape = jax.ShapeDtypeStruct((t, n), out_dtype)
        out_spec = pl.BlockSpec((tm, tn), lambda m, j: (m, j))
    return pl.pallas_call(
        functools.partial(_proj_kernel, rotary=rope is not None, sigmoid=sigmoid, heads=heads),
        out_shape=out_shape,
        grid=(t // tm, n // tn),
        in_specs=in_specs,
        out_specs=out_spec,
        compiler_params=_params(("parallel", "arbitrary")),
        name=name,
    )(*args)


def _rope_tables(seq):
    half = ROPE_DIM // 2
    inv_freq = 1.0 / (ROPE_THETA ** (jnp.arange(half, dtype=F32) / half))
    ang = jnp.arange(seq, dtype=F32)[:, None] * inv_freq[None, :]
    cos, sin = jnp.cos(ang), jnp.sin(ang)
    ones = jnp.ones((seq, LANES - ROPE_DIM), F32)
    zeros = jnp.zeros((seq, LANES - ROPE_DIM), F32)
    zh = jnp.zeros((seq, half), F32)
    cos_t = jnp.concatenate([cos, cos, ones], axis=1)
    sin_a = jnp.concatenate([zh, sin, zeros], axis=1)
    sin_b = jnp.concatenate([-sin, zh, zeros], axis=1)
    return cos_t, sin_a, sin_b


def _forget_kernel(fl_ref, b_ref, c_ref):
    z = fl_ref[...] + b_ref[...]
    logf = jnp.minimum(z, 0.0) - jnp.log1p(jnp.exp(-jnp.abs(z)))
    row = lax.broadcasted_iota(jnp.int32, (LANES, LANES), 0)
    col = lax.broadcasted_iota(jnp.int32, (LANES, LANES), 1)
    tri = (row <= col).astype(BF16)
    carry = jnp.zeros((logf.shape[0], 1), F32)
    for j in range(logf.shape[1] // LANES):
        xb = logf[:, j * LANES:(j + 1) * LANES]
        hi = xb.astype(BF16)
        r1 = xb - hi.astype(F32)
        mid = r1.astype(BF16)
        lo = (r1 - mid.astype(F32)).astype(BF16)
        cs = (jnp.dot(hi, tri, preferred_element_type=F32) + jnp.dot(mid, tri, preferred_element_type=F32)
              + jnp.dot(lo, tri, preferred_element_type=F32)) + carry
        c_ref[:, j * LANES:(j + 1) * LANES] = cs * LOG2E
        carry = cs[:, LANES - 1:LANES]


def _forget_cumsum(fl_t, b_forget):
    b, h, s = fl_t.shape
    return pl.pallas_call(
        _forget_kernel,
        out_shape=jax.ShapeDtypeStruct((b, h, s), F32),
        grid=(b,),
        in_specs=[pl.BlockSpec((None, h, s), lambda i: (i, 0, 0)), pl.BlockSpec((h, 1), lambda i: (0, 0))],
        out_specs=pl.BlockSpec((None, h, s), lambda i: (i, 0, 0)),
        compiler_params=_params(("parallel",)),
        name="forget_cumsum",
    )(fl_t, b_forget.reshape(h, 1).astype(F32))


def _attn_kernel(*refs, moba, tq, tk):
    if moba:
        q_ref, k_ref, v_ref, o_ref, vaug_ref, km_ref, kaug_ref, qaug_ref = refs
    else:
        q_ref, k_ref, v_ref, c_ref, o_ref, vaug_ref = refs
    d = q_ref.shape[1]
    nkb = q_ref.shape[0] // tk
    nt = (((1,), (1,)), ((), ()))
    row = lax.broadcasted_iota(jnp.int32, (tq, tk), 0)
    col = lax.broadcasted_iota(jnp.int32, (tq, tk), 1)

    vaug_ref[:, :d] = v_ref[...]
    vaug_ref[:, d:] = jnp.ones((v_ref.shape[0], LANES), BF16)
    if moba:
        lane = lax.broadcasted_iota(jnp.int32, (tk, LANES), 1)
        km_ref[...] = jnp.zeros_like(km_ref)
        for n in range(nkb):
            kb = k_ref[n * tk:(n + 1) * tk, :]
            km_ref[n:n + 1, :] = jnp.mean(kb.astype(F32), axis=0, keepdims=True)
            kaug_ref[n * tk:(n + 1) * tk, :d] = kb
            kaug_ref[n * tk:(n + 1) * tk, d:] = (lane == n).astype(BF16)
        s_len = q_ref.shape[0]
        g = lax.dot_general(km_ref[...].astype(BF16), q_ref[...], nt, preferred_element_type=F32)[:SUBLANES]
        blk = lax.broadcasted_iota(jnp.int32, (SUBLANES, s_len), 0)
        past = blk * tk + (tk - 1) < lax.broadcasted_iota(jnp.int32, (SUBLANES, s_len), 1)
        pen = jnp.zeros((SUBLANES, s_len), F32)
        for n in range(nkb - 1):
            g_n = g[n:n + 1, :]
            beats = ((g > g_n) | ((g == g_n) & (blk < n))) & past
            rank = jnp.sum(beats.astype(F32), axis=0, keepdims=True)
            pen = jnp.where((blk == n) & past & (rank >= MOBA_TOPK), MASKED, pen)
        pen = jnp.concatenate([pen, jnp.zeros((LANES - SUBLANES, s_len), F32)], axis=0)
        qaug_ref[:, :d] = q_ref[...]
        qaug_ref[:, d:] = pen.T.astype(BF16)

    def logits(q, n0, n1):
        s = lax.dot_general(q, (kaug_ref if moba else k_ref)[n0 * tk:n1 * tk, :], nt,
                            preferred_element_type=F32)
        return s if moba else s - c_ref[:, n0 * tk:n1 * tk]

    for nb in range(nkb):
        for h in range(tk // tq):
            r0 = nb * tk + h * tq
            q = (qaug_ref if moba else q_ref)[r0:r0 + tq, :]
            s = jnp.where(row + h * tq >= col, logits(q, nb, nb + 1), NEG_INF)
            m = jnp.max(s, axis=1, keepdims=True)
            acc = jnp.dot(jnp.exp2(s - m).astype(BF16), vaug_ref[nb * tk:(nb + 1) * tk, :],
                          preferred_element_type=F32)
            for n0 in range(0, nb, ATTN_PAST_BLOCKS):
                n1 = min(n0 + ATTN_PAST_BLOCKS, nb)
                s = logits(q, n0, n1)
                m_new = jnp.maximum(m, jnp.max(s, axis=1, keepdims=True))
                acc = jnp.exp2(m - m_new) * acc + jnp.dot(jnp.exp2(s - m_new).astype(BF16),
                                                          vaug_ref[n0 * tk:n1 * tk, :],
                                                          preferred_element_type=F32)
                m = m_new
            o_ref[r0:r0 + tq, :] = (acc[:, :d] / acc[:, d:]).astype(o_ref.dtype)


def _attention(q_arr, q_off, k_arr, k_off, v_arr, v_off, c_rows, *, moba, name):
    b, _, s, d = q_arr.shape
    tk = MOBA_BLOCK
    tq = min(ATTN_Q_ROWS, tk)
    assert s % tk == 0 and s // tk <= SUBLANES
    in_specs = [
        pl.BlockSpec((None, None, s, d), lambda bi, h: (bi, q_off + h, 0, 0)),
        pl.BlockSpec((None, None, s, d), lambda bi, h: (bi, k_off + h, 0, 0)),
        pl.BlockSpec((None, None, s, d), lambda bi, h: (bi, v_off + h, 0, 0)),
    ]
    args = [q_arr, k_arr, v_arr]
    scratch = [pltpu.VMEM((s, d + LANES), BF16)]
    if moba:
        scratch += [pltpu.VMEM((LANES, d), F32), pltpu.VMEM((s, d + LANES), BF16), pltpu.VMEM((s, d + LANES), BF16)]
    else:
        in_specs.append(pl.BlockSpec((None, None, 1, s), lambda bi, h: (bi, h, 0, 0)))
        args.append(c_rows)
    return pl.pallas_call(
        functools.partial(_attn_kernel, moba=moba, tq=tq, tk=tk),
        out_shape=jax.ShapeDtypeStruct((b * s, N_HEADS * d), BF16),
        grid=(b, N_HEADS),
        in_specs=in_specs,
        out_specs=pl.BlockSpec((s, d), lambda bi, h: (bi, h)),
        scratch_shapes=scratch,
        compiler_params=_params(("parallel", "parallel")),
        name=name,
    )(*args)


def _mix_kernel(of_ref, om_ref, g_ref, x_ref, wf_ref, wm_ref, wo_ref, gn_ref, wr_ref, br_ref,
                x2_ref, h2_ref, lg_ref):
    d = x_ref.shape[1]
    yf = jnp.dot(of_ref[...], wf_ref[...], preferred_element_type=F32)
    ym = jnp.dot(om_ref[...], wm_ref[...], preferred_element_type=F32)
    mixed = g_ref[:, :d].astype(F32) * yf + g_ref[:, d:].astype(F32) * ym
    x2 = x_ref[...] + jnp.dot(mixed.astype(BF16), wo_ref[...], preferred_element_type=F32)
    x2_ref[...] = x2
    ms = jnp.mean(x2 * x2, axis=-1, keepdims=True)
    h2 = x2 * lax.rsqrt(ms + RMS_EPS) * gn_ref[...]
    _store_token_tiles(h2_ref, h2)
    lg_ref[...] = jnp.dot(h2.astype(BF16), wr_ref[...], preferred_element_type=F32) + br_ref[...]


def _mix(o_fox, o_moba, gates, x, w_f, w_m, w_o, g_ffn, w_r, b_r, tm):
    t, d = x.shape
    w = o_fox.shape[1]
    njp = d // (2 * LANES)
    const = lambda i: (0, 0)
    resident = dict(pipeline_mode=pl.Buffered(1))
    return pl.pallas_call(
        _mix_kernel,
        out_shape=(jax.ShapeDtypeStruct((t, d), F32), jax.ShapeDtypeStruct((t * njp, LANES), U32),
                   jax.ShapeDtypeStruct((t, LANES), F32)),
        grid=(t // tm,),
        in_specs=[
            pl.BlockSpec((tm, w), lambda i: (i, 0)),
            pl.BlockSpec((tm, w), lambda i: (i, 0)),
            pl.BlockSpec((tm, 2 * d), lambda i: (i, 0)),
            pl.BlockSpec((tm, d), lambda i: (i, 0)),
            pl.BlockSpec((w, d), const, **resident),
            pl.BlockSpec((w, d), const, **resident),
            pl.BlockSpec((d, d), const, **resident),
            pl.BlockSpec((1, d), const),
            pl.BlockSpec((d, LANES), const),
            pl.BlockSpec((1, LANES), const),
        ],
        out_specs=(pl.BlockSpec((tm, d), lambda i: (i, 0)), pl.BlockSpec((tm * njp, LANES), lambda i: (i, 0)),
                   pl.BlockSpec((tm, LANES), lambda i: (i, 0))),
        compiler_params=_params(("parallel",)),
        name="mix_out_router",
    )(o_fox, o_moba, gates, x, w_f, w_m, w_o, g_ffn, w_r, b_r)


def _router_kernel(lg_ref, meta_ref, gate_ref, cnt_ref, carry_ref):
    @pl.when(pl.program_id(0) == 0)
    def _():
        carry_ref[...] = jnp.zeros_like(carry_ref)

    l = lg_ref[...]
    tr = l.shape[0]
    lane = lax.broadcasted_iota(jnp.int32, (tr, LANES), 1).astype(F32)
    vals, idxs = [], []
    for _ in range(TOP_K):
        m = jnp.max(l, axis=1, keepdims=True)
        ix = jnp.min(jnp.where(l == m, lane, float(LANES)), axis=1, keepdims=True)
        vals.append(m)
        idxs.append(ix)
        l = jnp.where(lane == ix, NEG_INF, l)
    ex = [jnp.exp(v - vals[0]) for v in vals]
    den = ex[0] + ex[1] + ex[2] + ex[3]
    onehot = jnp.zeros((tr, LANES), F32)
    for ix in idxs:
        onehot = onehot + (lane == ix).astype(F32)
    row = lax.broadcasted_iota(jnp.int32, (tr, tr), 0)
    col = lax.broadcasted_iota(jnp.int32, (tr, tr), 1)
    before = (col < row).astype(BF16)
    running = jnp.dot(before, onehot.astype(BF16), preferred_element_type=F32) + carry_ref[...]
    meta = jnp.zeros((tr, LANES), F32)
    gate = jnp.zeros((tr, LANES), F32)
    for k in range(TOP_K):
        pos = jnp.sum(jnp.where(lane == idxs[k], running, 0.0), axis=1, keepdims=True)
        meta = jnp.where(lane == float(k), idxs[k], meta)
        meta = jnp.where(lane == float(TOP_K + k), pos, meta)
        gate = jnp.where(lane == float(k), ex[k] / den, gate)
    meta_ref[...] = meta.astype(jnp.int32)
    gate_ref[...] = gate
    carry_ref[...] += jnp.sum(onehot, axis=0, keepdims=True)
    cnt_ref[...] = carry_ref[...]


def _router(logits, tr):
    t = logits.shape[0]
    return pl.pallas_call(
        _router_kernel,
        out_shape=(jax.ShapeDtypeStruct((t, LANES), jnp.int32), jax.ShapeDtypeStruct((t, LANES), F32),
                   jax.ShapeDtypeStruct((1, LANES), F32)),
        grid=(t // tr,),
        in_specs=[pl.BlockSpec((tr, LANES), lambda i: (i, 0))],
        out_specs=(pl.BlockSpec((tr, LANES), lambda i: (i, 0)), pl.BlockSpec((tr, LANES), lambda i: (i, 0)),
                   pl.BlockSpec((1, LANES), lambda i: (0, 0))),
        scratch_shapes=[pltpu.VMEM((1, LANES), F32)],
        compiler_params=_params(("arbitrary",)),
        name="router_topk",
    )(logits)


def _gather_kernel(nused_ref, tok_ref, tok_next_ref, h_hbm, o_ref, buf, sem, *, njp):
    r = pl.program_id(0)
    m = o_ref.shape[0]
    slot = r % 2

    def issue(tok, s):
        def body(i, c):
            _token_copy(h_hbm, tok[0, 0, 2 * i], buf.at[s], 2 * i, njp, sem.at[s]).start(priority=0)
            _token_copy(h_hbm, tok[0, 0, 2 * i + 1], buf.at[s], 2 * i + 1, njp, sem.at[s]).start(priority=1)
            return c

        lax.fori_loop(0, m // 2, body, 0, unroll=4)

    @pl.when(r == 0)
    def _():
        issue(tok_ref, 0)

    @pl.when(r + 1 < nused_ref[0])
    def _():
        issue(tok_next_ref, 1 - slot)

    @pl.when(r < nused_ref[0])
    def _():
        pltpu.make_async_copy(h_hbm.at[pl.ds(0, m * njp)], buf.at[slot, pl.ds(0, m * njp)], sem.at[slot]).wait()
        for j in range(njp):
            lo, hi = _load_token_cols(buf.at[slot], m, njp, j)
            o_ref[:, j * LANES:(j + 1) * LANES] = lo.astype(o_ref.dtype)
            o_ref[:, (njp + j) * LANES:(njp + j + 1) * LANES] = hi.astype(o_ref.dtype)

    @pl.when(r >= nused_ref[0])
    def _():
        o_ref[...] = jnp.zeros_like(o_ref)


def _gather_rows(h_tiles, row_tok, n_used, m, njp):
    n_blocks = row_tok.shape[0] // m
    tok3 = row_tok.reshape(n_blocks, 1, m)
    return pl.pallas_call(
        functools.partial(_gather_kernel, njp=njp),
        out_shape=jax.ShapeDtypeStruct((n_blocks * m, 2 * njp * LANES), BF16),
        grid_spec=pltpu.PrefetchScalarGridSpec(
            num_scalar_prefetch=1,
            grid=(n_blocks,),
            in_specs=[
                pl.BlockSpec((1, 1, m), lambda r, nu: (r, 0, 0), memory_space=pltpu.SMEM),
                pl.BlockSpec((1, 1, m), lambda r, nu: (jnp.minimum(r + 1, n_blocks - 1), 0, 0),
                             memory_space=pltpu.SMEM),
                pl.BlockSpec(memory_space=pl.ANY),
            ],
            out_specs=pl.BlockSpec((m, 2 * njp * LANES), lambda r, nu: (r, 0)),
            scratch_shapes=[pltpu.VMEM((2, m * _token_pitch(njp), LANES), U32), pltpu.SemaphoreType.DMA((2,))],
        ),
        compiler_params=_params(("arbitrary",), disable_bounds_checks=True),
        name="expert_row_gather",
    )(n_used, tok3, tok3, h_tiles)


def _group_start(be_ref, r):
    return jnp.logical_or(r == 0, be_ref[r] != be_ref[jnp.maximum(r - 1, 0)])


def _gmm1_kernel(be_ref, nne_ref, nused_ref, x_ref, w_hbm, perm_ref, bg_ref, bl_ref, a_ref,
                 stage, wg_s, wl_s, sem):
    j = pl.program_id(0)
    r = pl.program_id(1)
    pw = perm_ref.shape[0]
    tw = stage.shape[1]

    def fetch(jj, e):
        return pltpu.make_async_copy(w_hbm.at[e, :, pl.ds(pl.multiple_of(jj * tw, tw), tw)], stage, sem)

    @pl.when((j == 0) & (r == 0))
    def _():
        fetch(0, be_ref[0]).start()

    @pl.when(r < nused_ref[0])
    def _():
        @pl.when(_group_start(be_ref, r))
        def _():
            fetch(j, be_ref[r]).wait()
            for c in range(tw // pw):
                wb = stage[:, c * pw:(c + 1) * pw].astype(BF16)
                sp = jnp.dot(wb, perm_ref[...], preferred_element_type=F32).astype(BF16)
                wg_s[:, c * (pw // 2):(c + 1) * (pw // 2)] = sp[:, :pw // 2]
                wl_s[:, c * (pw // 2):(c + 1) * (pw // 2)] = sp[:, pw // 2:]
            nxt = nne_ref[be_ref[r]]

            @pl.when(nxt >= 0)
            def _():
                fetch(j, nxt).start()

            @pl.when((nxt < 0) & (j + 1 < pl.num_programs(0)))
            def _():
                fetch(j + 1, be_ref[0]).start()

        x = x_ref[...]
        ug = jnp.dot(x, wg_s[...], preferred_element_type=F32) + bg_ref[...]
        ul = jnp.dot(x, wl_s[...], preferred_element_type=F32) + bl_ref[...]
        glu = jnp.minimum(ug, SWIGLU_LIMIT)
        lin = jnp.clip(ul, -SWIGLU_LIMIT, SWIGLU_LIMIT)
        a_ref[...] = (glu * jax.nn.sigmoid(SWIGLU_ALPHA * glu) * (lin + 1.0)).astype(a_ref.dtype)

    @pl.when(r >= nused_ref[0])
    def _():
        a_ref[...] = jnp.zeros_like(a_ref)


def _deinterleave_perm(width):
    src = np.arange(width)
    dst = np.where(src % 2 == 0, src // 2, width // 2 + src // 2)
    p = np.zeros((width, width), np.float32)
    p[src, dst] = 1.0
    return jnp.asarray(p, BF16)


def _gmm1(xb, w1, b1g, b1l, block_e, next_expert, n_used, m, tn):
    rows, d = xb.shape
    f = w1.shape[2] // 2
    n_blocks = rows // m
    pw = 2 * LANES

    def rc(r, nu):
        return jnp.minimum(r, nu[0] - 1)

    return pl.pallas_call(
        _gmm1_kernel,
        out_shape=jax.ShapeDtypeStruct((rows, f), BF16),
        grid_spec=pltpu.PrefetchScalarGridSpec(
            num_scalar_prefetch=3,
            grid=(f // tn, n_blocks),
            in_specs=[
                pl.BlockSpec((m, d), lambda j, r, be, nne, nu: (rc(r, nu), 0)),
                pl.BlockSpec(memory_space=pl.ANY),
                pl.BlockSpec((pw, pw), lambda j, r, be, nne, nu: (0, 0)),
                pl.BlockSpec((None, 1, tn), lambda j, r, be, nne, nu: (be[rc(r, nu)], 0, j)),
                pl.BlockSpec((None, 1, tn), lambda j, r, be, nne, nu: (be[rc(r, nu)], 0, j)),
            ],
            out_specs=pl.BlockSpec((m, tn), lambda j, r, be, nne, nu: (r, j)),
            scratch_shapes=[pltpu.VMEM((d, 2 * tn), F32), pltpu.VMEM((d, tn), BF16), pltpu.VMEM((d, tn), BF16),
                            pltpu.SemaphoreType.DMA(())],
        ),
        compiler_params=_params(("arbitrary", "arbitrary")),
        name="expert_mlp1_swiglu",
    )(block_e, next_expert, n_used, xb, w1, _deinterleave_perm(pw), b1g, b1l)


def _gmm2_kernel(be_ref, nne_ref, nused_ref, a_ref, w_hbm, b_ref, y_ref, stage, w_s, sem):
    r = pl.program_id(0)

    def fetch(e):
        return pltpu.make_async_copy(w_hbm.at[e], stage, sem)

    @pl.when(r == 0)
    def _():
        fetch(be_ref[0]).start()

    @pl.when(r < nused_ref[0])
    def _():
        @pl.when(_group_start(be_ref, r))
        def _():
            fetch(be_ref[r]).wait()
            w_s[...] = stage[...].astype(BF16)
            nxt = nne_ref[be_ref[r]]

            @pl.when(nxt >= 0)
            def _():
                fetch(nxt).start()

        a = a_ref[...]
        n = a.shape[0]
        half = w_s.shape[1] // 2
        njp = half // LANES
        cw = min(MXU_COLS, half)
        for c in range(half // cw):
            lo_cols = slice(c * cw, (c + 1) * cw)
            hi_cols = slice(half + c * cw, half + (c + 1) * cw)
            y_lo = jnp.dot(a, w_s[:, lo_cols], preferred_element_type=F32) + b_ref[:, lo_cols]
            y_hi = jnp.dot(a, w_s[:, hi_cols], preferred_element_type=F32) + b_ref[:, hi_cols]
            for jj in range(cw // LANES):
                lanes = slice(jj * LANES, (jj + 1) * LANES)
                y_ref[pl.ds(c * (cw // LANES) + jj, n, stride=njp), :] = _pack_pairs(y_lo[:, lanes], y_hi[:, lanes])

    @pl.when(r >= nused_ref[0])
    def _():
        y_ref[...] = jnp.zeros_like(y_ref)


def _gmm2(a, w2, b2, block_e, next_expert, n_used, m):
    rows, f = a.shape
    d = w2.shape[2]
    njp = d // (2 * LANES)
    n_blocks = rows // m

    def rc(r, nu):
        return jnp.minimum(r, nu[0] - 1)

    return pl.pallas_call(
        _gmm2_kernel,
        out_shape=jax.ShapeDtypeStruct((rows * njp, LANES), U32),
        grid_spec=pltpu.PrefetchScalarGridSpec(
            num_scalar_prefetch=3,
            grid=(n_blocks,),
            in_specs=[
                pl.BlockSpec((m, f), lambda r, be, nne, nu: (rc(r, nu), 0)),
                pl.BlockSpec(memory_space=pl.ANY),
                pl.BlockSpec((None, 1, d), lambda r, be, nne, nu: (be[rc(r, nu)], 0, 0)),
            ],
            out_specs=pl.BlockSpec((m * njp, LANES), lambda r, be, nne, nu: (r, 0)),
            scratch_shapes=[pltpu.VMEM((f, d), F32), pltpu.VMEM((f, d), BF16), pltpu.SemaphoreType.DMA(())],
        ),
        compiler_params=_params(("arbitrary",)),
        name="expert_mlp2",
    )(block_e, next_expert, n_used, a, w2, b2)


def _combine_kernel(dest_ref, dest_next_ref, y_hbm, gate_ref, x2_ref, g_ref, o_ref, buf, sem, *, final_norm):
    i = pl.program_id(0)
    tc, d = x2_ref.shape
    njp = d // (2 * LANES)
    slot = i % 2

    def issue(dest, s):
        def body(t, c):
            for k in range(TOP_K):
                _token_copy(y_hbm, dest[0, 0, t * TOP_K + k], buf.at[s, k], t, njp, sem.at[s]).start(priority=k % 2)
            return c

        lax.fori_loop(0, tc, body, 0, unroll=2)

    @pl.when(i == 0)
    def _():
        issue(dest_ref, 0)

    @pl.when(i + 1 < pl.num_programs(0))
    def _():
        issue(dest_next_ref, 1 - slot)

    for k in range(TOP_K):
        pltpu.make_async_copy(y_hbm.at[pl.ds(0, tc * njp)], buf.at[slot, k, pl.ds(0, tc * njp)],
                              sem.at[slot]).wait()
    lo_cols, hi_cols = [], []
    for j in range(njp):
        acc_lo = x2_ref[:, j * LANES:(j + 1) * LANES]
        acc_hi = x2_ref[:, (njp + j) * LANES:(njp + j + 1) * LANES]
        for k in range(TOP_K):
            lo, hi = _load_token_cols(buf.at[slot, k], tc, njp, j)
            g = gate_ref[:, k:k + 1]
            acc_lo = acc_lo + g * lo
            acc_hi = acc_hi + g * hi
        lo_cols.append(acc_lo)
        hi_cols.append(acc_hi)
    acc = jnp.concatenate(lo_cols + hi_cols, axis=1)
    if final_norm:
        ms = jnp.mean(acc * acc, axis=-1, keepdims=True)
        acc = acc * lax.rsqrt(ms + RMS_EPS) * g_ref[...]
    o_ref[...] = acc


def _combine(y_tiles, dest, gates, x2, g_final, tc, final_norm):
    t, d = x2.shape
    njp = d // (2 * LANES)
    nt = t // tc
    dest3 = dest.reshape(nt, 1, tc * TOP_K)
    return pl.pallas_call(
        functools.partial(_combine_kernel, final_norm=final_norm),
        out_shape=jax.ShapeDtypeStruct((t, d), F32),
        grid=(nt,),
        in_specs=[
            pl.BlockSpec((1, 1, tc * TOP_K), lambda i: (i, 0, 0), memory_space=pltpu.SMEM),
            pl.BlockSpec((1, 1, tc * TOP_K), lambda i: (jnp.minimum(i + 1, nt - 1), 0, 0),
                         memory_space=pltpu.SMEM),
            pl.BlockSpec(memory_space=pl.ANY),
            pl.BlockSpec((tc, LANES), lambda i: (i, 0)),
            pl.BlockSpec((tc, d), lambda i: (i, 0)),
            pl.BlockSpec((1, d), lambda i: (0, 0)),
        ],
        out_specs=pl.BlockSpec((tc, d), lambda i: (i, 0)),
        scratch_shapes=[pltpu.VMEM((2, TOP_K, tc * _token_pitch(njp), LANES), U32),
                        pltpu.SemaphoreType.DMA((2,))],
        compiler_params=_params(("arbitrary",), disable_bounds_checks=True),
        name="moe_combine_final_norm",
    )(dest3, dest3, y_tiles, gates, x2, g_final.reshape(1, d))


def _layer(x, norm_mix_g, w_in, b_forget, w_branch_fox, w_branch_moba, w_out, norm_ffn_g,
           w_router, b_router, w_mlp1, b_mlp1, w_mlp2, b_mlp2):
    b, s, d = x.shape
    t = b * s
    width = N_HEADS * HEAD_DIM
    xt = x.reshape(t, d)
    q_scale = LOG2E * HEAD_DIM ** -0.5

    o = np.cumsum([0, width, width, width, N_HEADS, width, width, width, d, d])
    w_fox = w_in[:, o[0]:o[3]].astype(BF16)
    w_fl = jnp.pad(w_in[:, o[3]:o[4]], ((0, 0), (0, LANES - N_HEADS))).astype(BF16)
    w_mqk = w_in[:, o[4]:o[6]].astype(BF16)
    w_mv = w_in[:, o[6]:o[7]].astype(BF16)
    w_g = w_in[:, o[7]:o[9]].astype(BF16)
    colscale = jnp.concatenate([jnp.full((1, width), q_scale, F32), jnp.ones((1, 2 * width), F32)], axis=1)

    tm = min(2048, s)
    h = _rmsnorm(xt, norm_mix_g, tm=min(512, t))
    fox = _proj(h, w_fox, colscale, seq=s, tm=tm, tn=512, out_dtype=BF16, heads=True, name="proj_fox_qkv")
    mqk = _proj(h, w_mqk, colscale[:, :2 * width], seq=s, tm=tm, tn=512, out_dtype=BF16, heads=True,
                rope=_rope_tables(s), name="proj_moba_qk")
    mv = _proj(h, w_mv, colscale[:, width:2 * width], seq=s, tm=tm, tn=512, out_dtype=BF16, heads=True,
               name="proj_moba_v")
    gates = _proj(h, w_g, jnp.ones((1, 2 * d), F32), seq=s, tm=tm, tn=512, out_dtype=BF16, heads=False,
                  sigmoid=True, name="proj_gates")
    fl = _proj(h, w_fl, jnp.ones((1, LANES), F32), seq=s, tm=tm, tn=LANES, out_dtype=F32, heads=False,
               name="proj_forget")

    fl_t = fl[:, :N_HEADS].reshape(b, s, N_HEADS).transpose(0, 2, 1)
    c = _forget_cumsum(fl_t, b_forget)
    c_rows = c.reshape(b, N_HEADS, 1, s)

    o_fox = _attention(fox, 0, fox, N_HEADS, fox, 2 * N_HEADS, c_rows, moba=False, name="fox_attention")
    o_moba = _attention(mqk, 0, mqk, N_HEADS, mv, 0, None, moba=True, name="moba_attention")

    w_r = jnp.pad(w_router, ((0, 0), (0, LANES - N_EXPERTS))).astype(BF16)
    b_r = jnp.concatenate([b_router.astype(F32), jnp.full((LANES - N_EXPERTS,), NEG_INF, F32)]).reshape(1, LANES)
    x2, h2, logits = _mix(o_fox, o_moba, gates, xt, w_branch_fox.astype(BF16), w_branch_moba.astype(BF16),
                          w_out.astype(BF16), norm_ffn_g.reshape(1, d), w_r, b_r, tm=min(256, t))

    meta, gate_w, cnt = _router(logits, tr=min(512, t))

    m = MOE_ROWS
    assign = t * TOP_K
    n_blocks = (assign + N_EXPERTS * (m - 1) + m - 1) // m
    counts = cnt[0, :N_EXPERTS].astype(jnp.int32)
    padded = (counts + m - 1) // m * m
    pend = jnp.cumsum(padded)
    pstart = pend - padded
    experts = jnp.arange(N_EXPERTS, dtype=jnp.int32)
    e_sel = meta[:, :TOP_K]
    dest = meta[:, TOP_K:2 * TOP_K] + jnp.sum(jnp.where(e_sel[..., None] == experts, pstart, 0), axis=-1)
    n_used = (pend[-1] // m).astype(jnp.int32).reshape(1)
    first_row = jnp.arange(n_blocks, dtype=jnp.int32) * m
    block_e = jnp.minimum(jnp.sum((pend[None, :] <= first_row[:, None]).astype(jnp.int32), axis=1), N_EXPERTS - 1)
    keys = jnp.sort((e_sel * t + jnp.arange(t, dtype=jnp.int32)[:, None]).reshape(-1))
    start = jnp.cumsum(counts) - counts
    of_block = block_e[:, None] == experts[None, :]
    blk_start = jnp.sum(jnp.where(of_block, start, 0), axis=1)
    blk_count = jnp.sum(jnp.where(of_block, counts, 0), axis=1)
    blk_first = first_row - jnp.sum(jnp.where(of_block, pstart, 0), axis=1)
    in_expert = jnp.minimum(blk_first[:, None] + jnp.arange(m, dtype=jnp.int32)[None, :], blk_count[:, None] - 1)
    src = jnp.clip(blk_start[:, None] + in_expert, 0, assign - 1)
    row_tok = (keys[src.reshape(-1)] % t).astype(jnp.int32)

    xb = _gather_rows(h2, row_tok, n_used, m, d // (2 * LANES))
    f = w_mlp1.shape[2] // 2
    b1g = b_mlp1[:, 0::2].reshape(N_EXPERTS, 1, f)
    b1l = b_mlp1[:, 1::2].reshape(N_EXPERTS, 1, f)
    later = (experts[None, :] > experts[:, None]) & (counts[None, :] > 0)
    next_expert = jnp.min(jnp.where(later, experts[None, :], N_EXPERTS), axis=1)
    next_expert = jnp.where(next_expert == N_EXPERTS, -1, next_expert).astype(jnp.int32)
    a = _gmm1(xb, w_mlp1, b1g, b1l, block_e, next_expert, n_used, m, tn=min(1024, f))
    y = _gmm2(a, w_mlp2, b_mlp2.reshape(N_EXPERTS, 1, d), block_e, next_expert, n_used, m)
    return x2, y, dest, gate_w


def kernel(x, norm_mix_g, w_in, b_forget, w_branch_fox, w_branch_moba, w_out, norm_ffn_g, w_router, b_router,
           w_mlp1, b_mlp1, w_mlp2, b_mlp2, norm_final_g):
    depth = w_in.shape[0]
    b, s, d = x.shape
    for l in range(depth):
        x2, y, dest, gate_w = _layer(x, norm_mix_g[l], w_in[l], b_forget[l], w_branch_fox[l], w_branch_moba[l],
                                     w_out[l], norm_ffn_g[l], w_router[l], b_router[l], w_mlp1[l], b_mlp1[l],
                                     w_mlp2[l], b_mlp2[l])
        out = _combine(y, dest, gate_w, x2, norm_final_g, tc=min(128, b * s), final_norm=l == depth - 1)
        x = out.reshape(b, s, d)
    return x
```

```python
import functools
import math

import jax
import jax.numpy as jnp
import numpy as np
from jax import lax
from jax.experimental import pallas as pl
from jax.experimental.pallas import tpu as pltpu

HEAD_DIM = 128
N_HEADS = 8
MOBA_BLOCK = 256
MOBA_TOPK = 3
ROPE_THETA = 500000.0
ROPE_DIM = HEAD_DIM // 4
N_EXPERTS = 32
TOP_K = 4
SWIGLU_ALPHA = 1.702
SWIGLU_LIMIT = 7.0
RMS_EPS = 1e-5

LANES = 128
SUBLANES = 8
MXU_COLS = 256
VMEM_LIMIT_BYTES = 56 * 1024 * 1024
MOE_ROWS = 512
ATTN_Q_ROWS = 256
ATTN_PAST_BLOCKS = 1

F32 = jnp.float32
BF16 = jnp.bfloat16
U32 = jnp.uint32
NEG_INF = float("-inf")
MASKED = -1e30
LOG2E = math.log2(math.e)


def _params(sem, **kw):
    return pltpu.CompilerParams(dimension_semantics=sem, vmem_limit_bytes=VMEM_LIMIT_BYTES, **kw)


def _pack_pairs(lo, hi):
    lo_b = lax.bitcast_convert_type(lo.astype(BF16).astype(F32), U32) >> 16
    hi_b = lax.bitcast_convert_type(hi.astype(BF16).astype(F32), U32) & jnp.uint32(0xFFFF0000)
    return hi_b | lo_b


def _unpack_pairs(w):
    lo = lax.bitcast_convert_type(w << 16, F32)
    hi = lax.bitcast_convert_type(w & jnp.uint32(0xFFFF0000), F32)
    return lo, hi


def _store_token_tiles(ref, val):
    n, d = val.shape
    njp = d // (2 * LANES)
    for j in range(njp):
        lo = val[:, j * LANES:(j + 1) * LANES]
        hi = val[:, (njp + j) * LANES:(njp + j + 1) * LANES]
        ref[pl.ds(j, n, stride=njp), :] = _pack_pairs(lo, hi)


def _token_pitch(njp):
    p = -(-njp // 4)
    return 4 * (p if p % 2 else p + 1)


def _token_copy(src_hbm, row, dst_vmem, slot, njp, sem):
    return pltpu.make_async_copy(src_hbm.at[pl.ds(pl.multiple_of(row * njp, njp), njp)],
                                 dst_vmem.at[pl.ds(slot * _token_pitch(njp), njp)], sem)


def _load_token_cols(buf, n, njp, j):
    return _unpack_pairs(buf[pl.ds(j, n, stride=_token_pitch(njp)), :])


def _rmsnorm_kernel(x_ref, g_ref, o_ref):
    x = x_ref[...]
    ms = jnp.mean(x * x, axis=-1, keepdims=True)
    o_ref[...] = (x * lax.rsqrt(ms + RMS_EPS) * g_ref[...]).astype(o_ref.dtype)


def _rmsnorm(x, g, tm):
    t, d = x.shape
    return pl.pallas_call(
        _rmsnorm_kernel,
        out_shape=jax.ShapeDtypeStruct((t, d), BF16),
        grid=(t // tm,),
        in_specs=[pl.BlockSpec((tm, d), lambda i: (i, 0)), pl.BlockSpec((1, d), lambda i: (0, 0))],
        out_specs=pl.BlockSpec((tm, d), lambda i: (i, 0)),
        compiler_params=_params(("parallel",)),
        name="rmsnorm_mix",
    )(x, g.reshape(1, d))


def _proj_kernel(a_ref, w_ref, cs_ref, *rest, rotary, heads):
    if rotary:
        cos_ref, sa_ref, sb_ref, o_ref = rest
    else:
        (o_ref,) = rest
    res = jnp.dot(a_ref[...], w_ref[...], preferred_element_type=F32) * cs_ref[...]
    for j in range(res.shape[1] // LANES):
        r = res[:, j * LANES:(j + 1) * LANES]
        if rotary:
            half = ROPE_DIM // 2
            r = (r * cos_ref[...] + pltpu.roll(r, half, 1) * sa_ref[...]
                 + pltpu.roll(r, LANES - half, 1) * sb_ref[...])
        if heads:
            o_ref[0, j] = r.astype(o_ref.dtype)
        else:
            o_ref[:, j * LANES:(j + 1) * LANES] = r.astype(o_ref.dtype)


def _proj(a, w, colscale, *, seq, tm, tn, out_dtype, heads, rope=None, name):
    t, k = a.shape
    n = w.shape[1]
    sblk = seq // tm
    in_specs = [
        pl.BlockSpec((tm, k), lambda m, j: (m, 0)),
        pl.BlockSpec((k, tn), lambda m, j: (0, j)),
        pl.BlockSpec((1, tn), lambda m, j: (0, j)),
    ]
    args = [a, w, colscale]
    if rope is not None:
        in_specs += [pl.BlockSpec((tm, LANES), lambda m, j: (m % sblk, 0))] * 3
        args += list(rope)
    if heads:
        out_shape = jax.ShapeDtypeStruct((t // seq, n // LANES, seq, LANES), out_dtype)
        out_spec = pl.BlockSpec((1, tn // LANES, tm, LANES), lambda m, j: (m // sblk, j, m % sblk, 0))
    else:
        out_shape = jax.ShapeDtypeStruct((t, n), out_dtype)
        out_spec = pl.BlockSpec((tm, tn), lambda m, j: (m, j))
    return pl.pallas_call(
        functools.partial(_proj_kernel, rotary=rope is not None, heads=heads),
        out_shape=out_shape,
        grid=(t // tm, n // tn),
        in_specs=in_specs,
        out_specs=out_spec,
        compiler_params=_params(("parallel", "arbitrary")),
        name=name,
    )(*args)


def _rope_tables(seq):
    half = ROPE_DIM // 2
    inv_freq = 1.0 / (ROPE_THETA ** (jnp.arange(half, dtype=F32) / half))
    ang = jnp.arange(seq, dtype=F32)[:, None] * inv_freq[None, :]
    cos, sin = jnp.cos(ang), jnp.sin(ang)
    ones = jnp.ones((seq, LANES - ROPE_DIM), F32)
    zeros = jnp.zeros((seq, LANES - ROPE_DIM), F32)
    zh = jnp.zeros((seq, half), F32)
    cos_t = jnp.concatenate([cos, cos, ones], axis=1)
    sin_a = jnp.concatenate([zh, sin, zeros], axis=1)
    sin_b = jnp.concatenate([-sin, zh, zeros], axis=1)
    return cos_t, sin_a, sin_b


def _forget_kernel(fl_ref, b_ref, c_ref):
    z = fl_ref[...] + b_ref[...]
    logf = jnp.minimum(z, 0.0) - jnp.log1p(jnp.exp(-jnp.abs(z)))
    row = lax.broadcasted_iota(jnp.int32, (LANES, LANES), 0)
    col = lax.broadcasted_iota(jnp.int32, (LANES, LANES), 1)
    tri = (row <= col).astype(BF16)
    carry = jnp.zeros((logf.shape[0], 1), F32)
    for j in range(logf.shape[1] // LANES):
        xb = logf[:, j * LANES:(j + 1) * LANES]
        hi = xb.astype(BF16)
        r1 = xb - hi.astype(F32)
        mid = r1.astype(BF16)
        lo = (r1 - mid.astype(F32)).astype(BF16)
        cs = (jnp.dot(hi, tri, preferred_element_type=F32) + jnp.dot(mid, tri, preferred_element_type=F32)
              + jnp.dot(lo, tri, preferred_element_type=F32)) + carry
        c_ref[:, j * LANES:(j + 1) * LANES] = cs * LOG2E
        carry = cs[:, LANES - 1:LANES]


def _forget_cumsum(fl_t, b_forget):
    b, h, s = fl_t.shape
    return pl.pallas_call(
        _forget_kernel,
        out_shape=jax.ShapeDtypeStruct((b, h, s), F32),
        grid=(b,),
        in_specs=[pl.BlockSpec((None, h, s), lambda i: (i, 0, 0)), pl.BlockSpec((h, 1), lambda i: (0, 0))],
        out_specs=pl.BlockSpec((None, h, s), lambda i: (i, 0, 0)),
        compiler_params=_params(("parallel",)),
        name="forget_cumsum",
    )(fl_t, b_forget.reshape(h, 1).astype(F32))


def _attn_kernel(*refs, moba, tq, tk):
    if moba:
        q_ref, k_ref, v_ref, o_ref, vaug_ref, km_ref, kaug_ref, qaug_ref = refs
    else:
        q_ref, k_ref, v_ref, c_ref, o_ref, vaug_ref = refs
    d = q_ref.shape[1]
    nkb = q_ref.shape[0] // tk
    nt = (((1,), (1,)), ((), ()))
    row = lax.broadcasted_iota(jnp.int32, (tq, tk), 0)
    col = lax.broadcasted_iota(jnp.int32, (tq, tk), 1)

    vaug_ref[:, :d] = v_ref[...]
    vaug_ref[:, d:] = jnp.ones((v_ref.shape[0], LANES), BF16)
    if moba:
        lane = lax.broadcasted_iota(jnp.int32, (tk, LANES), 1)
        km_ref[...] = jnp.zeros_like(km_ref)
        for n in range(nkb):
            kb = k_ref[n * tk:(n + 1) * tk, :]
            km_ref[n:n + 1, :] = jnp.mean(kb.astype(F32), axis=0, keepdims=True)
            kaug_ref[n * tk:(n + 1) * tk, :d] = kb
            kaug_ref[n * tk:(n + 1) * tk, d:] = (lane == n).astype(BF16)
        s_len = q_ref.shape[0]
        g = lax.dot_general(km_ref[...].astype(BF16), q_ref[...], nt, preferred_element_type=F32)[:SUBLANES]
        blk = lax.broadcasted_iota(jnp.int32, (SUBLANES, s_len), 0)
        past = blk * tk + (tk - 1) < lax.broadcasted_iota(jnp.int32, (SUBLANES, s_len), 1)
        pen = jnp.zeros((SUBLANES, s_len), F32)
        for n in range(nkb - 1):
            g_n = g[n:n + 1, :]
            beats = ((g > g_n) | ((g == g_n) & (blk < n))) & past
            rank = jnp.sum(beats.astype(F32), axis=0, keepdims=True)
            pen = jnp.where((blk == n) & past & (rank >= MOBA_TOPK), MASKED, pen)
        pen = jnp.concatenate([pen, jnp.zeros((LANES - SUBLANES, s_len), F32)], axis=0)
        qaug_ref[:, :d] = q_ref[...]
        qaug_ref[:, d:] = pen.T.astype(BF16)

    def logits(q, n0, n1):
        s = lax.dot_general(q, (kaug_ref if moba else k_ref)[n0 * tk:n1 * tk, :], nt,
                            preferred_element_type=F32)
        return s if moba else s - c_ref[:, n0 * tk:n1 * tk]

    for nb in range(nkb):
        for h in range(tk // tq):
            r0 = nb * tk + h * tq
            q = (qaug_ref if moba else q_ref)[r0:r0 + tq, :]
            s = jnp.where(row + h * tq >= col, logits(q, nb, nb + 1), NEG_INF)
            m = jnp.max(s, axis=1, keepdims=True)
            acc = jnp.dot(jnp.exp2(s - m).astype(BF16), vaug_ref[nb * tk:(nb + 1) * tk, :],
                          preferred_element_type=F32)
            for n0 in range(0, nb, ATTN_PAST_BLOCKS):
                n1 = min(n0 + ATTN_PAST_BLOCKS, nb)
                s = logits(q, n0, n1)
                m_new = jnp.maximum(m, jnp.max(s, axis=1, keepdims=True))
                acc = jnp.exp2(m - m_new) * acc + jnp.dot(jnp.exp2(s - m_new).astype(BF16),
                                                          vaug_ref[n0 * tk:n1 * tk, :],
                                                          preferred_element_type=F32)
                m = m_new
            o_ref[r0:r0 + tq, :] = (acc[:, :d] / acc[:, d:]).astype(o_ref.dtype)


def _attention(q_arr, q_off, k_arr, k_off, v_arr, v_off, c_rows, *, moba, name):
    b, _, s, d = q_arr.shape
    tk = MOBA_BLOCK
    tq = min(ATTN_Q_ROWS, tk)
    assert s % tk == 0 and s // tk <= SUBLANES
    in_specs = [
        pl.BlockSpec((None, None, s, d), lambda bi, h: (bi, q_off + h, 0, 0)),
        pl.BlockSpec((None, None, s, d), lambda bi, h: (bi, k_off + h, 0, 0)),
        pl.BlockSpec((None, None, s, d), lambda bi, h: (bi, v_off + h, 0, 0)),
    ]
    args = [q_arr, k_arr, v_arr]
    scratch = [pltpu.VMEM((s, d + LANES), BF16)]
    if moba:
        scratch += [pltpu.VMEM((LANES, d), F32), pltpu.VMEM((s, d + LANES), BF16), pltpu.VMEM((s, d + LANES), BF16)]
    else:
        in_specs.append(pl.BlockSpec((None, None, 1, s), lambda bi, h: (bi, h, 0, 0)))
        args.append(c_rows)
    return pl.pallas_call(
        functools.partial(_attn_kernel, moba=moba, tq=tq, tk=tk),
        out_shape=jax.ShapeDtypeStruct((b * s, N_HEADS * d), BF16),
        grid=(b, N_HEADS),
        in_specs=in_specs,
        out_specs=pl.BlockSpec((s, d), lambda bi, h: (bi, h)),
        scratch_shapes=scratch,
        compiler_params=_params(("parallel", "parallel")),
        name=name,
    )(*args)


def _mix_kernel(of_ref, om_ref, g_ref, x_ref, wf_ref, wm_ref, wo_ref, gn_ref, wr_ref, br_ref,
                x2_ref, h2_ref, lg_ref):
    d = x_ref.shape[1]
    yf = jnp.dot(of_ref[...], wf_ref[...], preferred_element_type=F32)
    ym = jnp.dot(om_ref[...], wm_ref[...], preferred_element_type=F32)
    mixed = jax.nn.sigmoid(g_ref[:, :d].astype(F32)) * yf + jax.nn.sigmoid(g_ref[:, d:].astype(F32)) * ym
    x2 = x_ref[...] + jnp.dot(mixed.astype(BF16), wo_ref[...], preferred_element_type=F32)
    x2_ref[...] = x2
    ms = jnp.mean(x2 * x2, axis=-1, keepdims=True)
    h2 = x2 * lax.rsqrt(ms + RMS_EPS) * gn_ref[...]
    _store_token_tiles(h2_ref, h2)
    lg_ref[...] = jnp.dot(h2.astype(BF16), wr_ref[...], preferred_element_type=F32) + br_ref[...]


def _mix(o_fox, o_moba, gates, x, w_f, w_m, w_o, g_ffn, w_r, b_r, tm):
    t, d = x.shape
    w = o_fox.shape[1]
    njp = d // (2 * LANES)
    const = lambda i: (0, 0)
    resident = dict(pipeline_mode=pl.Buffered(1))
    return pl.pallas_call(
        _mix_kernel,
        out_shape=(jax.ShapeDtypeStruct((t, d), F32), jax.ShapeDtypeStruct((t * njp, LANES), U32),
                   jax.ShapeDtypeStruct((t, LANES), F32)),
        grid=(t // tm,),
        in_specs=[
            pl.BlockSpec((tm, w), lambda i: (i, 0)),
            pl.BlockSpec((tm, w), lambda i: (i, 0)),
            pl.BlockSpec((tm, 2 * d), lambda i: (i, 0)),
            pl.BlockSpec((tm, d), lambda i: (i, 0)),
            pl.BlockSpec((w, d), const, **resident),
            pl.BlockSpec((w, d), const, **resident),
            pl.BlockSpec((d, d), const, **resident),
            pl.BlockSpec((1, d), const),
            pl.BlockSpec((d, LANES), const),
            pl.BlockSpec((1, LANES), const),
        ],
        out_specs=(pl.BlockSpec((tm, d), lambda i: (i, 0)), pl.BlockSpec((tm * njp, LANES), lambda i: (i, 0)),
                   pl.BlockSpec((tm, LANES), lambda i: (i, 0))),
        compiler_params=_params(("parallel",)),
        name="mix_out_router",
    )(o_fox, o_moba, gates, x, w_f, w_m, w_o, g_ffn, w_r, b_r)


def _router_kernel(lg_ref, meta_ref, gate_ref, cnt_ref, carry_ref):
    @pl.when(pl.program_id(0) == 0)
    def _():
        carry_ref[...] = jnp.zeros_like(carry_ref)

    l = lg_ref[...]
    tr = l.shape[0]
    lane = lax.broadcasted_iota(jnp.int32, (tr, LANES), 1).astype(F32)
    vals, idxs = [], []
    for _ in range(TOP_K):
        m = jnp.max(l, axis=1, keepdims=True)
        ix = jnp.min(jnp.where(l == m, lane, float(LANES)), axis=1, keepdims=True)
        vals.append(m)
        idxs.append(ix)
        l = jnp.where(lane == ix, NEG_INF, l)
    ex = [jnp.exp(v - vals[0]) for v in vals]
    den = ex[0] + ex[1] + ex[2] + ex[3]
    onehot = jnp.zeros((tr, LANES), F32)
    for ix in idxs:
        onehot = onehot + (lane == ix).astype(F32)
    row = lax.broadcasted_iota(jnp.int32, (tr, tr), 0)
    col = lax.broadcasted_iota(jnp.int32, (tr, tr), 1)
    before = (col < row).astype(BF16)
    running = jnp.dot(before, onehot.astype(BF16), preferred_element_type=F32) + carry_ref[...]
    meta = jnp.zeros((tr, LANES), F32)
    gate = jnp.zeros((tr, LANES), F32)
    for k in range(TOP_K):
        pos = jnp.sum(jnp.where(lane == idxs[k], running, 0.0), axis=1, keepdims=True)
        meta = jnp.where(lane == float(k), idxs[k], meta)
        meta = jnp.where(lane == float(TOP_K + k), pos, meta)
        gate = jnp.where(lane == float(k), ex[k] / den, gate)
    meta_ref[...] = meta.astype(jnp.int32)
    gate_ref[...] = gate
    carry_ref[...] += jnp.sum(onehot, axis=0, keepdims=True)
    cnt_ref[...] = carry_ref[...]


def _router(logits, tr):
    t = logits.shape[0]
    return pl.pallas_call(
        _router_kernel,
        out_shape=(jax.ShapeDtypeStruct((t, LANES), jnp.int32), jax.ShapeDtypeStruct((t, LANES), F32),
                   jax.ShapeDtypeStruct((1, LANES), F32)),
        grid=(t // tr,),
        in_specs=[pl.BlockSpec((tr, LANES), lambda i: (i, 0))],
        out_specs=(pl.BlockSpec((tr, LANES), lambda i: (i, 0)), pl.BlockSpec((tr, LANES), lambda i: (i, 0)),
                   pl.BlockSpec((1, LANES), lambda i: (0, 0))),
        scratch_shapes=[pltpu.VMEM((1, LANES), F32)],
        compiler_params=_params(("arbitrary",)),
        name="router_topk",
    )(logits)


def _gather_kernel(nused_ref, tok_ref, tok_next_ref, h_hbm, o_ref, buf, sem, *, njp):
    r = pl.program_id(0)
    m = o_ref.shape[0]
    slot = r % 2

    def issue(tok, s):
        def body(i, c):
            _token_copy(h_hbm, tok[0, 0, 2 * i], buf.at[s], 2 * i, njp, sem.at[s]).start(priority=0)
            _token_copy(h_hbm, tok[0, 0, 2 * i + 1], buf.at[s], 2 * i + 1, njp, sem.at[s]).start(priority=1)
            return c

        lax.fori_loop(0, m // 2, body, 0, unroll=4)

    @pl.when(r == 0)
    def _():
        issue(tok_ref, 0)

    @pl.when(r + 1 < nused_ref[0])
    def _():
        issue(tok_next_ref, 1 - slot)

    @pl.when(r < nused_ref[0])
    def _():
        pltpu.make_async_copy(h_hbm.at[pl.ds(0, m * njp)], buf.at[slot, pl.ds(0, m * njp)], sem.at[slot]).wait()
        for j in range(njp):
            lo, hi = _load_token_cols(buf.at[slot], m, njp, j)
            o_ref[:, j * LANES:(j + 1) * LANES] = lo.astype(o_ref.dtype)
            o_ref[:, (njp + j) * LANES:(njp + j + 1) * LANES] = hi.astype(o_ref.dtype)

    @pl.when(r >= nused_ref[0])
    def _():
        o_ref[...] = jnp.zeros_like(o_ref)


def _gather_rows(h_tiles, row_tok, n_used, m, njp):
    n_blocks = row_tok.shape[0] // m
    tok3 = row_tok.reshape(n_blocks, 1, m)
    return pl.pallas_call(
        functools.partial(_gather_kernel, njp=njp),
        out_shape=jax.ShapeDtypeStruct((n_blocks * m, 2 * njp * LANES), BF16),
        grid_spec=pltpu.PrefetchScalarGridSpec(
            num_scalar_prefetch=1,
            grid=(n_blocks,),
            in_specs=[
                pl.BlockSpec((1, 1, m), lambda r, nu: (r, 0, 0), memory_space=pltpu.SMEM),
                pl.BlockSpec((1, 1, m), lambda r, nu: (jnp.minimum(r + 1, n_blocks - 1), 0, 0),
                             memory_space=pltpu.SMEM),
                pl.BlockSpec(memory_space=pl.ANY),
            ],
            out_specs=pl.BlockSpec((m, 2 * njp * LANES), lambda r, nu: (r, 0)),
            scratch_shapes=[pltpu.VMEM((2, m * _token_pitch(njp), LANES), U32), pltpu.SemaphoreType.DMA((2,))],
        ),
        compiler_params=_params(("arbitrary",), disable_bounds_checks=True),
        name="expert_row_gather",
    )(n_used, tok3, tok3, h_tiles)


def _group_start(be_ref, r):
    return jnp.logical_or(r == 0, be_ref[r] != be_ref[jnp.maximum(r - 1, 0)])


def _gmm1_kernel(be_ref, nne_ref, nused_ref, x_ref, w_hbm, perm_ref, bg_ref, bl_ref, a_ref,
                 stage, wg_s, wl_s, sem):
    j = pl.program_id(0)
    r = pl.program_id(1)
    pw = perm_ref.shape[0]
    tw = stage.shape[1]

    def fetch(jj, e):
        return pltpu.make_async_copy(w_hbm.at[e, :, pl.ds(pl.multiple_of(jj * tw, tw), tw)], stage, sem)

    @pl.when((j == 0) & (r == 0))
    def _():
        fetch(0, be_ref[0]).start()

    @pl.when(r < nused_ref[0])
    def _():
        @pl.when(_group_start(be_ref, r))
        def _():
            fetch(j, be_ref[r]).wait()
            for c in range(tw // pw):
                wb = stage[:, c * pw:(c + 1) * pw].astype(BF16)
                sp = jnp.dot(wb, perm_ref[...], preferred_element_type=F32).astype(BF16)
                wg_s[:, c * (pw // 2):(c + 1) * (pw // 2)] = sp[:, :pw // 2]
                wl_s[:, c * (pw // 2):(c + 1) * (pw // 2)] = sp[:, pw // 2:]
            nxt = nne_ref[be_ref[r]]

            @pl.when(nxt >= 0)
            def _():
                fetch(j, nxt).start()

            @pl.when((nxt < 0) & (j + 1 < pl.num_programs(0)))
            def _():
                fetch(j + 1, be_ref[0]).start()

        x = x_ref[...]
        ug = jnp.dot(x, wg_s[...], preferred_element_type=F32) + bg_ref[...]
        ul = jnp.dot(x, wl_s[...], preferred_element_type=F32) + bl_ref[...]
        glu = jnp.minimum(ug, SWIGLU_LIMIT)
        lin = jnp.clip(ul, -SWIGLU_LIMIT, SWIGLU_LIMIT)
        a_ref[...] = (glu * jax.nn.sigmoid(SWIGLU_ALPHA * glu) * (lin + 1.0)).astype(a_ref.dtype)

    @pl.when(r >= nused_ref[0])
    def _():
        a_ref[...] = jnp.zeros_like(a_ref)


def _deinterleave_perm(width):
    src = np.arange(width)
    dst = np.where(src % 2 == 0, src // 2, width // 2 + src // 2)
    p = np.zeros((width, width), np.float32)
    p[src, dst] = 1.0
    return jnp.asarray(p, BF16)


def _gmm1(xb, w1, b1g, b1l, block_e, next_expert, n_used, m, tn):
    rows, d = xb.shape
    f = w1.shape[2] // 2
    n_blocks = rows // m
    pw = 2 * LANES

    def rc(r, nu):
        return jnp.minimum(r, nu[0] - 1)

    return pl.pallas_call(
        _gmm1_kernel,
        out_shape=jax.ShapeDtypeStruct((rows, f), BF16),
        grid_spec=pltpu.PrefetchScalarGridSpec(
            num_scalar_prefetch=3,
            grid=(f // tn, n_blocks),
            in_specs=[
                pl.BlockSpec((m, d), lambda j, r, be, nne, nu: (rc(r, nu), 0)),
                pl.BlockSpec(memory_space=pl.ANY),
                pl.BlockSpec((pw, pw), lambda j, r, be, nne, nu: (0, 0)),
                pl.BlockSpec((None, 1, tn), lambda j, r, be, nne, nu: (be[rc(r, nu)], 0, j)),
                pl.BlockSpec((None, 1, tn), lambda j, r, be, nne, nu: (be[rc(r, nu)], 0, j)),
            ],
            out_specs=pl.BlockSpec((m, tn), lambda j, r, be, nne, nu: (r, j)),
            scratch_shapes=[pltpu.VMEM((d, 2 * tn), F32), pltpu.VMEM((d, tn), BF16), pltpu.VMEM((d, tn), BF16),
                            pltpu.SemaphoreType.DMA(())],
        ),
        compiler_params=_params(("arbitrary", "arbitrary")),
        name="expert_mlp1_swiglu",
    )(block_e, next_expert, n_used, xb, w1, _deinterleave_perm(pw), b1g, b1l)


def _gmm2_kernel(be_ref, nne_ref, nused_ref, a_ref, w_hbm, b_ref, y_ref, stage, w_s, sem):
    r = pl.program_id(0)

    def fetch(e):
        return pltpu.make_async_copy(w_hbm.at[e], stage, sem)

    @pl.when(r == 0)
    def _():
        fetch(be_ref[0]).start()

    @pl.when(r < nused_ref[0])
    def _():
        @pl.when(_group_start(be_ref, r))
        def _():
            fetch(be_ref[r]).wait()
            w_s[...] = stage[...].astype(BF16)
            nxt = nne_ref[be_ref[r]]

            @pl.when(nxt >= 0)
            def _():
                fetch(nxt).start()

        a = a_ref[...]
        n = a.shape[0]
        half = w_s.shape[1] // 2
        njp = half // LANES
        cw = min(MXU_COLS, half)
        for c in range(half // cw):
            lo_cols = slice(c * cw, (c + 1) * cw)
            hi_cols = slice(half + c * cw, half + (c + 1) * cw)
            y_lo = jnp.dot(a, w_s[:, lo_cols], preferred_element_type=F32) + b_ref[:, lo_cols]
            y_hi = jnp.dot(a, w_s[:, hi_cols], preferred_element_type=F32) + b_ref[:, hi_cols]
            for jj in range(cw // LANES):
                lanes = slice(jj * LANES, (jj + 1) * LANES)
                y_ref[pl.ds(c * (cw // LANES) + jj, n, stride=njp), :] = _pack_pairs(y_lo[:, lanes], y_hi[:, lanes])

    @pl.when(r >= nused_ref[0])
    def _():
        y_ref[...] = jnp.zeros_like(y_ref)


def _gmm2(a, w2, b2, block_e, next_expert, n_used, m):
    rows, f = a.shape
    d = w2.shape[2]
    njp = d // (2 * LANES)
    n_blocks = rows // m

    def rc(r, nu):
        return jnp.minimum(r, nu[0] - 1)

    return pl.pallas_call(
        _gmm2_kernel,
        out_shape=jax.ShapeDtypeStruct((rows * njp, LANES), U32),
        grid_spec=pltpu.PrefetchScalarGridSpec(
            num_scalar_prefetch=3,
            grid=(n_blocks,),
            in_specs=[
                pl.BlockSpec((m, f), lambda r, be, nne, nu: (rc(r, nu), 0)),
                pl.BlockSpec(memory_space=pl.ANY),
                pl.BlockSpec((None, 1, d), lambda r, be, nne, nu: (be[rc(r, nu)], 0, 0)),
            ],
            out_specs=pl.BlockSpec((m * njp, LANES), lambda r, be, nne, nu: (r, 0)),
            scratch_shapes=[pltpu.VMEM((f, d), F32), pltpu.VMEM((f, d), BF16), pltpu.SemaphoreType.DMA(())],
        ),
        compiler_params=_params(("arbitrary",)),
        name="expert_mlp2",
    )(block_e, next_expert, n_used, a, w2, b2)


def _combine_kernel(dest_ref, dest_next_ref, y_hbm, gate_ref, x2_ref, g_ref, o_ref, buf, sem, *, final_norm):
    i = pl.program_id(0)
    tc, d = x2_ref.shape
    njp = d // (2 * LANES)
    slot = i % 2

    def issue(dest, s):
        def body(t, c):
            for k in range(TOP_K):
                _token_copy(y_hbm, dest[0, 0, t * TOP_K + k], buf.at[s, k], t, njp, sem.at[s]).start(priority=k % 2)
            return c

        lax.fori_loop(0, tc, body, 0, unroll=2)

    @pl.when(i == 0)
    def _():
        issue(dest_ref, 0)

    @pl.when(i + 1 < pl.num_programs(0))
    def _():
        issue(dest_next_ref, 1 - slot)

    for k in range(TOP_K):
        pltpu.make_async_copy(y_hbm.at[pl.ds(0, tc * njp)], buf.at[slot, k, pl.ds(0, tc * njp)],
                              sem.at[slot]).wait()
    lo_cols, hi_cols = [], []
    for j in range(njp):
        acc_lo = x2_ref[:, j * LANES:(j + 1) * LANES]
        acc_hi = x2_ref[:, (njp + j) * LANES:(njp + j + 1) * LANES]
        for k in range(TOP_K):
            lo, hi = _load_token_cols(buf.at[slot, k], tc, njp, j)
            g = gate_ref[:, k:k + 1]
            acc_lo = acc_lo + g * lo
            acc_hi = acc_hi + g * hi
        lo_cols.append(acc_lo)
        hi_cols.append(acc_hi)
    acc = jnp.concatenate(lo_cols + hi_cols, axis=1)
    if final_norm:
        ms = jnp.mean(acc * acc, axis=-1, keepdims=True)
        acc = acc * lax.rsqrt(ms + RMS_EPS) * g_ref[...]
    o_ref[...] = acc


def _combine(y_tiles, dest, gates, x2, g_final, tc, final_norm):
    t, d = x2.shape
    njp = d // (2 * LANES)
    nt = t // tc
    dest3 = dest.reshape(nt, 1, tc * TOP_K)
    return pl.pallas_call(
        functools.partial(_combine_kernel, final_norm=final_norm),
        out_shape=jax.ShapeDtypeStruct((t, d), F32),
        grid=(nt,),
        in_specs=[
            pl.BlockSpec((1, 1, tc * TOP_K), lambda i: (i, 0, 0), memory_space=pltpu.SMEM),
            pl.BlockSpec((1, 1, tc * TOP_K), lambda i: (jnp.minimum(i + 1, nt - 1), 0, 0),
                         memory_space=pltpu.SMEM),
            pl.BlockSpec(memory_space=pl.ANY),
            pl.BlockSpec((tc, LANES), lambda i: (i, 0)),
            pl.BlockSpec((tc, d), lambda i: (i, 0)),
            pl.BlockSpec((1, d), lambda i: (0, 0)),
        ],
        out_specs=pl.BlockSpec((tc, d), lambda i: (i, 0)),
        scratch_shapes=[pltpu.VMEM((2, TOP_K, tc * _token_pitch(njp), LANES), U32),
                        pltpu.SemaphoreType.DMA((2,))],
        compiler_params=_params(("arbitrary",), disable_bounds_checks=True),
        name="moe_combine_final_norm",
    )(dest3, dest3, y_tiles, gates, x2, g_final.reshape(1, d))


def _layer(x, norm_mix_g, w_in, b_forget, w_branch_fox, w_branch_moba, w_out, norm_ffn_g,
           w_router, b_router, w_mlp1, b_mlp1, w_mlp2, b_mlp2):
    b, s, d = x.shape
    t = b * s
    width = N_HEADS * HEAD_DIM
    xt = x.reshape(t, d)
    q_scale = LOG2E * HEAD_DIM ** -0.5

    o = np.cumsum([0, width, width, width, N_HEADS, width, width, width, d, d])
    w_fox = w_in[:, o[0]:o[3]].astype(BF16)
    w_fl = jnp.pad(w_in[:, o[3]:o[4]], ((0, 0), (0, LANES - N_HEADS))).astype(BF16)
    w_mqk = w_in[:, o[4]:o[6]].astype(BF16)
    w_mv = w_in[:, o[6]:o[7]].astype(BF16)
    w_g = w_in[:, o[7]:o[9]].astype(BF16)
    colscale = jnp.concatenate([jnp.full((1, width), q_scale, F32), jnp.ones((1, 2 * width), F32)], axis=1)

    tm = min(2048, s)
    h = _rmsnorm(xt, norm_mix_g, tm=min(512, t))
    fox = _proj(h, w_fox, colscale, seq=s, tm=tm, tn=512, out_dtype=BF16, heads=True, name="proj_fox_qkv")
    mqk = _proj(h, w_mqk, colscale[:, :2 * width], seq=s, tm=tm, tn=512, out_dtype=BF16, heads=True,
                rope=_rope_tables(s), name="proj_moba_qk")
    mv = _proj(h, w_mv, colscale[:, width:2 * width], seq=s, tm=tm, tn=512, out_dtype=BF16, heads=True,
               name="proj_moba_v")
    gates = _proj(h, w_g, jnp.ones((1, 2 * d), F32), seq=s, tm=tm, tn=512, out_dtype=BF16, heads=False,
                  name="proj_gates")
    fl = _proj(h, w_fl, jnp.ones((1, LANES), F32), seq=s, tm=tm, tn=LANES, out_dtype=F32, heads=False,
               name="proj_forget")

    fl_t = fl[:, :N_HEADS].reshape(b, s, N_HEADS).transpose(0, 2, 1)
    c = _forget_cumsum(fl_t, b_forget)
    c_rows = c.reshape(b, N_HEADS, 1, s)

    o_fox = _attention(fox, 0, fox, N_HEADS, fox, 2 * N_HEADS, c_rows, moba=False, name="fox_attention")
    o_moba = _attention(mqk, 0, mqk, N_HEADS, mv, 0, None, moba=True, name="moba_attention")

    w_r = jnp.pad(w_router, ((0, 0), (0, LANES - N_EXPERTS))).astype(BF16)
    b_r = jnp.concatenate([b_router.astype(F32), jnp.full((LANES - N_EXPERTS,), NEG_INF, F32)]).reshape(1, LANES)
    x2, h2, logits = _mix(o_fox, o_moba, gates, xt, w_branch_fox.astype(BF16), w_branch_moba.astype(BF16),
                          w_out.astype(BF16), norm_ffn_g.reshape(1, d), w_r, b_r, tm=min(256, t))

    meta, gate_w, cnt = _router(logits, tr=min(512, t))

    m = MOE_ROWS
    assign = t * TOP_K
    n_blocks = (assign + N_EXPERTS * (m - 1) + m - 1) // m
    counts = cnt[0, :N_EXPERTS].astype(jnp.int32)
    padded = (counts + m - 1) // m * m
    pend = jnp.cumsum(padded)
    pstart = pend - padded
    experts = jnp.arange(N_EXPERTS, dtype=jnp.int32)
    e_sel = meta[:, :TOP_K]
    dest = meta[:, TOP_K:2 * TOP_K] + jnp.sum(jnp.where(e_sel[..., None] == experts, pstart, 0), axis=-1)
    n_used = (pend[-1] // m).astype(jnp.int32).reshape(1)
    first_row = jnp.arange(n_blocks, dtype=jnp.int32) * m
    block_e = jnp.minimum(jnp.sum((pend[None, :] <= first_row[:, None]).astype(jnp.int32), axis=1), N_EXPERTS - 1)
    keys = jnp.sort((e_sel * t + jnp.arange(t, dtype=jnp.int32)[:, None]).reshape(-1))
    start = jnp.cumsum(counts) - counts
    of_block = block_e[:, None] == experts[None, :]
    blk_start = jnp.sum(jnp.where(of_block, start, 0), axis=1)
    blk_count = jnp.sum(jnp.where(of_block, counts, 0), axis=1)
    blk_first = first_row - jnp.sum(jnp.where(of_block, pstart, 0), axis=1)
    in_expert = jnp.minimum(blk_first[:, None] + jnp.arange(m, dtype=jnp.int32)[None, :], blk_count[:, None] - 1)
    src = jnp.clip(blk_start[:, None] + in_expert, 0, assign - 1)
    row_tok = (keys[src.reshape(-1)] % t).astype(jnp.int32)

    xb = _gather_rows(h2, row_tok, n_used, m, d // (2 * LANES))
    f = w_mlp1.shape[2] // 2
    b1g = b_mlp1[:, 0::2].reshape(N_EXPERTS, 1, f)
    b1l = b_mlp1[:, 1::2].reshape(N_EXPERTS, 1, f)
    later = (experts[None, :] > experts[:, None]) & (counts[None, :] > 0)
    next_expert = jnp.min(jnp.where(later, experts[None, :], N_EXPERTS), axis=1)
    next_expert = jnp.where(next_expert == N_EXPERTS, -1, next_expert).astype(jnp.int32)
    a = _gmm1(xb, w_mlp1, b1g, b1l, block_e, next_expert, n_used, m, tn=min(1024, f))
    y = _gmm2(a, w_mlp2, b_mlp2.reshape(N_EXPERTS, 1, d), block_e, next_expert, n_used, m)
    return x2, y, dest, gate_w


def kernel(x, norm_mix_g, w_in, b_forget, w_branch_fox, w_branch_moba, w_out, norm_ffn_g, w_router, b_router,
           w_mlp1, b_mlp1, w_mlp2, b_mlp2, norm_final_g):
    depth = w_in.shape[0]
    b, s, d = x.shape
    for l in range(depth):
        x2, y, dest, gate_w = _layer(x, norm_mix_g[l], w_in[l], b_forget[l], w_branch_fox[l], w_branch_moba[l],
                                     w_out[l], norm_ffn_g[l], w_router[l], b_router[l], w_mlp1[l], b_mlp1[l],
                                     w_mlp2[l], b_mlp2[l])
        out = _combine(y, dest, gate_w, x2, norm_final_g, tc=min(128, b * s), final_norm=l == depth - 1)
        x = out.reshape(b, s, d)
    return x
```

```python
import functools
import math

import jax
import jax.numpy as jnp
import numpy as np
from jax import lax
from jax.experimental import pallas as pl
from jax.experimental.pallas import tpu as pltpu

HEAD_DIM = 128
N_HEADS = 8
MOBA_BLOCK = 256
MOBA_TOPK = 3
ROPE_THETA = 500000.0
ROPE_DIM = HEAD_DIM // 4
N_EXPERTS = 32
TOP_K = 4
SWIGLU_ALPHA = 1.702
SWIGLU_LIMIT = 7.0
RMS_EPS = 1e-5

LANES = 128
SUBLANES = 8
MXU_COLS = 256
VMEM_LIMIT_BYTES = 56 * 1024 * 1024
MOE_ROWS = 512
ATTN_Q_ROWS = 256
ATTN_PAST_BLOCKS = 1

F32 = jnp.float32
BF16 = jnp.bfloat16
U32 = jnp.uint32
NEG_INF = float("-inf")
MASKED = -1e30
LOG2E = math.log2(math.e)


def _params(sem, **kw):
    return pltpu.CompilerParams(dimension_semantics=sem, vmem_limit_bytes=VMEM_LIMIT_BYTES, **kw)


def _pack_pairs(lo, hi):
    lo_b = lax.bitcast_convert_type(lo.astype(BF16).astype(F32), U32) >> 16
    hi_b = lax.bitcast_convert_type(hi.astype(BF16).astype(F32), U32) & jnp.uint32(0xFFFF0000)
    return hi_b | lo_b


def _unpack_pairs(w):
    lo = lax.bitcast_convert_type(w << 16, F32)
    hi = lax.bitcast_convert_type(w & jnp.uint32(0xFFFF0000), F32)
    return lo, hi


def _store_token_tiles(ref, val):
    n, d = val.shape
    njp = d // (2 * LANES)
    for j in range(njp):
        lo = val[:, j * LANES:(j + 1) * LANES]
        hi = val[:, (njp + j) * LANES:(njp + j + 1) * LANES]
        ref[pl.ds(j, n, stride=njp), :] = _pack_pairs(lo, hi)


def _token_pitch(njp):
    p = -(-njp // 4)
    return 4 * (p if p % 2 else p + 1)


def _token_copy(src_hbm, row, dst_vmem, slot, njp, sem):
    return pltpu.make_async_copy(src_hbm.at[pl.ds(pl.multiple_of(row * njp, njp), njp)],
                                 dst_vmem.at[pl.ds(slot * _token_pitch(njp), njp)], sem)


def _load_token_cols(buf, n, njp, j):
    return _unpack_pairs(buf[pl.ds(j, n, stride=_token_pitch(njp)), :])


def _rmsnorm_kernel(x_ref, g_ref, o_ref):
    x = x_ref[...]
    ms = jnp.mean(x * x, axis=-1, keepdims=True)
    o_ref[...] = (x * lax.rsqrt(ms + RMS_EPS) * g_ref[...]).astype(o_ref.dtype)


def _rmsnorm(x, g, tm):
    t, d = x.shape
    return pl.pallas_call(
        _rmsnorm_kernel,
        out_shape=jax.ShapeDtypeStruct((t, d), BF16),
        grid=(t // tm,),
        in_specs=[pl.BlockSpec((tm, d), lambda i: (i, 0)), pl.BlockSpec((1, d), lambda i: (0, 0))],
        out_specs=pl.BlockSpec((tm, d), lambda i: (i, 0)),
        compiler_params=_params(("parallel",)),
        name="rmsnorm_mix",
    )(x, g.reshape(1, d))


def _proj_kernel(a_ref, w_ref, cs_ref, *rest, rotary, heads):
    if rotary:
        cos_ref, sin_ref, o_ref = rest
    else:
        (o_ref,) = rest
    res = jnp.dot(a_ref[...], w_ref[...], preferred_element_type=F32) * cs_ref[...]
    for j in range(res.shape[1] // LANES):
        r = res[:, j * LANES:(j + 1) * LANES]
        if rotary:
            r = r * cos_ref[...] + pltpu.roll(r, LANES // 2, 1) * sin_ref[...]
        if heads:
            o_ref[0, j] = r.astype(o_ref.dtype)
        else:
            o_ref[:, j * LANES:(j + 1) * LANES] = r.astype(o_ref.dtype)


def _proj(a, w, colscale, *, seq, tm, tn, out_dtype, heads, rope=None, name):
    t, k = a.shape
    n = w.shape[1]
    sblk = seq // tm
    in_specs = [
        pl.BlockSpec((tm, k), lambda m, j: (m, 0)),
        pl.BlockSpec((k, tn), lambda m, j: (0, j)),
        pl.BlockSpec((1, tn), lambda m, j: (0, j)),
    ]
    args = [a, w, colscale]
    if rope is not None:
        in_specs += [pl.BlockSpec((tm, LANES), lambda m, j: (m % sblk, 0))] * 2
        args += list(rope)
    if heads:
        out_shape = jax.ShapeDtypeStruct((t // seq, n // LANES, seq, LANES), out_dtype)
        out_spec = pl.BlockSpec((1, tn // LANES, tm, LANES), lambda m, j: (m // sblk, j, m % sblk, 0))
    else:
        out_shape = jax.ShapeDtypeStruct((t, n), out_dtype)
        out_spec = pl.BlockSpec((tm, tn), lambda m, j: (m, j))
    return pl.pallas_call(
        functools.partial(_proj_kernel, rotary=rope is not None, heads=heads),
        out_shape=out_shape,
        grid=(t // tm, n // tn),
        in_specs=in_specs,
        out_specs=out_spec,
        compiler_params=_params(("parallel", "arbitrary")),
        name=name,
    )(*args)


def _rope_head_order(w):
    half = ROPE_DIM // 2
    k, n = w.shape
    w = w.reshape(k, n // HEAD_DIM, HEAD_DIM)
    w = jnp.concatenate([w[..., :half], w[..., 2 * half:LANES // 2 + half], w[..., half:2 * half],
                         w[..., LANES // 2 + half:]], axis=-1)
    return w.reshape(k, n)


def _rope_tables(seq):
    half = ROPE_DIM // 2
    inv_freq = 1.0 / (ROPE_THETA ** (jnp.arange(half, dtype=F32) / half))
    ang = jnp.arange(seq, dtype=F32)[:, None] * inv_freq[None, :]
    cos, sin = jnp.cos(ang), jnp.sin(ang)
    ones = jnp.ones((seq, LANES // 2 - half), F32)
    zeros = jnp.zeros((seq, LANES // 2 - half), F32)
    cos_t = jnp.concatenate([cos, ones, cos, ones], axis=1)
    sin_t = jnp.concatenate([-sin, zeros, sin, zeros], axis=1)
    return cos_t, sin_t


def _forget_kernel(fl_ref, b_ref, c_ref):
    z = fl_ref[...] + b_ref[...]
    logf = jnp.minimum(z, 0.0) - jnp.log1p(jnp.exp(-jnp.abs(z)))
    row = lax.broadcasted_iota(jnp.int32, (LANES, LANES), 0)
    col = lax.broadcasted_iota(jnp.int32, (LANES, LANES), 1)
    tri = (row <= col).astype(BF16)
    carry = jnp.zeros((logf.shape[0], 1), F32)
    for j in range(logf.shape[1] // LANES):
        xb = logf[:, j * LANES:(j + 1) * LANES]
        hi = xb.astype(BF16)
        r1 = xb - hi.astype(F32)
        mid = r1.astype(BF16)
        lo = (r1 - mid.astype(F32)).astype(BF16)
        cs = (jnp.dot(hi, tri, preferred_element_type=F32) + jnp.dot(mid, tri, preferred_element_type=F32)
              + jnp.dot(lo, tri, preferred_element_type=F32)) + carry
        c_ref[:, j * LANES:(j + 1) * LANES] = cs * LOG2E
        carry = cs[:, LANES - 1:LANES]


def _forget_cumsum(fl_t, b_forget):
    b, h, s = fl_t.shape
    return pl.pallas_call(
        _forget_kernel,
        out_shape=jax.ShapeDtypeStruct((b, h, s), F32),
        grid=(b,),
        in_specs=[pl.BlockSpec((None, h, s), lambda i: (i, 0, 0)), pl.BlockSpec((h, 1), lambda i: (0, 0))],
        out_specs=pl.BlockSpec((None, h, s), lambda i: (i, 0, 0)),
        compiler_params=_params(("parallel",)),
        name="forget_cumsum",
    )(fl_t, b_forget.reshape(h, 1).astype(F32))


def _attn_kernel(*refs, moba, tq, tk):
    if moba:
        q_ref, k_ref, v_ref, o_ref, vaug_ref, km_ref, kaug_ref, qaug_ref = refs
    else:
        q_ref, k_ref, v_ref, c_ref, o_ref, vaug_ref = refs
    d = q_ref.shape[1]
    nkb = q_ref.shape[0] // tk
    nt = (((1,), (1,)), ((), ()))
    row = lax.broadcasted_iota(jnp.int32, (tq, tk), 0)
    col = lax.broadcasted_iota(jnp.int32, (tq, tk), 1)

    vaug_ref[:, :d] = v_ref[...]
    vaug_ref[:, d:] = jnp.ones((v_ref.shape[0], LANES), BF16)
    if moba:
        lane = lax.broadcasted_iota(jnp.int32, (tk, LANES), 1)
        km_ref[...] = jnp.zeros_like(km_ref)
        for n in range(nkb):
            kb = k_ref[n * tk:(n + 1) * tk, :]
            km_ref[n:n + 1, :] = jnp.mean(kb.astype(F32), axis=0, keepdims=True)
            kaug_ref[n * tk:(n + 1) * tk, :d] = kb
            kaug_ref[n * tk:(n + 1) * tk, d:] = (lane == n).astype(BF16)
        s_len = q_ref.shape[0]
        g = lax.dot_general(km_ref[...].astype(BF16), q_ref[...], nt, preferred_element_type=F32)[:SUBLANES]
        blk = lax.broadcasted_iota(jnp.int32, (SUBLANES, s_len), 0)
        past = blk * tk + (tk - 1) < lax.broadcasted_iota(jnp.int32, (SUBLANES, s_len), 1)
        pen = jnp.zeros((SUBLANES, s_len), F32)
        for n in range(nkb - 1):
            g_n = g[n:n + 1, :]
            beats = ((g > g_n) | ((g == g_n) & (blk < n))) & past
            rank = jnp.sum(beats.astype(F32), axis=0, keepdims=True)
            pen = jnp.where((blk == n) & past & (rank >= MOBA_TOPK), MASKED, pen)
        pen = jnp.concatenate([pen, jnp.zeros((LANES - SUBLANES, s_len), F32)], axis=0)
        qaug_ref[:, :d] = q_ref[...]
        qaug_ref[:, d:] = pen.T.astype(BF16)

    def logits(q, n0, n1):
        s = lax.dot_general(q, (kaug_ref if moba else k_ref)[n0 * tk:n1 * tk, :], nt,
                            preferred_element_type=F32)
        return s if moba else s - c_ref[:, n0 * tk:n1 * tk]

    for nb in range(nkb):
        for h in range(tk // tq):
            r0 = nb * tk + h * tq
            q = (qaug_ref if moba else q_ref)[r0:r0 + tq, :]
            s = jnp.where(row + h * tq >= col, logits(q, nb, nb + 1), NEG_INF)
            m = jnp.max(s, axis=1, keepdims=True)
            acc = jnp.dot(jnp.exp2(s - m).astype(BF16), vaug_ref[nb * tk:(nb + 1) * tk, :],
                          preferred_element_type=F32)
            for n0 in range(0, nb, ATTN_PAST_BLOCKS):
                n1 = min(n0 + ATTN_PAST_BLOCKS, nb)
                s = logits(q, n0, n1)
                m_new = jnp.maximum(m, jnp.max(s, axis=1, keepdims=True))
                acc = jnp.exp2(m - m_new) * acc + jnp.dot(jnp.exp2(s - m_new).astype(BF16),
                                                          vaug_ref[n0 * tk:n1 * tk, :],
                                                          preferred_element_type=F32)
                m = m_new
            o_ref[r0:r0 + tq, :] = (acc[:, :d] / acc[:, d:]).astype(o_ref.dtype)


def _attention(q_arr, q_off, k_arr, k_off, v_arr, v_off, c_rows, *, moba, name):
    b, _, s, d = q_arr.shape
    tk = MOBA_BLOCK
    tq = min(ATTN_Q_ROWS, tk)
    assert s % tk == 0 and s // tk <= SUBLANES
    in_specs = [
        pl.BlockSpec((None, None, s, d), lambda bi, h: (bi, q_off + h, 0, 0)),
        pl.BlockSpec((None, None, s, d), lambda bi, h: (bi, k_off + h, 0, 0)),
        pl.BlockSpec((None, None, s, d), lambda bi, h: (bi, v_off + h, 0, 0)),
    ]
    args = [q_arr, k_arr, v_arr]
    scratch = [pltpu.VMEM((s, d + LANES), BF16)]
    if moba:
        scratch += [pltpu.VMEM((LANES, d), F32), pltpu.VMEM((s, d + LANES), BF16), pltpu.VMEM((s, d + LANES), BF16)]
    else:
        in_specs.append(pl.BlockSpec((None, None, 1, s), lambda bi, h: (bi, h, 0, 0)))
        args.append(c_rows)
    return pl.pallas_call(
        functools.partial(_attn_kernel, moba=moba, tq=tq, tk=tk),
        out_shape=jax.ShapeDtypeStruct((b * s, N_HEADS * d), BF16),
        grid=(b, N_HEADS),
        in_specs=in_specs,
        out_specs=pl.BlockSpec((s, d), lambda bi, h: (bi, h)),
        scratch_shapes=scratch,
        compiler_params=_params(("parallel", "parallel")),
        name=name,
    )(*args)


def _mix_kernel(of_ref, om_ref, g_ref, x_ref, wf_ref, wm_ref, wo_ref, gn_ref, wr_ref, br_ref,
                x2_ref, h2_ref, lg_ref):
    d = x_ref.shape[1]
    yf = jnp.dot(of_ref[...], wf_ref[...], preferred_element_type=F32)
    ym = jnp.dot(om_ref[...], wm_ref[...], preferred_element_type=F32)
    mixed = jax.nn.sigmoid(g_ref[:, :d].astype(F32)) * yf + jax.nn.sigmoid(g_ref[:, d:].astype(F32)) * ym
    x2 = x_ref[...] + jnp.dot(mixed.astype(BF16), wo_ref[...], preferred_element_type=F32)
    x2_ref[...] = x2
    ms = jnp.mean(x2 * x2, axis=-1, keepdims=True)
    h2 = x2 * lax.rsqrt(ms + RMS_EPS) * gn_ref[...]
    _store_token_tiles(h2_ref, h2)
    lg_ref[...] = jnp.dot(h2.astype(BF16), wr_ref[...], preferred_element_type=F32) + br_ref[...]


def _mix(o_fox, o_moba, gates, x, w_f, w_m, w_o, g_ffn, w_r, b_r, tm):
    t, d = x.shape
    w = o_fox.shape[1]
    njp = d // (2 * LANES)
    const = lambda i: (0, 0)
    resident = dict(pipeline_mode=pl.Buffered(1))
    return pl.pallas_call(
        _mix_kernel,
        out_shape=(jax.ShapeDtypeStruct((t, d), F32), jax.ShapeDtypeStruct((t * njp, LANES), U32),
                   jax.ShapeDtypeStruct((t, LANES), F32)),
        grid=(t // tm,),
        in_specs=[
            pl.BlockSpec((tm, w), lambda i: (i, 0)),
            pl.BlockSpec((tm, w), lambda i: (i, 0)),
            pl.BlockSpec((tm, 2 * d), lambda i: (i, 0)),
            pl.BlockSpec((tm, d), lambda i: (i, 0)),
            pl.BlockSpec((w, d), const, **resident),
            pl.BlockSpec((w, d), const, **resident),
            pl.BlockSpec((d, d), const, **resident),
            pl.BlockSpec((1, d), const),
            pl.BlockSpec((d, LANES), const),
            pl.BlockSpec((1, LANES), const),
        ],
        out_specs=(pl.BlockSpec((tm, d), lambda i: (i, 0)), pl.BlockSpec((tm * njp, LANES), lambda i: (i, 0)),
                   pl.BlockSpec((tm, LANES), lambda i: (i, 0))),
        compiler_params=_params(("parallel",)),
        name="mix_out_router",
    )(o_fox, o_moba, gates, x, w_f, w_m, w_o, g_ffn, w_r, b_r)


def _router_kernel(lg_ref, meta_ref, gate_ref, cnt_ref, carry_ref):
    @pl.when(pl.program_id(0) == 0)
    def _():
        carry_ref[...] = jnp.zeros_like(carry_ref)

    l = lg_ref[...]
    tr = l.shape[0]
    lane = lax.broadcasted_iota(jnp.int32, (tr, LANES), 1).astype(F32)
    vals, idxs = [], []
    for _ in range(TOP_K):
        m = jnp.max(l, axis=1, keepdims=True)
        ix = jnp.min(jnp.where(l == m, lane, float(LANES)), axis=1, keepdims=True)
        vals.append(m)
        idxs.append(ix)
        l = jnp.where(lane == ix, NEG_INF, l)
    ex = [jnp.exp(v - vals[0]) for v in vals]
    den = ex[0] + ex[1] + ex[2] + ex[3]
    onehot = jnp.zeros((tr, LANES), F32)
    for ix in idxs:
        onehot = onehot + (lane == ix).astype(F32)
    row = lax.broadcasted_iota(jnp.int32, (tr, tr), 0)
    col = lax.broadcasted_iota(jnp.int32, (tr, tr), 1)
    before = (col < row).astype(BF16)
    running = jnp.dot(before, onehot.astype(BF16), preferred_element_type=F32) + carry_ref[...]
    meta = jnp.zeros((tr, LANES), F32)
    gate = jnp.zeros((tr, LANES), F32)
    for k in range(TOP_K):
        pos = jnp.sum(jnp.where(lane == idxs[k], running, 0.0), axis=1, keepdims=True)
        meta = jnp.where(lane == float(k), idxs[k], meta)
        meta = jnp.where(lane == float(TOP_K + k), pos, meta)
        gate = jnp.where(lane == float(k), ex[k] / den, gate)
    meta_ref[...] = meta.astype(jnp.int32)
    gate_ref[...] = gate
    carry_ref[...] += jnp.sum(onehot, axis=0, keepdims=True)
    cnt_ref[...] = carry_ref[...]


def _router(logits, tr):
    t = logits.shape[0]
    return pl.pallas_call(
        _router_kernel,
        out_shape=(jax.ShapeDtypeStruct((t, LANES), jnp.int32), jax.ShapeDtypeStruct((t, LANES), F32),
                   jax.ShapeDtypeStruct((1, LANES), F32)),
        grid=(t // tr,),
        in_specs=[pl.BlockSpec((tr, LANES), lambda i: (i, 0))],
        out_specs=(pl.BlockSpec((tr, LANES), lambda i: (i, 0)), pl.BlockSpec((tr, LANES), lambda i: (i, 0)),
                   pl.BlockSpec((1, LANES), lambda i: (0, 0))),
        scratch_shapes=[pltpu.VMEM((1, LANES), F32)],
        compiler_params=_params(("arbitrary",)),
        name="router_topk",
    )(logits)


def _gather_kernel(nused_ref, tok_ref, tok_next_ref, h_hbm, o_ref, buf, sem, *, njp):
    r = pl.program_id(0)
    m = o_ref.shape[0]
    slot = r % 2

    def issue(tok, s):
        def body(i, c):
            _token_copy(h_hbm, tok[0, 0, 2 * i], buf.at[s], 2 * i, njp, sem.at[s]).start(priority=0)
            _token_copy(h_hbm, tok[0, 0, 2 * i + 1], buf.at[s], 2 * i + 1, njp, sem.at[s]).start(priority=1)
            return c

        lax.fori_loop(0, m // 2, body, 0, unroll=4)

    @pl.when(r == 0)
    def _():
        issue(tok_ref, 0)

    @pl.when(r + 1 < nused_ref[0])
    def _():
        issue(tok_next_ref, 1 - slot)

    @pl.when(r < nused_ref[0])
    def _():
        pltpu.make_async_copy(h_hbm.at[pl.ds(0, m * njp)], buf.at[slot, pl.ds(0, m * njp)], sem.at[slot]).wait()
        for j in range(njp):
            lo, hi = _load_token_cols(buf.at[slot], m, njp, j)
            o_ref[:, j * LANES:(j + 1) * LANES] = lo.astype(o_ref.dtype)
            o_ref[:, (njp + j) * LANES:(njp + j + 1) * LANES] = hi.astype(o_ref.dtype)

    @pl.when(r >= nused_ref[0])
    def _():
        o_ref[...] = jnp.zeros_like(o_ref)


def _gather_rows(h_tiles, row_tok, n_used, m, njp):
    n_blocks = row_tok.shape[0] // m
    tok3 = row_tok.reshape(n_blocks, 1, m)
    return pl.pallas_call(
        functools.partial(_gather_kernel, njp=njp),
        out_shape=jax.ShapeDtypeStruct((n_blocks * m, 2 * njp * LANES), BF16),
        grid_spec=pltpu.PrefetchScalarGridSpec(
            num_scalar_prefetch=1,
            grid=(n_blocks,),
            in_specs=[
                pl.BlockSpec((1, 1, m), lambda r, nu: (r, 0, 0), memory_space=pltpu.SMEM),
                pl.BlockSpec((1, 1, m), lambda r, nu: (jnp.minimum(r + 1, n_blocks - 1), 0, 0),
                             memory_space=pltpu.SMEM),
                pl.BlockSpec(memory_space=pl.ANY),
            ],
            out_specs=pl.BlockSpec((m, 2 * njp * LANES), lambda r, nu: (r, 0)),
            scratch_shapes=[pltpu.VMEM((2, m * _token_pitch(njp), LANES), U32), pltpu.SemaphoreType.DMA((2,))],
        ),
        compiler_params=_params(("arbitrary",), disable_bounds_checks=True),
        name="expert_row_gather",
    )(n_used, tok3, tok3, h_tiles)


def _group_start(be_ref, r):
    return jnp.logical_or(r == 0, be_ref[r] != be_ref[jnp.maximum(r - 1, 0)])


def _gmm1_kernel(be_ref, nne_ref, nused_ref, x_ref, w_hbm, perm_ref, bg_ref, bl_ref, a_ref,
                 stage, wg_s, wl_s, sem):
    j = pl.program_id(0)
    r = pl.program_id(1)
    pw = perm_ref.shape[0]
    tw = stage.shape[1]

    def fetch(jj, e):
        return pltpu.make_async_copy(w_hbm.at[e, :, pl.ds(pl.multiple_of(jj * tw, tw), tw)], stage, sem)

    @pl.when((j == 0) & (r == 0))
    def _():
        fetch(0, be_ref[0]).start()

    @pl.when(r < nused_ref[0])
    def _():
        @pl.when(_group_start(be_ref, r))
        def _():
            fetch(j, be_ref[r]).wait()
            for c in range(tw // pw):
                wb = stage[:, c * pw:(c + 1) * pw].astype(BF16)
                sp = jnp.dot(wb, perm_ref[...], preferred_element_type=F32).astype(BF16)
                wg_s[:, c * (pw // 2):(c + 1) * (pw // 2)] = sp[:, :pw // 2]
                wl_s[:, c * (pw // 2):(c + 1) * (pw // 2)] = sp[:, pw // 2:]
            nxt = nne_ref[be_ref[r]]

            @pl.when(nxt >= 0)
            def _():
                fetch(j, nxt).start()

            @pl.when((nxt < 0) & (j + 1 < pl.num_programs(0)))
            def _():
                fetch(j + 1, be_ref[0]).start()

        x = x_ref[...]
        ug = jnp.dot(x, wg_s[...], preferred_element_type=F32) + bg_ref[...]
        ul = jnp.dot(x, wl_s[...], preferred_element_type=F32) + bl_ref[...]
        glu = jnp.minimum(ug, SWIGLU_LIMIT)
        lin = jnp.clip(ul, -SWIGLU_LIMIT, SWIGLU_LIMIT)
        a_ref[...] = (glu * jax.nn.sigmoid(SWIGLU_ALPHA * glu) * (lin + 1.0)).astype(a_ref.dtype)

    @pl.when(r >= nused_ref[0])
    def _():
        a_ref[...] = jnp.zeros_like(a_ref)


def _deinterleave_perm(width):
    src = np.arange(width)
    dst = np.where(src % 2 == 0, src // 2, width // 2 + src // 2)
    p = np.zeros((width, width), np.float32)
    p[src, dst] = 1.0
    return jnp.asarray(p, BF16)


def _gmm1(xb, w1, b1g, b1l, block_e, next_expert, n_used, m, tn):
    rows, d = xb.shape
    f = w1.shape[2] // 2
    n_blocks = rows // m
    pw = 2 * LANES

    def rc(r, nu):
        return jnp.minimum(r, nu[0] - 1)

    return pl.pallas_call(
        _gmm1_kernel,
        out_shape=jax.ShapeDtypeStruct((rows, f), BF16),
        grid_spec=pltpu.PrefetchScalarGridSpec(
            num_scalar_prefetch=3,
            grid=(f // tn, n_blocks),
            in_specs=[
                pl.BlockSpec((m, d), lambda j, r, be, nne, nu: (rc(r, nu), 0)),
                pl.BlockSpec(memory_space=pl.ANY),
                pl.BlockSpec((pw, pw), lambda j, r, be, nne, nu: (0, 0)),
                pl.BlockSpec((None, 1, tn), lambda j, r, be, nne, nu: (be[rc(r, nu)], 0, j)),
                pl.BlockSpec((None, 1, tn), lambda j, r, be, nne, nu: (be[rc(r, nu)], 0, j)),
            ],
            out_specs=pl.BlockSpec((m, tn), lambda j, r, be, nne, nu: (r, j)),
            scratch_shapes=[pltpu.VMEM((d, 2 * tn), F32), pltpu.VMEM((d, tn), BF16), pltpu.VMEM((d, tn), BF16),
                            pltpu.SemaphoreType.DMA(())],
        ),
        compiler_params=_params(("arbitrary", "arbitrary")),
        name="expert_mlp1_swiglu",
    )(block_e, next_expert, n_used, xb, w1, _deinterleave_perm(pw), b1g, b1l)


def _gmm2_kernel(be_ref, nne_ref, nused_ref, a_ref, w_hbm, b_ref, y_ref, stage, w_s, sem):
    r = pl.program_id(0)

    def fetch(e):
        return pltpu.make_async_copy(w_hbm.at[e], stage, sem)

    @pl.when(r == 0)
    def _():
        fetch(be_ref[0]).start()

    @pl.when(r < nused_ref[0])
    def _():
        @pl.when(_group_start(be_ref, r))
        def _():
            fetch(be_ref[r]).wait()
            w_s[...] = stage[...].astype(BF16)
            nxt = nne_ref[be_ref[r]]

            @pl.when(nxt >= 0)
            def _():
                fetch(nxt).start()

        a = a_ref[...]
        n = a.shape[0]
        half = w_s.shape[1] // 2
        njp = half // LANES
        cw = min(MXU_COLS, half)
        for c in range(half // cw):
            lo_cols = slice(c * cw, (c + 1) * cw)
            hi_cols = slice(half + c * cw, half + (c + 1) * cw)
            y_lo = jnp.dot(a, w_s[:, lo_cols], preferred_element_type=F32) + b_ref[:, lo_cols]
            y_hi = jnp.dot(a, w_s[:, hi_cols], preferred_element_type=F32) + b_ref[:, hi_cols]
            for jj in range(cw // LANES):
                lanes = slice(jj * LANES, (jj + 1) * LANES)
                y_ref[pl.ds(c * (cw // LANES) + jj, n, stride=njp), :] = _pack_pairs(y_lo[:, lanes], y_hi[:, lanes])

    @pl.when(r >= nused_ref[0])
    def _():
        y_ref[...] = jnp.zeros_like(y_ref)


def _gmm2(a, w2, b2, block_e, next_expert, n_used, m):
    rows, f = a.shape
    d = w2.shape[2]
    njp = d // (2 * LANES)
    n_blocks = rows // m

    def rc(r, nu):
        return jnp.minimum(r, nu[0] - 1)

    return pl.pallas_call(
        _gmm2_kernel,
        out_shape=jax.ShapeDtypeStruct((rows * njp, LANES), U32),
        grid_spec=pltpu.PrefetchScalarGridSpec(
            num_scalar_prefetch=3,
            grid=(n_blocks,),
            in_specs=[
                pl.BlockSpec((m, f), lambda r, be, nne, nu: (rc(r, nu), 0)),
                pl.BlockSpec(memory_space=pl.ANY),
                pl.BlockSpec((None, 1, d), lambda r, be, nne, nu: (be[rc(r, nu)], 0, 0)),
            ],
            out_specs=pl.BlockSpec((m * njp, LANES), lambda r, be, nne, nu: (r, 0)),
            scratch_shapes=[pltpu.VMEM((f, d), F32), pltpu.VMEM((f, d), BF16), pltpu.SemaphoreType.DMA(())],
        ),
        compiler_params=_params(("arbitrary",)),
        name="expert_mlp2",
    )(block_e, next_expert, n_used, a, w2, b2)


def _combine_kernel(dest_ref, dest_next_ref, y_hbm, gate_ref, x2_ref, g_ref, o_ref, buf, sem, *, final_norm):
    i = pl.program_id(0)
    tc, d = x2_ref.shape
    njp = d // (2 * LANES)
    slot = i % 2

    def issue(dest, s):
        def body(t, c):
            for k in range(TOP_K):
                _token_copy(y_hbm, dest[0, 0, t * TOP_K + k], buf.at[s, k], t, njp, sem.at[s]).start(priority=k % 2)
            return c

        lax.fori_loop(0, tc, body, 0, unroll=2)

    @pl.when(i == 0)
    def _():
        issue(dest_ref, 0)

    @pl.when(i + 1 < pl.num_programs(0))
    def _():
        issue(dest_next_ref, 1 - slot)

    for k in range(TOP_K):
        pltpu.make_async_copy(y_hbm.at[pl.ds(0, tc * njp)], buf.at[slot, k, pl.ds(0, tc * njp)],
                              sem.at[slot]).wait()
    lo_cols, hi_cols = [], []
    for j in range(njp):
        acc_lo = x2_ref[:, j * LANES:(j + 1) * LANES]
        acc_hi = x2_ref[:, (njp + j) * LANES:(njp + j + 1) * LANES]
        for k in range(TOP_K):
            lo, hi = _load_token_cols(buf.at[slot, k], tc, njp, j)
            g = gate_ref[:, k:k + 1]
            acc_lo = acc_lo + g * lo
            acc_hi = acc_hi + g * hi
        lo_cols.append(acc_lo)
        hi_cols.append(acc_hi)
    acc = jnp.concatenate(lo_cols + hi_cols, axis=1)
    if final_norm:
        ms = jnp.mean(acc * acc, axis=-1, keepdims=True)
        acc = acc * lax.rsqrt(ms + RMS_EPS) * g_ref[...]
    o_ref[...] = acc


def _combine(y_tiles, dest, gates, x2, g_final, tc, final_norm):
    t, d = x2.shape
    njp = d // (2 * LANES)
    nt = t // tc
    dest3 = dest.reshape(nt, 1, tc * TOP_K)
    return pl.pallas_call(
        functools.partial(_combine_kernel, final_norm=final_norm),
        out_shape=jax.ShapeDtypeStruct((t, d), F32),
        grid=(nt,),
        in_specs=[
            pl.BlockSpec((1, 1, tc * TOP_K), lambda i: (i, 0, 0), memory_space=pltpu.SMEM),
            pl.BlockSpec((1, 1, tc * TOP_K), lambda i: (jnp.minimum(i + 1, nt - 1), 0, 0),
                         memory_space=pltpu.SMEM),
            pl.BlockSpec(memory_space=pl.ANY),
            pl.BlockSpec((tc, LANES), lambda i: (i, 0)),
            pl.BlockSpec((tc, d), lambda i: (i, 0)),
            pl.BlockSpec((1, d), lambda i: (0, 0)),
        ],
        out_specs=pl.BlockSpec((tc, d), lambda i: (i, 0)),
        scratch_shapes=[pltpu.VMEM((2, TOP_K, tc * _token_pitch(njp), LANES), U32),
                        pltpu.SemaphoreType.DMA((2,))],
        compiler_params=_params(("arbitrary",), disable_bounds_checks=True),
        name="moe_combine_final_norm",
    )(dest3, dest3, y_tiles, gates, x2, g_final.reshape(1, d))


def _layer(x, norm_mix_g, w_in, b_forget, w_branch_fox, w_branch_moba, w_out, norm_ffn_g,
           w_router, b_router, w_mlp1, b_mlp1, w_mlp2, b_mlp2):
    b, s, d = x.shape
    t = b * s
    width = N_HEADS * HEAD_DIM
    xt = x.reshape(t, d)
    q_scale = LOG2E * HEAD_DIM ** -0.5

    o = np.cumsum([0, width, width, width, N_HEADS, width, width, width, d, d])
    w_fox = w_in[:, o[0]:o[3]].astype(BF16)
    w_fl = jnp.pad(w_in[:, o[3]:o[4]], ((0, 0), (0, LANES - N_HEADS))).astype(BF16)
    w_mqk = _rope_head_order(w_in[:, o[4]:o[6]]).astype(BF16)
    w_mv = w_in[:, o[6]:o[7]].astype(BF16)
    w_g = w_in[:, o[7]:o[9]].astype(BF16)
    colscale = jnp.concatenate([jnp.full((1, width), q_scale, F32), jnp.ones((1, 2 * width), F32)], axis=1)

    tm = min(2048, s)
    h = _rmsnorm(xt, norm_mix_g, tm=min(512, t))
    fox = _proj(h, w_fox, colscale, seq=s, tm=tm, tn=512, out_dtype=BF16, heads=True, name="proj_fox_qkv")
    mqk = _proj(h, w_mqk, colscale[:, :2 * width], seq=s, tm=tm, tn=512, out_dtype=BF16, heads=True,
                rope=_rope_tables(s), name="proj_moba_qk")
    mv = _proj(h, w_mv, colscale[:, width:2 * width], seq=s, tm=tm, tn=512, out_dtype=BF16, heads=True,
               name="proj_moba_v")
    gates = _proj(h, w_g, jnp.ones((1, 2 * d), F32), seq=s, tm=tm, tn=512, out_dtype=BF16, heads=False,
                  name="proj_gates")
    fl = _proj(h, w_fl, jnp.ones((1, LANES), F32), seq=s, tm=tm, tn=LANES, out_dtype=F32, heads=False,
               name="proj_forget")

    fl_t = fl[:, :N_HEADS].reshape(b, s, N_HEADS).transpose(0, 2, 1)
    c = _forget_cumsum(fl_t, b_forget)
    c_rows = c.reshape(b, N_HEADS, 1, s)

    o_fox = _attention(fox, 0, fox, N_HEADS, fox, 2 * N_HEADS, c_rows, moba=False, name="fox_attention")
    o_moba = _attention(mqk, 0, mqk, N_HEADS, mv, 0, None, moba=True, name="moba_attention")

    w_r = jnp.pad(w_router, ((0, 0), (0, LANES - N_EXPERTS))).astype(BF16)
    b_r = jnp.concatenate([b_router.astype(F32), jnp.full((LANES - N_EXPERTS,), NEG_INF, F32)]).reshape(1, LANES)
    x2, h2, logits = _mix(o_fox, o_moba, gates, xt, w_branch_fox.astype(BF16), w_branch_moba.astype(BF16),
                          w_out.astype(BF16), norm_ffn_g.reshape(1, d), w_r, b_r, tm=min(256, t))

    meta, gate_w, cnt = _router(logits, tr=min(512, t))

    m = MOE_ROWS
    assign = t * TOP_K
    n_blocks = (assign + N_EXPERTS * (m - 1) + m - 1) // m
    counts = cnt[0, :N_EXPERTS].astype(jnp.int32)
    padded = (counts + m - 1) // m * m
    pend = jnp.cumsum(padded)
    pstart = pend - padded
    experts = jnp.arange(N_EXPERTS, dtype=jnp.int32)
    e_sel = meta[:, :TOP_K]
    dest = meta[:, TOP_K:2 * TOP_K] + jnp.sum(jnp.where(e_sel[..., None] == experts, pstart, 0), axis=-1)
    n_used = (pend[-1] // m).astype(jnp.int32).reshape(1)
    first_row = jnp.arange(n_blocks, dtype=jnp.int32) * m
    block_e = jnp.minimum(jnp.sum((pend[None, :] <= first_row[:, None]).astype(jnp.int32), axis=1), N_EXPERTS - 1)
    keys = jnp.sort((e_sel * t + jnp.arange(t, dtype=jnp.int32)[:, None]).reshape(-1))
    start = jnp.cumsum(counts) - counts
    of_block = block_e[:, None] == experts[None, :]
    blk_start = jnp.sum(jnp.where(of_block, start, 0), axis=1)
    blk_count = jnp.sum(jnp.where(of_block, counts, 0), axis=1)
    blk_first = first_row - jnp.sum(jnp.where(of_block, pstart, 0), axis=1)
    in_expert = blk_first[:, None] + jnp.arange(m, dtype=jnp.int32)[None, :]
    src = jnp.clip(blk_start[:, None] + in_expert, 0, assign - 1)
    row_tok = jnp.where((in_expert < blk_count[:, None]).reshape(-1), keys[src.reshape(-1)] % t,
                        jnp.arange(n_blocks * m, dtype=jnp.int32) % t).astype(jnp.int32)

    xb = _gather_rows(h2, row_tok, n_used, m, d // (2 * LANES))
    f = w_mlp1.shape[2] // 2
    b1g = b_mlp1[:, 0::2].reshape(N_EXPERTS, 1, f)
    b1l = b_mlp1[:, 1::2].reshape(N_EXPERTS, 1, f)
    later = (experts[None, :] > experts[:, None]) & (counts[None, :] > 0)
    next_expert = jnp.min(jnp.where(later, experts[None, :], N_EXPERTS), axis=1)
    next_expert = jnp.where(next_expert == N_EXPERTS, -1, next_expert).astype(jnp.int32)
    a = _gmm1(xb, w_mlp1, b1g, b1l, block_e, next_expert, n_used, m, tn=min(1024, f))
    y = _gmm2(a, w_mlp2, b_mlp2.reshape(N_EXPERTS, 1, d), block_e, next_expert, n_used, m)
    return x2, y, dest, gate_w


def kernel(x, norm_mix_g, w_in, b_forget, w_branch_fox, w_branch_moba, w_out, norm_ffn_g, w_router, b_router,
           w_mlp1, b_mlp1, w_mlp2, b_mlp2, norm_final_g):
    depth = w_in.shape[0]
    b, s, d = x.shape
    for l in range(depth):
        x2, y, dest, gate_w = _layer(x, norm_mix_g[l], w_in[l], b_forget[l], w_branch_fox[l], w_branch_moba[l],
                                     w_out[l], norm_ffn_g[l], w_router[l], b_router[l], w_mlp1[l], b_mlp1[l],
                                     w_mlp2[l], b_mlp2[l])
        out = _combine(y, dest, gate_w, x2, norm_final_g, tc=min(128, b * s), final_norm=l == depth - 1)
        x = out.reshape(b, s, d)
    return x
```

```python
import functools
import math
from typing import NamedTuple

import jax
import jax.numpy as jnp
import numpy as np
from jax import lax
from jax.experimental import pallas as pl
from jax.experimental.pallas import tpu as pltpu

HEAD_DIM = 128
N_HEADS = 8
MOBA_BLOCK = 256
MOBA_TOPK = 3
ROPE_THETA = 500000.0
ROPE_DIM = HEAD_DIM // 4
N_EXPERTS = 32
TOP_K = 4
SWIGLU_ALPHA = 1.702
SWIGLU_LIMIT = 7.0
RMS_EPS = 1e-5

LANES = 128
SUBLANES = 8
MXU_COLS = 256
VMEM_LIMIT_BYTES = 56 * 1024 * 1024
MOE_ROWS = 512
ATTN_Q_ROWS = 256
ATTN_PAST_BLOCKS = 1

F32 = jnp.float32
BF16 = jnp.bfloat16
U32 = jnp.uint32
NEG_INF = float("-inf")
MASKED = -1e30
LOG2E = math.log2(math.e)


def _params(sem, **kw):
    return pltpu.CompilerParams(dimension_semantics=sem, vmem_limit_bytes=VMEM_LIMIT_BYTES, **kw)


class _Tiles(NamedTuple):
    norm_rows: int
    proj_rows: int
    proj_cols: int
    mix_rows: int
    router_rows: int
    moe_rows: int
    mlp1_cols: int
    combine_rows: int


def _tiles(t, s, f):
    return _Tiles(norm_rows=min(512, t), proj_rows=min(2048, s), proj_cols=512, mix_rows=min(256, t),
                  router_rows=min(512, t), moe_rows=MOE_ROWS, mlp1_cols=min(1024, f), combine_rows=min(128, t))


def _pack_pairs(lo, hi):
    lo_b = lax.bitcast_convert_type(lo.astype(BF16).astype(F32), U32) >> 16
    hi_b = lax.bitcast_convert_type(hi.astype(BF16).astype(F32), U32) & jnp.uint32(0xFFFF0000)
    return hi_b | lo_b


def _unpack_pairs(w):
    lo = lax.bitcast_convert_type(w << 16, F32)
    hi = lax.bitcast_convert_type(w & jnp.uint32(0xFFFF0000), F32)
    return lo, hi


def _store_token_tiles(ref, val):
    n, d = val.shape
    njp = d // (2 * LANES)
    for j in range(njp):
        lo = val[:, j * LANES:(j + 1) * LANES]
        hi = val[:, (njp + j) * LANES:(njp + j + 1) * LANES]
        ref[pl.ds(j, n, stride=njp), :] = _pack_pairs(lo, hi)


def _token_pitch(njp):
    p = -(-njp // 4)
    return 4 * (p if p % 2 else p + 1)


def _token_copy(src_hbm, row, dst_vmem, slot, njp, sem):
    return pltpu.make_async_copy(src_hbm.at[pl.ds(pl.multiple_of(row * njp, njp), njp)],
                                 dst_vmem.at[pl.ds(slot * _token_pitch(njp), njp)], sem)


def _load_token_cols(buf, n, njp, j):
    return _unpack_pairs(buf[pl.ds(j, n, stride=_token_pitch(njp)), :])


def _rmsnorm_kernel(x_ref, g_ref, o_ref):
    x = x_ref[...]
    ms = jnp.mean(x * x, axis=-1, keepdims=True)
    o_ref[...] = (x * lax.rsqrt(ms + RMS_EPS) * g_ref[...]).astype(o_ref.dtype)


def _rmsnorm(x, g, tm):
    t, d = x.shape
    return pl.pallas_call(
        _rmsnorm_kernel,
        out_shape=jax.ShapeDtypeStruct((t, d), BF16),
        grid=(t // tm,),
        in_specs=[pl.BlockSpec((tm, d), lambda i: (i, 0)), pl.BlockSpec((1, d), lambda i: (0, 0))],
        out_specs=pl.BlockSpec((tm, d), lambda i: (i, 0)),
        compiler_params=_params(("parallel",)),
        name="rmsnorm_mix",
    )(x, g.reshape(1, d))


def _proj_kernel(a_ref, w_ref, cs_ref, *rest, rotary, heads):
    if rotary:
        cos_ref, sin_ref, o_ref = rest
    else:
        (o_ref,) = rest
    res = jnp.dot(a_ref[...], w_ref[...], preferred_element_type=F32) * cs_ref[...]
    for j in range(res.shape[1] // LANES):
        r = res[:, j * LANES:(j + 1) * LANES]
        if rotary:
            r = r * cos_ref[...] + pltpu.roll(r, LANES // 2, 1) * sin_ref[...]
        if heads:
            o_ref[0, j] = r.astype(o_ref.dtype)
        else:
            o_ref[:, j * LANES:(j + 1) * LANES] = r.astype(o_ref.dtype)


def _proj(a, w, colscale, *, seq, tm, tn, out_dtype, heads, rope=None, name):
    t, k = a.shape
    n = w.shape[1]
    sblk = seq // tm
    in_specs = [
        pl.BlockSpec((tm, k), lambda m, j: (m, 0)),
        pl.BlockSpec((k, tn), lambda m, j: (0, j)),
        pl.BlockSpec((1, tn), lambda m, j: (0, j)),
    ]
    args = [a, w, colscale]
    if rope is not None:
        in_specs += [pl.BlockSpec((tm, LANES), lambda m, j: (m % sblk, 0))] * 2
        args += list(rope)
    if heads:
        out_shape = jax.ShapeDtypeStruct((t // seq, n // LANES, seq, LANES), out_dtype)
        out_spec = pl.BlockSpec((1, tn // LANES, tm, LANES), lambda m, j: (m // sblk, j, m % sblk, 0))
    else:
        out_shape = jax.ShapeDtypeStruct((t, n), out_dtype)
        out_spec = pl.BlockSpec((tm, tn), lambda m, j: (m, j))
    return pl.pallas_call(
        functools.partial(_proj_kernel, rotary=rope is not None, heads=heads),
        out_shape=out_shape,
        grid=(t // tm, n // tn),
        in_specs=in_specs,
        out_specs=out_spec,
        compiler_params=_params(("parallel", "arbitrary")),
        name=name,
    )(*args)


def _rope_head_order(w):
    half = ROPE_DIM // 2
    k, n = w.shape
    w = w.reshape(k, n // HEAD_DIM, HEAD_DIM)
    w = jnp.concatenate([w[..., :half], w[..., 2 * half:LANES // 2 + half], w[..., half:2 * half],
                         w[..., LANES // 2 + half:]], axis=-1)
    return w.reshape(k, n)


def _rope_tables(seq):
    half = ROPE_DIM // 2
    inv_freq = 1.0 / (ROPE_THETA ** (jnp.arange(half, dtype=F32) / half))
    ang = jnp.arange(seq, dtype=F32)[:, None] * inv_freq[None, :]
    cos, sin = jnp.cos(ang), jnp.sin(ang)
    ones = jnp.ones((seq, LANES // 2 - half), F32)
    zeros = jnp.zeros((seq, LANES // 2 - half), F32)
    cos_t = jnp.concatenate([cos, ones, cos, ones], axis=1)
    sin_t = jnp.concatenate([-sin, zeros, sin, zeros], axis=1)
    return cos_t, sin_t


def _forget_kernel(fl_ref, b_ref, c_ref):
    z = fl_ref[...] + b_ref[...]
    logf = jnp.minimum(z, 0.0) - jnp.log1p(jnp.exp(-jnp.abs(z)))
    row = lax.broadcasted_iota(jnp.int32, (LANES, LANES), 0)
    col = lax.broadcasted_iota(jnp.int32, (LANES, LANES), 1)
    tri = (row <= col).astype(BF16)
    carry = jnp.zeros((logf.shape[0], 1), F32)
    for j in range(logf.shape[1] // LANES):
        xb = logf[:, j * LANES:(j + 1) * LANES]
        hi = xb.astype(BF16)
        r1 = xb - hi.astype(F32)
        mid = r1.astype(BF16)
        lo = (r1 - mid.astype(F32)).astype(BF16)
        cs = (jnp.dot(hi, tri, preferred_element_type=F32) + jnp.dot(mid, tri, preferred_element_type=F32)
              + jnp.dot(lo, tri, preferred_element_type=F32)) + carry
        c_ref[:, j * LANES:(j + 1) * LANES] = cs * LOG2E
        carry = cs[:, LANES - 1:LANES]


def _forget_cumsum(fl_t, b_forget):
    b, h, s = fl_t.shape
    return pl.pallas_call(
        _forget_kernel,
        out_shape=jax.ShapeDtypeStruct((b, h, s), F32),
        grid=(b,),
        in_specs=[pl.BlockSpec((None, h, s), lambda i: (i, 0, 0)), pl.BlockSpec((h, 1), lambda i: (0, 0))],
        out_specs=pl.BlockSpec((None, h, s), lambda i: (i, 0, 0)),
        compiler_params=_params(("parallel",)),
        name="forget_cumsum",
    )(fl_t, b_forget.reshape(h, 1).astype(F32))


def _attn_kernel(*refs, moba, tq, tk):
    if moba:
        q_ref, k_ref, v_ref, o_ref, vaug_ref, km_ref, kaug_ref, qaug_ref = refs
    else:
        q_ref, k_ref, v_ref, c_ref, o_ref, vaug_ref = refs
    d = q_ref.shape[1]
    nkb = q_ref.shape[0] // tk
    nt = (((1,), (1,)), ((), ()))
    row = lax.broadcasted_iota(jnp.int32, (tq, tk), 0)
    col = lax.broadcasted_iota(jnp.int32, (tq, tk), 1)

    vaug_ref[:, :d] = v_ref[...]
    vaug_ref[:, d:] = jnp.ones((v_ref.shape[0], LANES), BF16)
    if moba:
        lane = lax.broadcasted_iota(jnp.int32, (tk, LANES), 1)
        km_ref[...] = jnp.zeros_like(km_ref)
        for n in range(nkb):
            kb = k_ref[n * tk:(n + 1) * tk, :]
            km_ref[n:n + 1, :] = jnp.mean(kb.astype(F32), axis=0, keepdims=True)
            kaug_ref[n * tk:(n + 1) * tk, :d] = kb
            kaug_ref[n * tk:(n + 1) * tk, d:] = (lane == n).astype(BF16)
        s_len = q_ref.shape[0]
        g = lax.dot_general(km_ref[...].astype(BF16), q_ref[...], nt, preferred_element_type=F32)[:SUBLANES]
        blk = lax.broadcasted_iota(jnp.int32, (SUBLANES, s_len), 0)
        past = blk * tk + (tk - 1) < lax.broadcasted_iota(jnp.int32, (SUBLANES, s_len), 1)
        pen = jnp.zeros((SUBLANES, s_len), F32)
        for n in range(nkb - 1):
            g_n = g[n:n + 1, :]
            beats = ((g > g_n) | ((g == g_n) & (blk < n))) & past
            rank = jnp.sum(beats.astype(F32), axis=0, keepdims=True)
            pen = jnp.where((blk == n) & past & (rank >= MOBA_TOPK), MASKED, pen)
        pen = jnp.concatenate([pen, jnp.zeros((LANES - SUBLANES, s_len), F32)], axis=0)
        qaug_ref[:, :d] = q_ref[...]
        qaug_ref[:, d:] = pen.T.astype(BF16)

    def logits(q, n0, n1):
        s = lax.dot_general(q, (kaug_ref if moba else k_ref)[n0 * tk:n1 * tk, :], nt,
                            preferred_element_type=F32)
        return s if moba else s - c_ref[:, n0 * tk:n1 * tk]

    for nb in range(nkb):
        for h in range(tk // tq):
            r0 = nb * tk + h * tq
            q = (qaug_ref if moba else q_ref)[r0:r0 + tq, :]
            s = jnp.where(row + h * tq >= col, logits(q, nb, nb + 1), NEG_INF)
            m = jnp.max(s, axis=1, keepdims=True)
            acc = jnp.dot(jnp.exp2(s - m).astype(BF16), vaug_ref[nb * tk:(nb + 1) * tk, :],
                          preferred_element_type=F32)
            for n0 in range(0, nb, ATTN_PAST_BLOCKS):
                n1 = min(n0 + ATTN_PAST_BLOCKS, nb)
                s = logits(q, n0, n1)
                m_new = jnp.maximum(m, jnp.max(s, axis=1, keepdims=True))
                acc = jnp.exp2(m - m_new) * acc + jnp.dot(jnp.exp2(s - m_new).astype(BF16),
                                                          vaug_ref[n0 * tk:n1 * tk, :],
                                                          preferred_element_type=F32)
                m = m_new
            o_ref[r0:r0 + tq, :] = (acc[:, :d] / acc[:, d:]).astype(o_ref.dtype)


def _attention(q_arr, q_off, k_arr, k_off, v_arr, v_off, c_rows, *, moba, name):
    b, _, s, d = q_arr.shape
    tk = MOBA_BLOCK
    tq = min(ATTN_Q_ROWS, tk)
    assert s % tk == 0 and s // tk <= SUBLANES
    in_specs = [
        pl.BlockSpec((None, None, s, d), lambda bi, h: (bi, q_off + h, 0, 0)),
        pl.BlockSpec((None, None, s, d), lambda bi, h: (bi, k_off + h, 0, 0)),
        pl.BlockSpec((None, None, s, d), lambda bi, h: (bi, v_off + h, 0, 0)),
    ]
    args = [q_arr, k_arr, v_arr]
    scratch = [pltpu.VMEM((s, d + LANES), BF16)]
    if moba:
        scratch += [pltpu.VMEM((LANES, d), F32), pltpu.VMEM((s, d + LANES), BF16), pltpu.VMEM((s, d + LANES), BF16)]
    else:
        in_specs.append(pl.BlockSpec((None, None, 1, s), lambda bi, h: (bi, h, 0, 0)))
        args.append(c_rows)
    return pl.pallas_call(
        functools.partial(_attn_kernel, moba=moba, tq=tq, tk=tk),
        out_shape=jax.ShapeDtypeStruct((b * s, N_HEADS * d), BF16),
        grid=(b, N_HEADS),
        in_specs=in_specs,
        out_specs=pl.BlockSpec((s, d), lambda bi, h: (bi, h)),
        scratch_shapes=scratch,
        compiler_params=_params(("parallel", "parallel")),
        name=name,
    )(*args)


def _mix_kernel(of_ref, om_ref, g_ref, x_ref, wf_ref, wm_ref, wo_ref, gn_ref, wr_ref, br_ref,
                x2_ref, h2_ref, lg_ref):
    d = x_ref.shape[1]
    yf = jnp.dot(of_ref[...], wf_ref[...], preferred_element_type=F32)
    ym = jnp.dot(om_ref[...], wm_ref[...], preferred_element_type=F32)
    mixed = jax.nn.sigmoid(g_ref[:, :d].astype(F32)) * yf + jax.nn.sigmoid(g_ref[:, d:].astype(F32)) * ym
    x2 = x_ref[...] + jnp.dot(mixed.astype(BF16), wo_ref[...], preferred_element_type=F32)
    x2_ref[...] = x2
    ms = jnp.mean(x2 * x2, axis=-1, keepdims=True)
    h2 = x2 * lax.rsqrt(ms + RMS_EPS) * gn_ref[...]
    _store_token_tiles(h2_ref, h2)
    lg_ref[...] = jnp.dot(h2.astype(BF16), wr_ref[...], preferred_element_type=F32) + br_ref[...]


def _mix(o_fox, o_moba, gates, x, w_f, w_m, w_o, g_ffn, w_r, b_r, tm):
    t, d = x.shape
    w = o_fox.shape[1]
    njp = d // (2 * LANES)
    const = lambda i: (0, 0)
    resident = dict(pipeline_mode=pl.Buffered(1))
    return pl.pallas_call(
        _mix_kernel,
        out_shape=(jax.ShapeDtypeStruct((t, d), F32), jax.ShapeDtypeStruct((t * njp, LANES), U32),
                   jax.ShapeDtypeStruct((t, LANES), F32)),
        grid=(t // tm,),
        in_specs=[
            pl.BlockSpec((tm, w), lambda i: (i, 0)),
            pl.BlockSpec((tm, w), lambda i: (i, 0)),
            pl.BlockSpec((tm, 2 * d), lambda i: (i, 0)),
            pl.BlockSpec((tm, d), lambda i: (i, 0)),
            pl.BlockSpec((w, d), const, **resident),
            pl.BlockSpec((w, d), const, **resident),
            pl.BlockSpec((d, d), const, **resident),
            pl.BlockSpec((1, d), const),
            pl.BlockSpec((d, LANES), const),
            pl.BlockSpec((1, LANES), const),
        ],
        out_specs=(pl.BlockSpec((tm, d), lambda i: (i, 0)), pl.BlockSpec((tm * njp, LANES), lambda i: (i, 0)),
                   pl.BlockSpec((tm, LANES), lambda i: (i, 0))),
        compiler_params=_params(("parallel",)),
        name="mix_out_router",
    )(o_fox, o_moba, gates, x, w_f, w_m, w_o, g_ffn, w_r, b_r)


def _router_kernel(lg_ref, meta_ref, gate_ref, cnt_ref, carry_ref):
    @pl.when(pl.program_id(0) == 0)
    def _():
        carry_ref[...] = jnp.zeros_like(carry_ref)

    l = lg_ref[...]
    tr = l.shape[0]
    lane = lax.broadcasted_iota(jnp.int32, (tr, LANES), 1).astype(F32)
    vals, idxs = [], []
    for _ in range(TOP_K):
        m = jnp.max(l, axis=1, keepdims=True)
        ix = jnp.min(jnp.where(l == m, lane, float(LANES)), axis=1, keepdims=True)
        vals.append(m)
        idxs.append(ix)
        l = jnp.where(lane == ix, NEG_INF, l)
    ex = [jnp.exp(v - vals[0]) for v in vals]
    den = ex[0] + ex[1] + ex[2] + ex[3]
    onehot = jnp.zeros((tr, LANES), F32)
    for ix in idxs:
        onehot = onehot + (lane == ix).astype(F32)
    row = lax.broadcasted_iota(jnp.int32, (tr, tr), 0)
    col = lax.broadcasted_iota(jnp.int32, (tr, tr), 1)
    before = (col < row).astype(BF16)
    running = jnp.dot(before, onehot.astype(BF16), preferred_element_type=F32) + carry_ref[...]
    meta = jnp.zeros((tr, LANES), F32)
    gate = jnp.zeros((tr, LANES), F32)
    for k in range(TOP_K):
        pos = jnp.sum(jnp.where(lane == idxs[k], running, 0.0), axis=1, keepdims=True)
        meta = jnp.where(lane == float(k), idxs[k], meta)
        meta = jnp.where(lane == float(TOP_K + k), pos, meta)
        gate = jnp.where(lane == float(k), ex[k] / den, gate)
    meta_ref[...] = meta.astype(jnp.int32)
    gate_ref[...] = gate
    carry_ref[...] += jnp.sum(onehot, axis=0, keepdims=True)
    cnt_ref[...] = carry_ref[...]


def _router(logits, tr):
    t = logits.shape[0]
    return pl.pallas_call(
        _router_kernel,
        out_shape=(jax.ShapeDtypeStruct((t, LANES), jnp.int32), jax.ShapeDtypeStruct((t, LANES), F32),
                   jax.ShapeDtypeStruct((1, LANES), F32)),
        grid=(t // tr,),
        in_specs=[pl.BlockSpec((tr, LANES), lambda i: (i, 0))],
        out_specs=(pl.BlockSpec((tr, LANES), lambda i: (i, 0)), pl.BlockSpec((tr, LANES), lambda i: (i, 0)),
                   pl.BlockSpec((1, LANES), lambda i: (0, 0))),
        scratch_shapes=[pltpu.VMEM((1, LANES), F32)],
        compiler_params=_params(("arbitrary",)),
        name="router_topk",
    )(logits)


def _gather_kernel(nused_ref, tok_ref, tok_next_ref, h_hbm, o_ref, buf, sem, *, njp):
    r = pl.program_id(0)
    m = o_ref.shape[0]
    slot = r % 2

    def issue(tok, s):
        def body(i, c):
            _token_copy(h_hbm, tok[0, 0, 2 * i], buf.at[s], 2 * i, njp, sem.at[s]).start(priority=0)
            _token_copy(h_hbm, tok[0, 0, 2 * i + 1], buf.at[s], 2 * i + 1, njp, sem.at[s]).start(priority=1)
            return c

        lax.fori_loop(0, m // 2, body, 0, unroll=4)

    @pl.when(r == 0)
    def _():
        issue(tok_ref, 0)

    @pl.when(r + 1 < nused_ref[0])
    def _():
        issue(tok_next_ref, 1 - slot)

    @pl.when(r < nused_ref[0])
    def _():
        pltpu.make_async_copy(h_hbm.at[pl.ds(0, m * njp)], buf.at[slot, pl.ds(0, m * njp)], sem.at[slot]).wait()
        for j in range(njp):
            lo, hi = _load_token_cols(buf.at[slot], m, njp, j)
            o_ref[:, j * LANES:(j + 1) * LANES] = lo.astype(o_ref.dtype)
            o_ref[:, (njp + j) * LANES:(njp + j + 1) * LANES] = hi.astype(o_ref.dtype)

    @pl.when(r >= nused_ref[0])
    def _():
        o_ref[...] = jnp.zeros_like(o_ref)


def _gather_rows(h_tiles, row_tok, n_used, m, njp):
    n_blocks = row_tok.shape[0] // m
    tok3 = row_tok.reshape(n_blocks, 1, m)
    return pl.pallas_call(
        functools.partial(_gather_kernel, njp=njp),
        out_shape=jax.ShapeDtypeStruct((n_blocks * m, 2 * njp * LANES), BF16),
        grid_spec=pltpu.PrefetchScalarGridSpec(
            num_scalar_prefetch=1,
            grid=(n_blocks,),
            in_specs=[
                pl.BlockSpec((1, 1, m), lambda r, nu: (r, 0, 0), memory_space=pltpu.SMEM),
                pl.BlockSpec((1, 1, m), lambda r, nu: (jnp.minimum(r + 1, n_blocks - 1), 0, 0),
                             memory_space=pltpu.SMEM),
                pl.BlockSpec(memory_space=pl.ANY),
            ],
            out_specs=pl.BlockSpec((m, 2 * njp * LANES), lambda r, nu: (r, 0)),
            scratch_shapes=[pltpu.VMEM((2, m * _token_pitch(njp), LANES), U32), pltpu.SemaphoreType.DMA((2,))],
        ),
        compiler_params=_params(("arbitrary",), disable_bounds_checks=True),
        name="expert_row_gather",
    )(n_used, tok3, tok3, h_tiles)


def _group_start(be_ref, r):
    return jnp.logical_or(r == 0, be_ref[r] != be_ref[jnp.maximum(r - 1, 0)])


def _gmm1_kernel(be_ref, nne_ref, nused_ref, x_ref, w_hbm, perm_ref, bg_ref, bl_ref, a_ref,
                 stage, wg_s, wl_s, sem):
    j = pl.program_id(0)
    r = pl.program_id(1)
    pw = perm_ref.shape[0]
    tw = stage.shape[1]

    def fetch(jj, e):
        return pltpu.make_async_copy(w_hbm.at[e, :, pl.ds(pl.multiple_of(jj * tw, tw), tw)], stage, sem)

    @pl.when((j == 0) & (r == 0))
    def _():
        fetch(0, be_ref[0]).start()

    @pl.when(r < nused_ref[0])
    def _():
        @pl.when(_group_start(be_ref, r))
        def _():
            fetch(j, be_ref[r]).wait()
            for c in range(tw // pw):
                wb = stage[:, c * pw:(c + 1) * pw].astype(BF16)
                sp = jnp.dot(wb, perm_ref[...], preferred_element_type=F32).astype(BF16)
                wg_s[:, c * (pw // 2):(c + 1) * (pw // 2)] = sp[:, :pw // 2]
                wl_s[:, c * (pw // 2):(c + 1) * (pw // 2)] = sp[:, pw // 2:]
            nxt = nne_ref[be_ref[r]]

            @pl.when(nxt >= 0)
            def _():
                fetch(j, nxt).start()

            @pl.when((nxt < 0) & (j + 1 < pl.num_programs(0)))
            def _():
                fetch(j + 1, be_ref[0]).start()

        x = x_ref[...]
        ug = jnp.dot(x, wg_s[...], preferred_element_type=F32) + bg_ref[...]
        ul = jnp.dot(x, wl_s[...], preferred_element_type=F32) + bl_ref[...]
        glu = jnp.minimum(ug, SWIGLU_LIMIT)
        lin = jnp.clip(ul, -SWIGLU_LIMIT, SWIGLU_LIMIT)
        a_ref[...] = (glu * jax.nn.sigmoid(SWIGLU_ALPHA * glu) * (lin + 1.0)).astype(a_ref.dtype)

    @pl.when(r >= nused_ref[0])
    def _():
        a_ref[...] = jnp.zeros_like(a_ref)


def _deinterleave_perm(width):
    src = np.arange(width)
    dst = np.where(src % 2 == 0, src // 2, width // 2 + src // 2)
    p = np.zeros((width, width), np.float32)
    p[src, dst] = 1.0
    return jnp.asarray(p, BF16)


def _gmm1(xb, w1, b1g, b1l, block_e, next_expert, n_used, m, tn):
    rows, d = xb.shape
    f = w1.shape[2] // 2
    n_blocks = rows // m
    pw = 2 * LANES

    def rc(r, nu):
        return jnp.minimum(r, nu[0] - 1)

    return pl.pallas_call(
        _gmm1_kernel,
        out_shape=jax.ShapeDtypeStruct((rows, f), BF16),
        grid_spec=pltpu.PrefetchScalarGridSpec(
            num_scalar_prefetch=3,
            grid=(f // tn, n_blocks),
            in_specs=[
                pl.BlockSpec((m, d), lambda j, r, be, nne, nu: (rc(r, nu), 0)),
                pl.BlockSpec(memory_space=pl.ANY),
                pl.BlockSpec((pw, pw), lambda j, r, be, nne, nu: (0, 0)),
                pl.BlockSpec((None, 1, tn), lambda j, r, be, nne, nu: (be[rc(r, nu)], 0, j)),
                pl.BlockSpec((None, 1, tn), lambda j, r, be, nne, nu: (be[rc(r, nu)], 0, j)),
            ],
            out_specs=pl.BlockSpec((m, tn), lambda j, r, be, nne, nu: (r, j)),
            scratch_shapes=[pltpu.VMEM((d, 2 * tn), F32), pltpu.VMEM((d, tn), BF16), pltpu.VMEM((d, tn), BF16),
                            pltpu.SemaphoreType.DMA(())],
        ),
        compiler_params=_params(("arbitrary", "arbitrary")),
        name="expert_mlp1_swiglu",
    )(block_e, next_expert, n_used, xb, w1, _deinterleave_perm(pw), b1g, b1l)


def _gmm2_kernel(be_ref, nne_ref, nused_ref, a_ref, w_hbm, b_ref, y_ref, stage, w_s, sem):
    r = pl.program_id(0)

    def fetch(e):
        return pltpu.make_async_copy(w_hbm.at[e], stage, sem)

    @pl.when(r == 0)
    def _():
        fetch(be_ref[0]).start()

    @pl.when(r < nused_ref[0])
    def _():
        @pl.when(_group_start(be_ref, r))
        def _():
            fetch(be_ref[r]).wait()
            w_s[...] = stage[...].astype(BF16)
            nxt = nne_ref[be_ref[r]]

            @pl.when(nxt >= 0)
            def _():
                fetch(nxt).start()

        a = a_ref[...]
        n = a.shape[0]
        half = w_s.shape[1] // 2
        njp = half // LANES
        cw = min(MXU_COLS, half)
        for c in range(half // cw):
            lo_cols = slice(c * cw, (c + 1) * cw)
            hi_cols = slice(half + c * cw, half + (c + 1) * cw)
            y_lo = jnp.dot(a, w_s[:, lo_cols], preferred_element_type=F32) + b_ref[:, lo_cols]
            y_hi = jnp.dot(a, w_s[:, hi_cols], preferred_element_type=F32) + b_ref[:, hi_cols]
            for jj in range(cw // LANES):
                lanes = slice(jj * LANES, (jj + 1) * LANES)
                y_ref[pl.ds(c * (cw // LANES) + jj, n, stride=njp), :] = _pack_pairs(y_lo[:, lanes], y_hi[:, lanes])

    @pl.when(r >= nused_ref[0])
    def _():
        y_ref[...] = jnp.zeros_like(y_ref)


def _gmm2(a, w2, b2, block_e, next_expert, n_used, m):
    rows, f = a.shape
    d = w2.shape[2]
    njp = d // (2 * LANES)
    n_blocks = rows // m

    def rc(r, nu):
        return jnp.minimum(r, nu[0] - 1)

    return pl.pallas_call(
        _gmm2_kernel,
        out_shape=jax.ShapeDtypeStruct((rows * njp, LANES), U32),
        grid_spec=pltpu.PrefetchScalarGridSpec(
            num_scalar_prefetch=3,
            grid=(n_blocks,),
            in_specs=[
                pl.BlockSpec((m, f), lambda r, be, nne, nu: (rc(r, nu), 0)),
                pl.BlockSpec(memory_space=pl.ANY),
                pl.BlockSpec((None, 1, d), lambda r, be, nne, nu: (be[rc(r, nu)], 0, 0)),
            ],
            out_specs=pl.BlockSpec((m * njp, LANES), lambda r, be, nne, nu: (r, 0)),
            scratch_shapes=[pltpu.VMEM((f, d), F32), pltpu.VMEM((f, d), BF16), pltpu.SemaphoreType.DMA(())],
        ),
        compiler_params=_params(("arbitrary",)),
        name="expert_mlp2",
    )(block_e, next_expert, n_used, a, w2, b2)


def _combine_kernel(dest_ref, dest_next_ref, y_hbm, gate_ref, x2_ref, g_ref, o_ref, buf, sem, *, final_norm):
    i = pl.program_id(0)
    tc, d = x2_ref.shape
    njp = d // (2 * LANES)
    slot = i % 2

    def issue(dest, s):
        def body(t, c):
            for k in range(TOP_K):
                _token_copy(y_hbm, dest[0, 0, t * TOP_K + k], buf.at[s, k], t, njp, sem.at[s]).start(priority=k % 2)
            return c

        lax.fori_loop(0, tc, body, 0, unroll=2)

    @pl.when(i == 0)
    def _():
        issue(dest_ref, 0)

    @pl.when(i + 1 < pl.num_programs(0))
    def _():
        issue(dest_next_ref, 1 - slot)

    for k in range(TOP_K):
        pltpu.make_async_copy(y_hbm.at[pl.ds(0, tc * njp)], buf.at[slot, k, pl.ds(0, tc * njp)],
                              sem.at[slot]).wait()
    lo_cols, hi_cols = [], []
    for j in range(njp):
        acc_lo = x2_ref[:, j * LANES:(j + 1) * LANES]
        acc_hi = x2_ref[:, (njp + j) * LANES:(njp + j + 1) * LANES]
        for k in range(TOP_K):
            lo, hi = _load_token_cols(buf.at[slot, k], tc, njp, j)
            g = gate_ref[:, k:k + 1]
            acc_lo = acc_lo + g * lo
            acc_hi = acc_hi + g * hi
        lo_cols.append(acc_lo)
        hi_cols.append(acc_hi)
    acc = jnp.concatenate(lo_cols + hi_cols, axis=1)
    if final_norm:
        ms = jnp.mean(acc * acc, axis=-1, keepdims=True)
        acc = acc * lax.rsqrt(ms + RMS_EPS) * g_ref[...]
    o_ref[...] = acc


def _combine(y_tiles, dest, gates, x2, g_final, tc, final_norm):
    t, d = x2.shape
    njp = d // (2 * LANES)
    nt = t // tc
    dest3 = dest.reshape(nt, 1, tc * TOP_K)
    return pl.pallas_call(
        functools.partial(_combine_kernel, final_norm=final_norm),
        out_shape=jax.ShapeDtypeStruct((t, d), F32),
        grid=(nt,),
        in_specs=[
            pl.BlockSpec((1, 1, tc * TOP_K), lambda i: (i, 0, 0), memory_space=pltpu.SMEM),
            pl.BlockSpec((1, 1, tc * TOP_K), lambda i: (jnp.minimum(i + 1, nt - 1), 0, 0),
                         memory_space=pltpu.SMEM),
            pl.BlockSpec(memory_space=pl.ANY),
            pl.BlockSpec((tc, LANES), lambda i: (i, 0)),
            pl.BlockSpec((tc, d), lambda i: (i, 0)),
            pl.BlockSpec((1, d), lambda i: (0, 0)),
        ],
        out_specs=pl.BlockSpec((tc, d), lambda i: (i, 0)),
        scratch_shapes=[pltpu.VMEM((2, TOP_K, tc * _token_pitch(njp), LANES), U32),
                        pltpu.SemaphoreType.DMA((2,))],
        compiler_params=_params(("arbitrary",), disable_bounds_checks=True),
        name="moe_combine_final_norm",
    )(dest3, dest3, y_tiles, gates, x2, g_final.reshape(1, d))


def _layer(tiles, x, norm_mix_g, w_in, b_forget, w_branch_fox, w_branch_moba, w_out, norm_ffn_g,
           w_router, b_router, w_mlp1, b_mlp1, w_mlp2, b_mlp2):
    b, s, d = x.shape
    t = b * s
    width = N_HEADS * HEAD_DIM
    xt = x.reshape(t, d)
    q_scale = LOG2E * HEAD_DIM ** -0.5

    o = np.cumsum([0, width, width, width, N_HEADS, width, width, width, d, d])
    w_fox = w_in[:, o[0]:o[3]].astype(BF16)
    w_fl = jnp.pad(w_in[:, o[3]:o[4]], ((0, 0), (0, LANES - N_HEADS))).astype(BF16)
    w_mqk = _rope_head_order(w_in[:, o[4]:o[6]].astype(BF16))
    w_mv = w_in[:, o[6]:o[7]].astype(BF16)
    w_g = w_in[:, o[7]:o[9]].astype(BF16)
    colscale = jnp.concatenate([jnp.full((1, width), q_scale, F32), jnp.ones((1, 2 * width), F32)], axis=1)

    proj = functools.partial(_proj, seq=s, tm=tiles.proj_rows)
    h = _rmsnorm(xt, norm_mix_g, tm=tiles.norm_rows)
    fox = proj(h, w_fox, colscale, tn=tiles.proj_cols, out_dtype=BF16, heads=True, name="proj_fox_qkv")
    mqk = proj(h, w_mqk, colscale[:, :2 * width], tn=tiles.proj_cols, out_dtype=BF16, heads=True,
               rope=_rope_tables(s), name="proj_moba_qk")
    mv = proj(h, w_mv, colscale[:, width:2 * width], tn=tiles.proj_cols, out_dtype=BF16, heads=True,
              name="proj_moba_v")
    gates = proj(h, w_g, jnp.ones((1, 2 * d), F32), tn=tiles.proj_cols, out_dtype=BF16, heads=False,
                 name="proj_gates")
    fl = proj(h, w_fl, jnp.ones((1, LANES), F32), tn=LANES, out_dtype=F32, heads=False, name="proj_forget")

    fl_t = fl[:, :N_HEADS].reshape(b, s, N_HEADS).transpose(0, 2, 1)
    c = _forget_cumsum(fl_t, b_forget)
    c_rows = c.reshape(b, N_HEADS, 1, s)

    o_fox = _attention(fox, 0, fox, N_HEADS, fox, 2 * N_HEADS, c_rows, moba=False, name="fox_attention")
    o_moba = _attention(mqk, 0, mqk, N_HEADS, mv, 0, None, moba=True, name="moba_attention")

    w_r = jnp.pad(w_router, ((0, 0), (0, LANES - N_EXPERTS))).astype(BF16)
    b_r = jnp.concatenate([b_router.astype(F32), jnp.full((LANES - N_EXPERTS,), NEG_INF, F32)]).reshape(1, LANES)
    x2, h2, logits = _mix(o_fox, o_moba, gates, xt, w_branch_fox.astype(BF16), w_branch_moba.astype(BF16),
                          w_out.astype(BF16), norm_ffn_g.reshape(1, d), w_r, b_r, tm=tiles.mix_rows)

    meta, gate_w, cnt = _router(logits, tr=tiles.router_rows)

    m = tiles.moe_rows
    assign = t * TOP_K
    n_blocks = (assign + N_EXPERTS * (m - 1) + m - 1) // m
    counts = cnt[0, :N_EXPERTS].astype(jnp.int32)
    padded = (counts + m - 1) // m * m
    pend = jnp.cumsum(padded)
    pstart = pend - padded
    experts = jnp.arange(N_EXPERTS, dtype=jnp.int32)
    e_sel = meta[:, :TOP_K]
    dest = meta[:, TOP_K:2 * TOP_K] + jnp.sum(jnp.where(e_sel[..., None] == experts, pstart, 0), axis=-1)
    n_used = (pend[-1] // m).astype(jnp.int32).reshape(1)
    first_row = jnp.arange(n_blocks, dtype=jnp.int32) * m
    block_e = jnp.minimum(jnp.sum((pend[None, :] <= first_row[:, None]).astype(jnp.int32), axis=1), N_EXPERTS - 1)
    keys = jnp.sort((e_sel * t + jnp.arange(t, dtype=jnp.int32)[:, None]).reshape(-1))
    start = jnp.cumsum(counts) - counts
    of_block = block_e[:, None] == experts[None, :]
    blk_start = jnp.sum(jnp.where(of_block, start, 0), axis=1)
    blk_count = jnp.sum(jnp.where(of_block, counts, 0), axis=1)
    blk_first = first_row - jnp.sum(jnp.where(of_block, pstart, 0), axis=1)
    in_expert = blk_first[:, None] + jnp.arange(m, dtype=jnp.int32)[None, :]
    src = jnp.clip(blk_start[:, None] + in_expert, 0, assign - 1)
    row_tok = jnp.where((in_expert < blk_count[:, None]).reshape(-1), keys[src.reshape(-1)] % t,
                        jnp.arange(n_blocks * m, dtype=jnp.int32) % t).astype(jnp.int32)

    xb = _gather_rows(h2, row_tok, n_used, m, d // (2 * LANES))
    f = w_mlp1.shape[2] // 2
    b1g = b_mlp1[:, 0::2].reshape(N_EXPERTS, 1, f)
    b1l = b_mlp1[:, 1::2].reshape(N_EXPERTS, 1, f)
    later = (experts[None, :] > experts[:, None]) & (counts[None, :] > 0)
    next_expert = jnp.min(jnp.where(later, experts[None, :], N_EXPERTS), axis=1)
    next_expert = jnp.where(next_expert == N_EXPERTS, -1, next_expert).astype(jnp.int32)
    a = _gmm1(xb, w_mlp1, b1g, b1l, block_e, next_expert, n_used, m, tn=tiles.mlp1_cols)
    y = _gmm2(a, w_mlp2, b_mlp2.reshape(N_EXPERTS, 1, d), block_e, next_expert, n_used, m)
    return x2, y, dest, gate_w


def kernel(x, norm_mix_g, w_in, b_forget, w_branch_fox, w_branch_moba, w_out, norm_ffn_g, w_router, b_router,
           w_mlp1, b_mlp1, w_mlp2, b_mlp2, norm_final_g):
    depth = w_in.shape[0]
    b, s, d = x.shape
    tiles = _tiles(b * s, s, w_mlp1.shape[3] // 2)
    for l in range(depth):
        x2, y, dest, gate_w = _layer(tiles, x, norm_mix_g[l], w_in[l], b_forget[l], w_branch_fox[l],
                                     w_branch_moba[l], w_out[l], norm_ffn_g[l], w_router[l], b_router[l],
                                     w_mlp1[l], b_mlp1[l], w_mlp2[l], b_mlp2[l])
        out = _combine(y, dest, gate_w, x2, norm_final_g, tc=tiles.combine_rows, final_norm=l == depth - 1)
        x = out.reshape(b, s, d)
    return x
```

```python
import functools
import math
from typing import NamedTuple

import jax
import jax.numpy as jnp
import numpy as np
from jax import lax
from jax.experimental import pallas as pl
from jax.experimental.pallas import tpu as pltpu

HEAD_DIM = 128
N_HEADS = 8
MOBA_BLOCK = 256
MOBA_TOPK = 3
ROPE_THETA = 500000.0
ROPE_DIM = HEAD_DIM // 4
N_EXPERTS = 32
TOP_K = 4
SWIGLU_ALPHA = 1.702
SWIGLU_LIMIT = 7.0
RMS_EPS = 1e-5

LANES = 128
SUBLANES = 8
MXU_COLS = 256
VMEM_LIMIT_BYTES = 56 * 1024 * 1024
MOE_ROWS = 512
ATTN_Q_ROWS = 256

F32 = jnp.float32
BF16 = jnp.bfloat16
U32 = jnp.uint32
NEG_INF = float("-inf")
MASKED = -1e30
LOG2E = math.log2(math.e)


def _params(sem, **kw):
    return pltpu.CompilerParams(dimension_semantics=sem, vmem_limit_bytes=VMEM_LIMIT_BYTES, **kw)


class _Tiles(NamedTuple):
    norm_rows: int
    proj_rows: int
    proj_cols: int
    mix_rows: int
    router_rows: int
    moe_rows: int
    mlp1_cols: int
    combine_rows: int


def _tiles(t, s, f):
    return _Tiles(norm_rows=min(512, t), proj_rows=min(2048, s), proj_cols=1024, mix_rows=min(256, t),
                  router_rows=min(512, t), moe_rows=MOE_ROWS, mlp1_cols=min(1024, f), combine_rows=min(128, t))


def _pack_pairs(lo, hi):
    lo_b = lax.bitcast_convert_type(lo.astype(BF16).astype(F32), U32) >> 16
    hi_b = lax.bitcast_convert_type(hi.astype(BF16).astype(F32), U32) & jnp.uint32(0xFFFF0000)
    return hi_b | lo_b


def _unpack_pairs(w):
    lo = lax.bitcast_convert_type(w << 16, F32)
    hi = lax.bitcast_convert_type(w & jnp.uint32(0xFFFF0000), F32)
    return lo, hi


def _store_token_tiles(ref, val):
    n, d = val.shape
    njp = d // (2 * LANES)
    for j in range(njp):
        lo = val[:, j * LANES:(j + 1) * LANES]
        hi = val[:, (njp + j) * LANES:(njp + j + 1) * LANES]
        ref[pl.ds(j, n, stride=njp), :] = _pack_pairs(lo, hi)


def _token_pitch(njp):
    p = -(-njp // 4)
    return 4 * (p if p % 2 else p + 1)


def _token_copy(src_hbm, row, dst_vmem, slot, njp, sem):
    return pltpu.make_async_copy(src_hbm.at[pl.ds(pl.multiple_of(row * njp, njp), njp)],
                                 dst_vmem.at[pl.ds(slot * _token_pitch(njp), njp)], sem)


def _load_token_cols(buf, n, njp, j):
    return _unpack_pairs(buf[pl.ds(j, n, stride=_token_pitch(njp)), :])


def _rmsnorm_kernel(x_ref, g_ref, o_ref):
    x = x_ref[...]
    ms = jnp.mean(x * x, axis=-1, keepdims=True)
    o_ref[...] = (x * lax.rsqrt(ms + RMS_EPS) * g_ref[...]).astype(o_ref.dtype)


def _rmsnorm(x, g, tm):
    t, d = x.shape
    return pl.pallas_call(
        _rmsnorm_kernel,
        out_shape=jax.ShapeDtypeStruct((t, d), BF16),
        grid=(t // tm,),
        in_specs=[pl.BlockSpec((tm, d), lambda i: (i, 0)), pl.BlockSpec((1, d), lambda i: (0, 0))],
        out_specs=pl.BlockSpec((tm, d), lambda i: (i, 0)),
        compiler_params=_params(("parallel",)),
        name="rmsnorm_mix",
    )(x, g.reshape(1, d))


def _proj_kernel(a_ref, w_ref, cs_ref, *rest, rotary, heads):
    if rotary:
        cos_ref, sin_ref, o_ref = rest
    else:
        (o_ref,) = rest
    res = jnp.dot(a_ref[...], w_ref[...], preferred_element_type=F32) * cs_ref[...]
    for j in range(res.shape[1] // LANES):
        r = res[:, j * LANES:(j + 1) * LANES]
        if rotary:
            r = r * cos_ref[...] + pltpu.roll(r, LANES // 2, 1) * sin_ref[...]
        if heads:
            o_ref[0, j] = r.astype(o_ref.dtype)
        else:
            o_ref[:, j * LANES:(j + 1) * LANES] = r.astype(o_ref.dtype)


def _proj(a, w, colscale, *, seq, tm, tn, out_dtype, heads, rope=None, name):
    t, k = a.shape
    n = w.shape[1]
    tn = min(tn, n)
    sblk = seq // tm
    in_specs = [
        pl.BlockSpec((tm, k), lambda m, j: (m, 0)),
        pl.BlockSpec((k, tn), lambda m, j: (0, j)),
        pl.BlockSpec((1, tn), lambda m, j: (0, j)),
    ]
    args = [a, w, colscale]
    if rope is not None:
        in_specs += [pl.BlockSpec((tm, LANES), lambda m, j: (m % sblk, 0))] * 2
        args += list(rope)
    if heads:
        out_shape = jax.ShapeDtypeStruct((t // seq, n // LANES, seq, LANES), out_dtype)
        out_spec = pl.BlockSpec((1, tn // LANES, tm, LANES), lambda m, j: (m // sblk, j, m % sblk, 0))
    else:
        out_shape = jax.ShapeDtypeStruct((t, n), out_dtype)
        out_spec = pl.BlockSpec((tm, tn), lambda m, j: (m, j))
    return pl.pallas_call(
        functools.partial(_proj_kernel, rotary=rope is not None, heads=heads),
        out_shape=out_shape,
        grid=(t // tm, n // tn),
        in_specs=in_specs,
        out_specs=out_spec,
        compiler_params=_params(("parallel", "arbitrary")),
        name=name,
    )(*args)


def _rope_head_order(w):
    half = ROPE_DIM // 2
    k, n = w.shape
    w = w.reshape(k, n // HEAD_DIM, HEAD_DIM)
    w = jnp.concatenate([w[..., :half], w[..., 2 * half:LANES // 2 + half], w[..., half:2 * half],
                         w[..., LANES // 2 + half:]], axis=-1)
    return w.reshape(k, n)


def _rope_tables(seq):
    half = ROPE_DIM // 2
    inv_freq = 1.0 / (ROPE_THETA ** (jnp.arange(half, dtype=F32) / half))
    ang = jnp.arange(seq, dtype=F32)[:, None] * inv_freq[None, :]
    cos, sin = jnp.cos(ang), jnp.sin(ang)
    ones = jnp.ones((seq, LANES // 2 - half), F32)
    zeros = jnp.zeros((seq, LANES // 2 - half), F32)
    cos_t = jnp.concatenate([cos, ones, cos, ones], axis=1)
    sin_t = jnp.concatenate([-sin, zeros, sin, zeros], axis=1)
    return cos_t, sin_t


def _forget_kernel(fl_ref, b_ref, c_ref):
    z = fl_ref[...] + b_ref[...]
    logf = jnp.minimum(z, 0.0) - jnp.log1p(jnp.exp(-jnp.abs(z)))
    row = lax.broadcasted_iota(jnp.int32, (LANES, LANES), 0)
    col = lax.broadcasted_iota(jnp.int32, (LANES, LANES), 1)
    tri = (row <= col).astype(BF16)
    carry = jnp.zeros((logf.shape[0], 1), F32)
    for j in range(logf.shape[1] // LANES):
        xb = logf[:, j * LANES:(j + 1) * LANES]
        hi = xb.astype(BF16)
        r1 = xb - hi.astype(F32)
        mid = r1.astype(BF16)
        lo = (r1 - mid.astype(F32)).astype(BF16)
        cs = (jnp.dot(hi, tri, preferred_element_type=F32) + jnp.dot(mid, tri, preferred_element_type=F32)
              + jnp.dot(lo, tri, preferred_element_type=F32)) + carry
        c_ref[:, j * LANES:(j + 1) * LANES] = cs * LOG2E
        carry = cs[:, LANES - 1:LANES]


def _forget_cumsum(fl_t, b_forget):
    b, h, s = fl_t.shape
    return pl.pallas_call(
        _forget_kernel,
        out_shape=jax.ShapeDtypeStruct((b, h, s), F32),
        grid=(b,),
        in_specs=[pl.BlockSpec((None, h, s), lambda i: (i, 0, 0)), pl.BlockSpec((h, 1), lambda i: (0, 0))],
        out_specs=pl.BlockSpec((None, h, s), lambda i: (i, 0, 0)),
        compiler_params=_params(("parallel",)),
        name="forget_cumsum",
    )(fl_t, b_forget.reshape(h, 1).astype(F32))


def _attn_kernel(*refs, moba, tq, tk):
    if moba:
        q_ref, k_ref, v_ref, o_ref, vaug_ref, km_ref, kaug_ref, qaug_ref = refs
    else:
        q_ref, k_ref, v_ref, c_ref, o_ref, vaug_ref = refs
    d = q_ref.shape[1]
    nkb = q_ref.shape[0] // tk
    nt = (((1,), (1,)), ((), ()))
    row = lax.broadcasted_iota(jnp.int32, (tq, tk), 0)
    col = lax.broadcasted_iota(jnp.int32, (tq, tk), 1)

    vaug_ref[:, :d] = v_ref[...]
    vaug_ref[:, d:] = jnp.ones((v_ref.shape[0], LANES), BF16)
    if moba:
        lane = lax.broadcasted_iota(jnp.int32, (tk, LANES), 1)
        km_ref[...] = jnp.zeros_like(km_ref)
        for n in range(nkb):
            kb = k_ref[n * tk:(n + 1) * tk, :]
            km_ref[n:n + 1, :] = jnp.mean(kb.astype(F32), axis=0, keepdims=True)
            kaug_ref[n * tk:(n + 1) * tk, :d] = kb
            kaug_ref[n * tk:(n + 1) * tk, d:] = (lane == n).astype(BF16)
        s_len = q_ref.shape[0]
        g = lax.dot_general(km_ref[...].astype(BF16), q_ref[...], nt, preferred_element_type=F32)[:SUBLANES]
        blk = lax.broadcasted_iota(jnp.int32, (SUBLANES, s_len), 0)
        past = blk * tk + (tk - 1) < lax.broadcasted_iota(jnp.int32, (SUBLANES, s_len), 1)
        pen = jnp.zeros((SUBLANES, s_len), F32)
        for n in range(nkb - 1):
            g_n = g[n:n + 1, :]
            beats = ((g > g_n) | ((g == g_n) & (blk < n))) & past
            rank = jnp.sum(beats.astype(F32), axis=0, keepdims=True)
            pen = jnp.where((blk == n) & past & (rank >= MOBA_TOPK), MASKED, pen)
        pen = jnp.concatenate([pen, jnp.zeros((LANES - SUBLANES, s_len), F32)], axis=0)
        qaug_ref[:, :d] = q_ref[...]
        qaug_ref[:, d:] = pen.T.astype(BF16)

    def logits(q, n):
        s = lax.dot_general(q, (kaug_ref if moba else k_ref)[n * tk:(n + 1) * tk, :], nt,
                            preferred_element_type=F32)
        return s if moba else s - c_ref[:, n * tk:(n + 1) * tk]

    for nb in range(nkb):
        for h in range(tk // tq):
            r0 = nb * tk + h * tq
            q = (qaug_ref if moba else q_ref)[r0:r0 + tq, :]
            s = jnp.where(row + h * tq >= col, logits(q, nb), NEG_INF)
            m = jnp.max(s, axis=1, keepdims=True)
            acc = jnp.dot(jnp.exp2(s - m).astype(BF16), vaug_ref[nb * tk:(nb + 1) * tk, :],
                          preferred_element_type=F32)
            for n in range(nb):
                s = logits(q, n)
                m_new = jnp.maximum(m, jnp.max(s, axis=1, keepdims=True))
                acc = jnp.exp2(m - m_new) * acc + jnp.dot(jnp.exp2(s - m_new).astype(BF16),
                                                          vaug_ref[n * tk:(n + 1) * tk, :],
                                                          preferred_element_type=F32)
                m = m_new
            o_ref[r0:r0 + tq, :] = (acc[:, :d] / acc[:, d:]).astype(o_ref.dtype)


def _attention(q_arr, q_off, k_arr, k_off, v_arr, v_off, c_rows, *, moba, name):
    b, _, s, d = q_arr.shape
    tk = MOBA_BLOCK
    tq = min(ATTN_Q_ROWS, tk)
    assert s % tk == 0 and s // tk <= SUBLANES
    in_specs = [
        pl.BlockSpec((None, None, s, d), lambda bi, h: (bi, q_off + h, 0, 0)),
        pl.BlockSpec((None, None, s, d), lambda bi, h: (bi, k_off + h, 0, 0)),
        pl.BlockSpec((None, None, s, d), lambda bi, h: (bi, v_off + h, 0, 0)),
    ]
    args = [q_arr, k_arr, v_arr]
    scratch = [pltpu.VMEM((s, d + LANES), BF16)]
    if moba:
        scratch += [pltpu.VMEM((LANES, d), F32), pltpu.VMEM((s, d + LANES), BF16), pltpu.VMEM((s, d + LANES), BF16)]
    else:
        in_specs.append(pl.BlockSpec((None, None, 1, s), lambda bi, h: (bi, h, 0, 0)))
        args.append(c_rows)
    return pl.pallas_call(
        functools.partial(_attn_kernel, moba=moba, tq=tq, tk=tk),
        out_shape=jax.ShapeDtypeStruct((b * s, N_HEADS * d), BF16),
        grid=(b, N_HEADS),
        in_specs=in_specs,
        out_specs=pl.BlockSpec((s, d), lambda bi, h: (bi, h)),
        scratch_shapes=scratch,
        compiler_params=_params(("parallel", "parallel")),
        name=name,
    )(*args)


def _mix_kernel(of_ref, om_ref, g_ref, x_ref, wf_ref, wm_ref, wo_ref, gn_ref, wr_ref, br_ref,
                x2_ref, h2_ref, lg_ref):
    d = x_ref.shape[1]
    yf = jnp.dot(of_ref[...], wf_ref[...], preferred_element_type=F32)
    ym = jnp.dot(om_ref[...], wm_ref[...], preferred_element_type=F32)
    mixed = jax.nn.sigmoid(g_ref[:, :d].astype(F32)) * yf + jax.nn.sigmoid(g_ref[:, d:].astype(F32)) * ym
    x2 = x_ref[...] + jnp.dot(mixed.astype(BF16), wo_ref[...], preferred_element_type=F32)
    x2_ref[...] = x2
    ms = jnp.mean(x2 * x2, axis=-1, keepdims=True)
    h2 = x2 * lax.rsqrt(ms + RMS_EPS) * gn_ref[...]
    _store_token_tiles(h2_ref, h2)
    lg_ref[...] = jnp.dot(h2.astype(BF16), wr_ref[...], preferred_element_type=F32) + br_ref[...]


def _mix(o_fox, o_moba, gates, x, w_f, w_m, w_o, g_ffn, w_r, b_r, tm):
    t, d = x.shape
    w = o_fox.shape[1]
    njp = d // (2 * LANES)
    const = lambda i: (0, 0)
    resident = dict(pipeline_mode=pl.Buffered(1))
    return pl.pallas_call(
        _mix_kernel,
        out_shape=(jax.ShapeDtypeStruct((t, d), F32), jax.ShapeDtypeStruct((t * njp, LANES), U32),
                   jax.ShapeDtypeStruct((t, LANES), F32)),
        grid=(t // tm,),
        in_specs=[
            pl.BlockSpec((tm, w), lambda i: (i, 0)),
            pl.BlockSpec((tm, w), lambda i: (i, 0)),
            pl.BlockSpec((tm, 2 * d), lambda i: (i, 0)),
            pl.BlockSpec((tm, d), lambda i: (i, 0)),
            pl.BlockSpec((w, d), const, **resident),
            pl.BlockSpec((w, d), const, **resident),
            pl.BlockSpec((d, d), const, **resident),
            pl.BlockSpec((1, d), const),
            pl.BlockSpec((d, LANES), const),
            pl.BlockSpec((1, LANES), const),
        ],
        out_specs=(pl.BlockSpec((tm, d), lambda i: (i, 0)), pl.BlockSpec((tm * njp, LANES), lambda i: (i, 0)),
                   pl.BlockSpec((tm, LANES), lambda i: (i, 0))),
        compiler_params=_params(("parallel",)),
        name="mix_out_router",
    )(o_fox, o_moba, gates, x, w_f, w_m, w_o, g_ffn, w_r, b_r)


def _router_kernel(lg_ref, meta_ref, gate_ref, cnt_ref, carry_ref):
    @pl.when(pl.program_id(0) == 0)
    def _():
        carry_ref[...] = jnp.zeros_like(carry_ref)

    l = lg_ref[...]
    tr = l.shape[0]
    lane = lax.broadcasted_iota(jnp.int32, (tr, LANES), 1).astype(F32)
    vals, idxs = [], []
    for _ in range(TOP_K):
        m = jnp.max(l, axis=1, keepdims=True)
        ix = jnp.min(jnp.where(l == m, lane, float(LANES)), axis=1, keepdims=True)
        vals.append(m)
        idxs.append(ix)
        l = jnp.where(lane == ix, NEG_INF, l)
    ex = [jnp.exp(v - vals[0]) for v in vals]
    den = ex[0] + ex[1] + ex[2] + ex[3]
    onehot = jnp.zeros((tr, LANES), F32)
    for ix in idxs:
        onehot = onehot + (lane == ix).astype(F32)
    row = lax.broadcasted_iota(jnp.int32, (tr, tr), 0)
    col = lax.broadcasted_iota(jnp.int32, (tr, tr), 1)
    before = (col < row).astype(BF16)
    running = jnp.dot(before, onehot.astype(BF16), preferred_element_type=F32) + carry_ref[...]
    meta = jnp.zeros((tr, LANES), F32)
    gate = jnp.zeros((tr, LANES), F32)
    for k in range(TOP_K):
        pos = jnp.sum(jnp.where(lane == idxs[k], running, 0.0), axis=1, keepdims=True)
        meta = jnp.where(lane == float(k), idxs[k], meta)
        meta = jnp.where(lane == float(TOP_K + k), pos, meta)
        gate = jnp.where(lane == float(k), ex[k] / den, gate)
    meta_ref[...] = meta.astype(jnp.int32)
    gate_ref[...] = gate
    carry_ref[...] += jnp.sum(onehot, axis=0, keepdims=True)
    cnt_ref[...] = carry_ref[...]


def _router(logits, tr):
    t = logits.shape[0]
    return pl.pallas_call(
        _router_kernel,
        out_shape=(jax.ShapeDtypeStruct((t, LANES), jnp.int32), jax.ShapeDtypeStruct((t, LANES), F32),
                   jax.ShapeDtypeStruct((1, LANES), F32)),
        grid=(t // tr,),
        in_specs=[pl.BlockSpec((tr, LANES), lambda i: (i, 0))],
        out_specs=(pl.BlockSpec((tr, LANES), lambda i: (i, 0)), pl.BlockSpec((tr, LANES), lambda i: (i, 0)),
                   pl.BlockSpec((1, LANES), lambda i: (0, 0))),
        scratch_shapes=[pltpu.VMEM((1, LANES), F32)],
        compiler_params=_params(("arbitrary",)),
        name="router_topk",
    )(logits)


def _gather_kernel(nused_ref, tok_ref, tok_next_ref, h_hbm, o_ref, buf, sem, *, njp):
    r = pl.program_id(0)
    m = o_ref.shape[0]
    slot = r % 2

    def issue(tok, s):
        def body(i, c):
            _token_copy(h_hbm, tok[0, 0, 2 * i], buf.at[s], 2 * i, njp, sem.at[s]).start(priority=0)
            _token_copy(h_hbm, tok[0, 0, 2 * i + 1], buf.at[s], 2 * i + 1, njp, sem.at[s]).start(priority=1)
            return c

        lax.fori_loop(0, m // 2, body, 0, unroll=4)

    @pl.when(r == 0)
    def _():
        issue(tok_ref, 0)

    @pl.when(r + 1 < nused_ref[0])
    def _():
        issue(tok_next_ref, 1 - slot)

    @pl.when(r < nused_ref[0])
    def _():
        pltpu.make_async_copy(h_hbm.at[pl.ds(0, m * njp)], buf.at[slot, pl.ds(0, m * njp)], sem.at[slot]).wait()
        for j in range(njp):
            lo, hi = _load_token_cols(buf.at[slot], m, njp, j)
            o_ref[:, j * LANES:(j + 1) * LANES] = lo.astype(o_ref.dtype)
            o_ref[:, (njp + j) * LANES:(njp + j + 1) * LANES] = hi.astype(o_ref.dtype)

    @pl.when(r >= nused_ref[0])
    def _():
        o_ref[...] = jnp.zeros_like(o_ref)


def _gather_rows(h_tiles, row_tok, n_used, m, njp):
    n_blocks = row_tok.shape[0] // m
    tok3 = row_tok.reshape(n_blocks, 1, m)
    return pl.pallas_call(
        functools.partial(_gather_kernel, njp=njp),
        out_shape=jax.ShapeDtypeStruct((n_blocks * m, 2 * njp * LANES), BF16),
        grid_spec=pltpu.PrefetchScalarGridSpec(
            num_scalar_prefetch=1,
            grid=(n_blocks,),
            in_specs=[
                pl.BlockSpec((1, 1, m), lambda r, nu: (r, 0, 0), memory_space=pltpu.SMEM),
                pl.BlockSpec((1, 1, m), lambda r, nu: (jnp.minimum(r + 1, n_blocks - 1), 0, 0),
                             memory_space=pltpu.SMEM),
                pl.BlockSpec(memory_space=pl.ANY),
            ],
            out_specs=pl.BlockSpec((m, 2 * njp * LANES), lambda r, nu: (r, 0)),
            scratch_shapes=[pltpu.VMEM((2, m * _token_pitch(njp), LANES), U32), pltpu.SemaphoreType.DMA((2,))],
        ),
        compiler_params=_params(("arbitrary",), disable_bounds_checks=True),
        name="expert_row_gather",
    )(n_used, tok3, tok3, h_tiles)


def _group_start(be_ref, r):
    return jnp.logical_or(r == 0, be_ref[r] != be_ref[jnp.maximum(r - 1, 0)])


def _gmm1_kernel(be_ref, nne_ref, nused_ref, x_ref, w_hbm, perm_ref, bg_ref, bl_ref, a_ref,
                 stage, wg_s, wl_s, sem):
    j = pl.program_id(0)
    r = pl.program_id(1)
    pw = perm_ref.shape[0]
    tw = stage.shape[1]

    def fetch(jj, e):
        return pltpu.make_async_copy(w_hbm.at[e, :, pl.ds(pl.multiple_of(jj * tw, tw), tw)], stage, sem)

    @pl.when((j == 0) & (r == 0))
    def _():
        fetch(0, be_ref[0]).start()

    @pl.when(r < nused_ref[0])
    def _():
        @pl.when(_group_start(be_ref, r))
        def _():
            fetch(j, be_ref[r]).wait()
            for c in range(tw // pw):
                wb = stage[:, c * pw:(c + 1) * pw].astype(BF16)
                sp = jnp.dot(wb, perm_ref[...], preferred_element_type=F32).astype(BF16)
                wg_s[:, c * (pw // 2):(c + 1) * (pw // 2)] = sp[:, :pw // 2]
                wl_s[:, c * (pw // 2):(c + 1) * (pw // 2)] = sp[:, pw // 2:]
            nxt = nne_ref[be_ref[r]]

            @pl.when(nxt >= 0)
            def _():
                fetch(j, nxt).start()

            @pl.when((nxt < 0) & (j + 1 < pl.num_programs(0)))
            def _():
                fetch(j + 1, be_ref[0]).start()

        x = x_ref[...]
        ug = jnp.dot(x, wg_s[...], preferred_element_type=F32) + bg_ref[...]
        ul = jnp.dot(x, wl_s[...], preferred_element_type=F32) + bl_ref[...]
        glu = jnp.minimum(ug, SWIGLU_LIMIT)
        lin = jnp.clip(ul, -SWIGLU_LIMIT, SWIGLU_LIMIT)
        a_ref[...] = (glu * jax.nn.sigmoid(SWIGLU_ALPHA * glu) * (lin + 1.0)).astype(a_ref.dtype)

    @pl.when(r >= nused_ref[0])
    def _():
        a_ref[...] = jnp.zeros_like(a_ref)


def _deinterleave_perm(width):
    src = np.arange(width)
    dst = np.where(src % 2 == 0, src // 2, width // 2 + src // 2)
    p = np.zeros((width, width), np.float32)
    p[src, dst] = 1.0
    return jnp.asarray(p, BF16)


def _gmm1(xb, w1, b1g, b1l, block_e, next_expert, n_used, m, tn):
    rows, d = xb.shape
    f = w1.shape[2] // 2
    n_blocks = rows // m
    pw = 2 * LANES

    def rc(r, nu):
        return jnp.minimum(r, nu[0] - 1)

    return pl.pallas_call(
        _gmm1_kernel,
        out_shape=jax.ShapeDtypeStruct((rows, f), BF16),
        grid_spec=pltpu.PrefetchScalarGridSpec(
            num_scalar_prefetch=3,
            grid=(f // tn, n_blocks),
            in_specs=[
                pl.BlockSpec((m, d), lambda j, r, be, nne, nu: (rc(r, nu), 0)),
                pl.BlockSpec(memory_space=pl.ANY),
                pl.BlockSpec((pw, pw), lambda j, r, be, nne, nu: (0, 0)),
                pl.BlockSpec((None, 1, tn), lambda j, r, be, nne, nu: (be[rc(r, nu)], 0, j)),
                pl.BlockSpec((None, 1, tn), lambda j, r, be, nne, nu: (be[rc(r, nu)], 0, j)),
            ],
            out_specs=pl.BlockSpec((m, tn), lambda j, r, be, nne, nu: (r, j)),
            scratch_shapes=[pltpu.VMEM((d, 2 * tn), F32), pltpu.VMEM((d, tn), BF16), pltpu.VMEM((d, tn), BF16),
                            pltpu.SemaphoreType.DMA(())],
        ),
        compiler_params=_params(("arbitrary", "arbitrary")),
        name="expert_mlp1_swiglu",
    )(block_e, next_expert, n_used, xb, w1, _deinterleave_perm(pw), b1g, b1l)


def _gmm2_kernel(be_ref, nne_ref, nused_ref, a_ref, w_hbm, b_ref, y_ref, stage, w_s, sem):
    r = pl.program_id(0)

    def fetch(e):
        return pltpu.make_async_copy(w_hbm.at[e], stage, sem)

    @pl.when(r == 0)
    def _():
        fetch(be_ref[0]).start()

    @pl.when(r < nused_ref[0])
    def _():
        @pl.when(_group_start(be_ref, r))
        def _():
            fetch(be_ref[r]).wait()
            w_s[...] = stage[...].astype(BF16)
            nxt = nne_ref[be_ref[r]]

            @pl.when(nxt >= 0)
            def _():
                fetch(nxt).start()

        a = a_ref[...]
        n = a.shape[0]
        half = w_s.shape[1] // 2
        njp = half // LANES
        cw = min(MXU_COLS, half)
        for c in range(half // cw):
            lo_cols = slice(c * cw, (c + 1) * cw)
            hi_cols = slice(half + c * cw, half + (c + 1) * cw)
            y_lo = jnp.dot(a, w_s[:, lo_cols], preferred_element_type=F32) + b_ref[:, lo_cols]
            y_hi = jnp.dot(a, w_s[:, hi_cols], preferred_element_type=F32) + b_ref[:, hi_cols]
            for jj in range(cw // LANES):
                lanes = slice(jj * LANES, (jj + 1) * LANES)
                y_ref[pl.ds(c * (cw // LANES) + jj, n, stride=njp), :] = _pack_pairs(y_lo[:, lanes], y_hi[:, lanes])

    @pl.when(r >= nused_ref[0])
    def _():
        y_ref[...] = jnp.zeros_like(y_ref)


def _gmm2(a, w2, b2, block_e, next_expert, n_used, m):
    rows, f = a.shape
    d = w2.shape[2]
    njp = d // (2 * LANES)
    n_blocks = rows // m

    def rc(r, nu):
        return jnp.minimum(r, nu[0] - 1)

    return pl.pallas_call(
        _gmm2_kernel,
        out_shape=jax.ShapeDtypeStruct((rows * njp, LANES), U32),
        grid_spec=pltpu.PrefetchScalarGridSpec(
            num_scalar_prefetch=3,
            grid=(n_blocks,),
            in_specs=[
                pl.BlockSpec((m, f), lambda r, be, nne, nu: (rc(r, nu), 0)),
                pl.BlockSpec(memory_space=pl.ANY),
                pl.BlockSpec((None, 1, d), lambda r, be, nne, nu: (be[rc(r, nu)], 0, 0)),
            ],
            out_specs=pl.BlockSpec((m * njp, LANES), lambda r, be, nne, nu: (r, 0)),
            scratch_shapes=[pltpu.VMEM((f, d), F32), pltpu.VMEM((f, d), BF16), pltpu.SemaphoreType.DMA(())],
        ),
        compiler_params=_params(("arbitrary",)),
        name="expert_mlp2",
    )(block_e, next_expert, n_used, a, w2, b2)


def _combine_kernel(dest_ref, dest_next_ref, y_hbm, gate_ref, x2_ref, g_ref, o_ref, buf, sem, *, final_norm):
    i = pl.program_id(0)
    tc, d = x2_ref.shape
    njp = d // (2 * LANES)
    slot = i % 2

    def issue(dest, s):
        def body(t, c):
            for k in range(TOP_K):
                _token_copy(y_hbm, dest[0, 0, t * TOP_K + k], buf.at[s, k], t, njp, sem.at[s]).start(priority=k % 2)
            return c

        lax.fori_loop(0, tc, body, 0, unroll=2)

    @pl.when(i == 0)
    def _():
        issue(dest_ref, 0)

    @pl.when(i + 1 < pl.num_programs(0))
    def _():
        issue(dest_next_ref, 1 - slot)

    for k in range(TOP_K):
        pltpu.make_async_copy(y_hbm.at[pl.ds(0, tc * njp)], buf.at[slot, k, pl.ds(0, tc * njp)],
                              sem.at[slot]).wait()
    lo_cols, hi_cols = [], []
    for j in range(njp):
        acc_lo = x2_ref[:, j * LANES:(j + 1) * LANES]
        acc_hi = x2_ref[:, (njp + j) * LANES:(njp + j + 1) * LANES]
        for k in range(TOP_K):
            lo, hi = _load_token_cols(buf.at[slot, k], tc, njp, j)
            g = gate_ref[:, k:k + 1]
            acc_lo = acc_lo + g * lo
            acc_hi = acc_hi + g * hi
        lo_cols.append(acc_lo)
        hi_cols.append(acc_hi)
    acc = jnp.concatenate(lo_cols + hi_cols, axis=1)
    if final_norm:
        ms = jnp.mean(acc * acc, axis=-1, keepdims=True)
        acc = acc * lax.rsqrt(ms + RMS_EPS) * g_ref[...]
    o_ref[...] = acc


def _combine(y_tiles, dest, gates, x2, g_final, tc, final_norm):
    t, d = x2.shape
    njp = d // (2 * LANES)
    nt = t // tc
    dest3 = dest.reshape(nt, 1, tc * TOP_K)
    return pl.pallas_call(
        functools.partial(_combine_kernel, final_norm=final_norm),
        out_shape=jax.ShapeDtypeStruct((t, d), F32),
        grid=(nt,),
        in_specs=[
            pl.BlockSpec((1, 1, tc * TOP_K), lambda i: (i, 0, 0), memory_space=pltpu.SMEM),
            pl.BlockSpec((1, 1, tc * TOP_K), lambda i: (jnp.minimum(i + 1, nt - 1), 0, 0),
                         memory_space=pltpu.SMEM),
            pl.BlockSpec(memory_space=pl.ANY),
            pl.BlockSpec((tc, LANES), lambda i: (i, 0)),
            pl.BlockSpec((tc, d), lambda i: (i, 0)),
            pl.BlockSpec((1, d), lambda i: (0, 0)),
        ],
        out_specs=pl.BlockSpec((tc, d), lambda i: (i, 0)),
        scratch_shapes=[pltpu.VMEM((2, TOP_K, tc * _token_pitch(njp), LANES), U32),
                        pltpu.SemaphoreType.DMA((2,))],
        compiler_params=_params(("arbitrary",), disable_bounds_checks=True),
        name="moe_combine_final_norm",
    )(dest3, dest3, y_tiles, gates, x2, g_final.reshape(1, d))


def _layer(tiles, x, norm_mix_g, w_in, b_forget, w_branch_fox, w_branch_moba, w_out, norm_ffn_g,
           w_router, b_router, w_mlp1, b_mlp1, w_mlp2, b_mlp2):
    b, s, d = x.shape
    t = b * s
    width = N_HEADS * HEAD_DIM
    xt = x.reshape(t, d)
    q_scale = LOG2E * HEAD_DIM ** -0.5

    o = np.cumsum([0, width, width, width, N_HEADS, width, width, width, d, d])
    w_fox = w_in[:, o[0]:o[3]].astype(BF16)
    w_fl = jnp.pad(w_in[:, o[3]:o[4]], ((0, 0), (0, LANES - N_HEADS))).astype(BF16)
    w_mqk = _rope_head_order(w_in[:, o[4]:o[6]].astype(BF16))
    w_mv = w_in[:, o[6]:o[7]].astype(BF16)
    w_g = w_in[:, o[7]:o[9]].astype(BF16)
    colscale = jnp.concatenate([jnp.full((1, width), q_scale, F32), jnp.ones((1, 2 * width), F32)], axis=1)

    proj = functools.partial(_proj, seq=s, tm=tiles.proj_rows)
    h = _rmsnorm(xt, norm_mix_g, tm=tiles.norm_rows)
    fox = proj(h, w_fox, colscale, tn=tiles.proj_cols, out_dtype=BF16, heads=True, name="proj_fox_qkv")
    mqk = proj(h, w_mqk, colscale[:, :2 * width], tn=tiles.proj_cols, out_dtype=BF16, heads=True,
               rope=_rope_tables(s), name="proj_moba_qk")
    mv = proj(h, w_mv, colscale[:, width:2 * width], tn=tiles.proj_cols, out_dtype=BF16, heads=True,
              name="proj_moba_v")
    gates = proj(h, w_g, jnp.ones((1, 2 * d), F32), tn=tiles.proj_cols, out_dtype=BF16, heads=False,
                 name="proj_gates")
    fl = proj(h, w_fl, jnp.ones((1, LANES), F32), tn=LANES, out_dtype=F32, heads=False, name="proj_forget")

    fl_t = fl[:, :N_HEADS].reshape(b, s, N_HEADS).transpose(0, 2, 1)
    c = _forget_cumsum(fl_t, b_forget)
    c_rows = c.reshape(b, N_HEADS, 1, s)

    o_fox = _attention(fox, 0, fox, N_HEADS, fox, 2 * N_HEADS, c_rows, moba=False, name="fox_attention")
    o_moba = _attention(mqk, 0, mqk, N_HEADS, mv, 0, None, moba=True, name="moba_attention")

    w_r = jnp.pad(w_router, ((0, 0), (0, LANES - N_EXPERTS))).astype(BF16)
    b_r = jnp.concatenate([b_router.astype(F32), jnp.full((LANES - N_EXPERTS,), NEG_INF, F32)]).reshape(1, LANES)
    x2, h2, logits = _mix(o_fox, o_moba, gates, xt, w_branch_fox.astype(BF16), w_branch_moba.astype(BF16),
                          w_out.astype(BF16), norm_ffn_g.reshape(1, d), w_r, b_r, tm=tiles.mix_rows)

    meta, gate_w, cnt = _router(logits, tr=tiles.router_rows)

    m = tiles.moe_rows
    assign = t * TOP_K
    n_blocks = (assign + N_EXPERTS * (m - 1) + m - 1) // m
    counts = cnt[0, :N_EXPERTS].astype(jnp.int32)
    padded = (counts + m - 1) // m * m
    pend = jnp.cumsum(padded)
    pstart = pend - padded
    experts = jnp.arange(N_EXPERTS, dtype=jnp.int32)
    e_sel = meta[:, :TOP_K]
    dest = meta[:, TOP_K:2 * TOP_K] + jnp.sum(jnp.where(e_sel[..., None] == experts, pstart, 0), axis=-1)
    n_used = (pend[-1] // m).astype(jnp.int32).reshape(1)
    first_row = jnp.arange(n_blocks, dtype=jnp.int32) * m
    block_e = jnp.minimum(jnp.sum((pend[None, :] <= first_row[:, None]).astype(jnp.int32), axis=1), N_EXPERTS - 1)
    keys = jnp.sort((e_sel * t + jnp.arange(t, dtype=jnp.int32)[:, None]).reshape(-1))
    start = jnp.cumsum(counts) - counts
    of_block = block_e[:, None] == experts[None, :]
    blk_start = jnp.sum(jnp.where(of_block, start, 0), axis=1)
    blk_count = jnp.sum(jnp.where(of_block, counts, 0), axis=1)
    blk_first = first_row - jnp.sum(jnp.where(of_block, pstart, 0), axis=1)
    in_expert = blk_first[:, None] + jnp.arange(m, dtype=jnp.int32)[None, :]
    src = jnp.clip(blk_start[:, None] + in_expert, 0, assign - 1)
    row_tok = jnp.where((in_expert < blk_count[:, None]).reshape(-1), keys[src.reshape(-1)] % t,
                        jnp.arange(n_blocks * m, dtype=jnp.int32) % t).astype(jnp.int32)

    xb = _gather_rows(h2, row_tok, n_used, m, d // (2 * LANES))
    f = w_mlp1.shape[2] // 2
    b1g = b_mlp1[:, 0::2].reshape(N_EXPERTS, 1, f)
    b1l = b_mlp1[:, 1::2].reshape(N_EXPERTS, 1, f)
    later = (experts[None, :] > experts[:, None]) & (counts[None, :] > 0)
    next_expert = jnp.min(jnp.where(later, experts[None, :], N_EXPERTS), axis=1)
    next_expert = jnp.where(next_expert == N_EXPERTS, -1, next_expert).astype(jnp.int32)
    a = _gmm1(xb, w_mlp1, b1g, b1l, block_e, next_expert, n_used, m, tn=tiles.mlp1_cols)
    y = _gmm2(a, w_mlp2, b_mlp2.reshape(N_EXPERTS, 1, d), block_e, next_expert, n_used, m)
    return x2, y, dest, gate_w


def kernel(x, norm_mix_g, w_in, b_forget, w_branch_fox, w_branch_moba, w_out, norm_ffn_g, w_router, b_router,
           w_mlp1, b_mlp1, w_mlp2, b_mlp2, norm_final_g):
    depth = w_in.shape[0]
    b, s, d = x.shape
    tiles = _tiles(b * s, s, w_mlp1.shape[3] // 2)
    for l in range(depth):
        x2, y, dest, gate_w = _layer(tiles, x, norm_mix_g[l], w_in[l], b_forget[l], w_branch_fox[l],
                                     w_branch_moba[l], w_out[l], norm_ffn_g[l], w_router[l], b_router[l],
                                     w_mlp1[l], b_mlp1[l], w_mlp2[l], b_mlp2[l])
        out = _combine(y, dest, gate_w, x2, norm_final_g, tc=tiles.combine_rows, final_norm=l == depth - 1)
        x = out.reshape(b, s, d)
    return x
```

```python
import functools
import math
from typing import NamedTuple

import jax
import jax.numpy as jnp
import numpy as np
from jax import lax
from jax.experimental import pallas as pl
from jax.experimental.pallas import tpu as pltpu

HEAD_DIM = 128
N_HEADS = 8
MOBA_BLOCK = 256
MOBA_TOPK = 3
ROPE_THETA = 500000.0
ROPE_DIM = HEAD_DIM // 4
N_EXPERTS = 32
TOP_K = 4
SWIGLU_ALPHA = 1.702
SWIGLU_LIMIT = 7.0
RMS_EPS = 1e-5

LANES = 128
SUBLANES = 8
MXU_COLS = 256
VMEM_LIMIT_BYTES = 56 * 1024 * 1024
MOE_ROWS = 512
ATTN_Q_ROWS = 256

F32 = jnp.float32
BF16 = jnp.bfloat16
U32 = jnp.uint32
NEG_INF = float("-inf")
MASKED = -1e30
LOG2E = math.log2(math.e)


def _params(sem, **kw):
    return pltpu.CompilerParams(dimension_semantics=sem, vmem_limit_bytes=VMEM_LIMIT_BYTES, **kw)


class _Tiles(NamedTuple):
    norm_rows: int
    proj_rows: int
    proj_cols: int
    mix_rows: int
    router_rows: int
    moe_rows: int
    mlp1_cols: int
    combine_rows: int


def _tiles(t, s, f):
    return _Tiles(norm_rows=min(512, t), proj_rows=min(2048, s), proj_cols=1024, mix_rows=min(256, t),
                  router_rows=min(512, t), moe_rows=MOE_ROWS, mlp1_cols=min(1024, f), combine_rows=min(128, t))


def _pack_pairs(lo, hi):
    lo_b = lax.bitcast_convert_type(lo.astype(BF16).astype(F32), U32) >> 16
    hi_b = lax.bitcast_convert_type(hi.astype(BF16).astype(F32), U32) & jnp.uint32(0xFFFF0000)
    return hi_b | lo_b


def _unpack_pairs(w):
    lo = lax.bitcast_convert_type(w << 16, F32)
    hi = lax.bitcast_convert_type(w & jnp.uint32(0xFFFF0000), F32)
    return lo, hi


def _store_token_tiles(ref, val):
    n, d = val.shape
    njp = d // (2 * LANES)
    for j in range(njp):
        lo = val[:, j * LANES:(j + 1) * LANES]
        hi = val[:, (njp + j) * LANES:(njp + j + 1) * LANES]
        ref[pl.ds(j, n, stride=njp), :] = _pack_pairs(lo, hi)


def _token_pitch(njp):
    p = -(-njp // 4)
    return 4 * (p if p % 2 else p + 1)


def _token_copy(src_hbm, row, dst_vmem, slot, njp, sem):
    return pltpu.make_async_copy(src_hbm.at[pl.ds(pl.multiple_of(row * njp, njp), njp)],
                                 dst_vmem.at[pl.ds(slot * _token_pitch(njp), njp)], sem)


def _load_token_cols(buf, n, njp, j):
    return _unpack_pairs(buf[pl.ds(j, n, stride=_token_pitch(njp)), :])


def _rmsnorm_kernel(x_ref, g_ref, o_ref):
    x = x_ref[...]
    ms = jnp.mean(x * x, axis=-1, keepdims=True)
    o_ref[...] = (x * lax.rsqrt(ms + RMS_EPS) * g_ref[...]).astype(o_ref.dtype)


def _rmsnorm(x, g, tm):
    t, d = x.shape
    return pl.pallas_call(
        _rmsnorm_kernel,
        out_shape=jax.ShapeDtypeStruct((t, d), BF16),
        grid=(t // tm,),
        in_specs=[pl.BlockSpec((tm, d), lambda i: (i, 0)), pl.BlockSpec((1, d), lambda i: (0, 0))],
        out_specs=pl.BlockSpec((tm, d), lambda i: (i, 0)),
        compiler_params=_params(("parallel",)),
        name="rmsnorm_mix",
    )(x, g.reshape(1, d))


def _proj_kernel(a_ref, w_ref, cs_ref, *rest, rotary, heads):
    if rotary:
        cos_ref, sin_ref, o_ref = rest
    else:
        (o_ref,) = rest
    res = jnp.dot(a_ref[...], w_ref[...], preferred_element_type=F32) * cs_ref[...]
    for j in range(res.shape[1] // LANES):
        r = res[:, j * LANES:(j + 1) * LANES]
        if rotary:
            r = r * cos_ref[...] + pltpu.roll(r, LANES // 2, 1) * sin_ref[...]
        if heads:
            o_ref[0, j] = r.astype(o_ref.dtype)
        else:
            o_ref[:, j * LANES:(j + 1) * LANES] = r.astype(o_ref.dtype)


def _proj(a, w, colscale, *, seq, tm, tn, out_dtype, heads, rope=None, name):
    t, k = a.shape
    n = w.shape[1]
    tn = min(tn, n)
    sblk = seq // tm
    in_specs = [
        pl.BlockSpec((tm, k), lambda m, j: (m, 0)),
        pl.BlockSpec((k, tn), lambda m, j: (0, j)),
        pl.BlockSpec((1, tn), lambda m, j: (0, j)),
    ]
    args = [a, w, colscale]
    if rope is not None:
        in_specs += [pl.BlockSpec((tm, LANES), lambda m, j: (m % sblk, 0))] * 2
        args += list(rope)
    if heads:
        out_shape = jax.ShapeDtypeStruct((t // seq, n // LANES, seq, LANES), out_dtype)
        out_spec = pl.BlockSpec((1, tn // LANES, tm, LANES), lambda m, j: (m // sblk, j, m % sblk, 0))
    else:
        out_shape = jax.ShapeDtypeStruct((t, n), out_dtype)
        out_spec = pl.BlockSpec((tm, tn), lambda m, j: (m, j))
    return pl.pallas_call(
        functools.partial(_proj_kernel, rotary=rope is not None, heads=heads),
        out_shape=out_shape,
        grid=(t // tm, n // tn),
        in_specs=in_specs,
        out_specs=out_spec,
        compiler_params=_params(("parallel", "arbitrary")),
        name=name,
    )(*args)


def _rope_head_order(w):
    half = ROPE_DIM // 2
    k, n = w.shape
    w = w.reshape(k, n // HEAD_DIM, HEAD_DIM)
    w = jnp.concatenate([w[..., :half], w[..., 2 * half:LANES // 2 + half], w[..., half:2 * half],
                         w[..., LANES // 2 + half:]], axis=-1)
    return w.reshape(k, n)


def _rope_tables(seq):
    half = ROPE_DIM // 2
    inv_freq = 1.0 / (ROPE_THETA ** (jnp.arange(half, dtype=F32) / half))
    ang = jnp.arange(seq, dtype=F32)[:, None] * inv_freq[None, :]
    cos, sin = jnp.cos(ang), jnp.sin(ang)
    ones = jnp.ones((seq, LANES // 2 - half), F32)
    zeros = jnp.zeros((seq, LANES // 2 - half), F32)
    cos_t = jnp.concatenate([cos, ones, cos, ones], axis=1)
    sin_t = jnp.concatenate([-sin, zeros, sin, zeros], axis=1)
    return cos_t, sin_t


def _forget_kernel(fl_ref, b_ref, c_ref):
    z = fl_ref[...] + b_ref[...]
    logf = jnp.minimum(z, 0.0) - jnp.log1p(jnp.exp(-jnp.abs(z)))
    row = lax.broadcasted_iota(jnp.int32, (LANES, LANES), 0)
    col = lax.broadcasted_iota(jnp.int32, (LANES, LANES), 1)
    tri = (row <= col).astype(BF16)
    carry = jnp.zeros((logf.shape[0], 1), F32)
    for j in range(logf.shape[1] // LANES):
        xb = logf[:, j * LANES:(j + 1) * LANES]
        hi = xb.astype(BF16)
        r1 = xb - hi.astype(F32)
        mid = r1.astype(BF16)
        lo = (r1 - mid.astype(F32)).astype(BF16)
        cs = (jnp.dot(hi, tri, preferred_element_type=F32) + jnp.dot(mid, tri, preferred_element_type=F32)
              + jnp.dot(lo, tri, preferred_element_type=F32)) + carry
        c_ref[:, j * LANES:(j + 1) * LANES] = cs * LOG2E
        carry = cs[:, LANES - 1:LANES]


def _forget_cumsum(fl_t, b_forget):
    b, h, s = fl_t.shape
    return pl.pallas_call(
        _forget_kernel,
        out_shape=jax.ShapeDtypeStruct((b, h, s), F32),
        grid=(b,),
        in_specs=[pl.BlockSpec((None, h, s), lambda i: (i, 0, 0)), pl.BlockSpec((h, 1), lambda i: (0, 0))],
        out_specs=pl.BlockSpec((None, h, s), lambda i: (i, 0, 0)),
        compiler_params=_params(("parallel",)),
        name="forget_cumsum",
    )(fl_t, b_forget.reshape(h, 1).astype(F32))


def _attn_kernel(*refs, moba, tq, tk):
    if moba:
        q_ref, k_ref, v_ref, o_ref, vaug_ref, km_ref, kaug_ref, qaug_ref = refs
    else:
        q_ref, k_ref, v_ref, c_ref, o_ref, vaug_ref = refs
    d = q_ref.shape[1]
    nkb = q_ref.shape[0] // tk
    nt = (((1,), (1,)), ((), ()))
    row = lax.broadcasted_iota(jnp.int32, (tq, tk), 0)
    col = lax.broadcasted_iota(jnp.int32, (tq, tk), 1)

    vaug_ref[:, :d] = v_ref[...]
    vaug_ref[:, d:] = jnp.ones((v_ref.shape[0], LANES), BF16)
    if moba:
        lane = lax.broadcasted_iota(jnp.int32, (tk, LANES), 1)
        km_ref[...] = jnp.zeros_like(km_ref)
        for n in range(nkb):
            kb = k_ref[n * tk:(n + 1) * tk, :]
            km_ref[n:n + 1, :] = jnp.mean(kb.astype(F32), axis=0, keepdims=True)
            kaug_ref[n * tk:(n + 1) * tk, :d] = kb
            kaug_ref[n * tk:(n + 1) * tk, d:] = (lane == n).astype(BF16)
        s_len = q_ref.shape[0]
        g = lax.dot_general(km_ref[...].astype(BF16), q_ref[...], nt, preferred_element_type=F32)[:SUBLANES]
        blk = lax.broadcasted_iota(jnp.int32, (SUBLANES, s_len), 0)
        past = blk * tk + (tk - 1) < lax.broadcasted_iota(jnp.int32, (SUBLANES, s_len), 1)
        pen = jnp.zeros((SUBLANES, s_len), F32)
        for n in range(nkb - 1):
            g_n = g[n:n + 1, :]
            beats = ((g > g_n) | ((g == g_n) & (blk < n))) & past
            rank = jnp.sum(beats.astype(F32), axis=0, keepdims=True)
            pen = jnp.where((blk == n) & past & (rank >= MOBA_TOPK), MASKED, pen)
        pen = jnp.concatenate([pen, jnp.zeros((LANES - SUBLANES, s_len), F32)], axis=0)
        qaug_ref[:, :d] = q_ref[...]
        qaug_ref[:, d:] = pen.T.astype(BF16)

    def logits(q, n):
        s = lax.dot_general(q, (kaug_ref if moba else k_ref)[n * tk:(n + 1) * tk, :], nt,
                            preferred_element_type=F32)
        return s if moba else s - c_ref[:, n * tk:(n + 1) * tk]

    for nb in range(nkb):
        for h in range(tk // tq):
            r0 = nb * tk + h * tq
            q = (qaug_ref if moba else q_ref)[r0:r0 + tq, :]
            s = jnp.where(row + h * tq >= col, logits(q, nb), NEG_INF)
            m = jnp.max(s, axis=1, keepdims=True)
            acc = jnp.dot(jnp.exp2(s - m).astype(BF16), vaug_ref[nb * tk:(nb + 1) * tk, :],
                          preferred_element_type=F32)
            for n in range(nb):
                s = logits(q, n)
                m_new = jnp.maximum(m, jnp.max(s, axis=1, keepdims=True))
                acc = jnp.exp2(m - m_new) * acc + jnp.dot(jnp.exp2(s - m_new).astype(BF16),
                                                          vaug_ref[n * tk:(n + 1) * tk, :],
                                                          preferred_element_type=F32)
                m = m_new
            o_ref[r0:r0 + tq, :] = (acc[:, :d] / acc[:, d:]).astype(o_ref.dtype)


def _attention(q_arr, q_off, k_arr, k_off, v_arr, v_off, c_rows, *, moba, name):
    b, _, s, d = q_arr.shape
    tk = MOBA_BLOCK
    tq = min(ATTN_Q_ROWS, tk)
    assert s % tk == 0 and s // tk <= SUBLANES
    in_specs = [
        pl.BlockSpec((None, None, s, d), lambda bi, h: (bi, q_off + h, 0, 0)),
        pl.BlockSpec((None, None, s, d), lambda bi, h: (bi, k_off + h, 0, 0)),
        pl.BlockSpec((None, None, s, d), lambda bi, h: (bi, v_off + h, 0, 0)),
    ]
    args = [q_arr, k_arr, v_arr]
    scratch = [pltpu.VMEM((s, d + LANES), BF16)]
    if moba:
        scratch += [pltpu.VMEM((LANES, d), F32), pltpu.VMEM((s, d + LANES), BF16), pltpu.VMEM((s, d + LANES), BF16)]
    else:
        in_specs.append(pl.BlockSpec((None, None, 1, s), lambda bi, h: (bi, h, 0, 0)))
        args.append(c_rows)
    return pl.pallas_call(
        functools.partial(_attn_kernel, moba=moba, tq=tq, tk=tk),
        out_shape=jax.ShapeDtypeStruct((b * s, N_HEADS * d), BF16),
        grid=(b, N_HEADS),
        in_specs=in_specs,
        out_specs=pl.BlockSpec((s, d), lambda bi, h: (bi, h)),
        scratch_shapes=scratch,
        compiler_params=_params(("parallel", "parallel")),
        name=name,
    )(*args)


def _mix_kernel(of_ref, om_ref, g_ref, x_ref, wf_ref, wm_ref, wo_ref, gn_ref, wr_ref, br_ref,
                x2_ref, h2_ref, lg_ref):
    d = x_ref.shape[1]
    yf = jnp.dot(of_ref[...], wf_ref[...], preferred_element_type=F32)
    ym = jnp.dot(om_ref[...], wm_ref[...], preferred_element_type=F32)
    mixed = jax.nn.sigmoid(g_ref[:, :d].astype(F32)) * yf + jax.nn.sigmoid(g_ref[:, d:].astype(F32)) * ym
    x2 = x_ref[...] + jnp.dot(mixed.astype(BF16), wo_ref[...], preferred_element_type=F32)
    x2_ref[...] = x2
    ms = jnp.mean(x2 * x2, axis=-1, keepdims=True)
    h2 = x2 * lax.rsqrt(ms + RMS_EPS) * gn_ref[...]
    _store_token_tiles(h2_ref, h2)
    lg_ref[...] = jnp.dot(h2.astype(BF16), wr_ref[...], preferred_element_type=F32) + br_ref[...]


def _mix(o_fox, o_moba, gates, x, w_f, w_m, w_o, g_ffn, w_r, b_r, tm):
    t, d = x.shape
    w = o_fox.shape[1]
    njp = d // (2 * LANES)
    const = lambda i: (0, 0)
    resident = dict(pipeline_mode=pl.Buffered(1))
    return pl.pallas_call(
        _mix_kernel,
        out_shape=(jax.ShapeDtypeStruct((t, d), F32), jax.ShapeDtypeStruct((t * njp, LANES), U32),
                   jax.ShapeDtypeStruct((t, LANES), F32)),
        grid=(t // tm,),
        in_specs=[
            pl.BlockSpec((tm, w), lambda i: (i, 0)),
            pl.BlockSpec((tm, w), lambda i: (i, 0)),
            pl.BlockSpec((tm, 2 * d), lambda i: (i, 0)),
            pl.BlockSpec((tm, d), lambda i: (i, 0)),
            pl.BlockSpec((w, d), const, **resident),
            pl.BlockSpec((w, d), const, **resident),
            pl.BlockSpec((d, d), const, **resident),
            pl.BlockSpec((1, d), const),
            pl.BlockSpec((d, LANES), const),
            pl.BlockSpec((1, LANES), const),
        ],
        out_specs=(pl.BlockSpec((tm, d), lambda i: (i, 0)), pl.BlockSpec((tm * njp, LANES), lambda i: (i, 0)),
                   pl.BlockSpec((tm, LANES), lambda i: (i, 0))),
        compiler_params=_params(("parallel",)),
        name="mix_out_router",
    )(o_fox, o_moba, gates, x, w_f, w_m, w_o, g_ffn, w_r, b_r)


def _router_kernel(lg_ref, meta_ref, gate_ref, cnt_ref, carry_ref):
    @pl.when(pl.program_id(0) == 0)
    def _():
        carry_ref[...] = jnp.zeros_like(carry_ref)

    l = lg_ref[...]
    tr = l.shape[0]
    lane = lax.broadcasted_iota(jnp.int32, (tr, LANES), 1).astype(F32)
    vals, idxs = [], []
    for _ in range(TOP_K):
        m = jnp.max(l, axis=1, keepdims=True)
        ix = jnp.min(jnp.where(l == m, lane, float(LANES)), axis=1, keepdims=True)
        vals.append(m)
        idxs.append(ix)
        l = jnp.where(lane == ix, NEG_INF, l)
    ex = [jnp.exp(v - vals[0]) for v in vals]
    den = ex[0] + ex[1] + ex[2] + ex[3]
    onehot = jnp.zeros((tr, LANES), F32)
    for ix in idxs:
        onehot = onehot + (lane == ix).astype(F32)
    row = lax.broadcasted_iota(jnp.int32, (tr, tr), 0)
    col = lax.broadcasted_iota(jnp.int32, (tr, tr), 1)
    before = (col < row).astype(BF16)
    running = jnp.dot(before, onehot.astype(BF16), preferred_element_type=F32) + carry_ref[...]
    meta = jnp.zeros((tr, LANES), F32)
    gate = jnp.zeros((tr, LANES), F32)
    for k in range(TOP_K):
        pos = jnp.sum(jnp.where(lane == idxs[k], running, 0.0), axis=1, keepdims=True)
        meta = jnp.where(lane == float(k), idxs[k], meta)
        meta = jnp.where(lane == float(TOP_K + k), pos, meta)
        gate = jnp.where(lane == float(k), ex[k] / den, gate)
    meta_ref[...] = meta.astype(jnp.int32)
    gate_ref[...] = gate
    carry_ref[...] += jnp.sum(onehot, axis=0, keepdims=True)
    cnt_ref[...] = carry_ref[...]


def _router(logits, tr):
    t = logits.shape[0]
    return pl.pallas_call(
        _router_kernel,
        out_shape=(jax.ShapeDtypeStruct((t, LANES), jnp.int32), jax.ShapeDtypeStruct((t, LANES), F32),
                   jax.ShapeDtypeStruct((1, LANES), F32)),
        grid=(t // tr,),
        in_specs=[pl.BlockSpec((tr, LANES), lambda i: (i, 0))],
        out_specs=(pl.BlockSpec((tr, LANES), lambda i: (i, 0)), pl.BlockSpec((tr, LANES), lambda i: (i, 0)),
                   pl.BlockSpec((1, LANES), lambda i: (0, 0))),
        scratch_shapes=[pltpu.VMEM((1, LANES), F32)],
        compiler_params=_params(("arbitrary",)),
        name="router_topk",
    )(logits)


def _gather_kernel(nused_ref, tok_ref, tok_next_ref, h_hbm, o_ref, buf, sem, *, njp):
    r = pl.program_id(0)
    m = o_ref.shape[0]
    slot = r % 2

    def issue(tok, s):
        def body(i, c):
            _token_copy(h_hbm, tok[0, 0, 2 * i], buf.at[s], 2 * i, njp, sem.at[s]).start(priority=0)
            _token_copy(h_hbm, tok[0, 0, 2 * i + 1], buf.at[s], 2 * i + 1, njp, sem.at[s]).start(priority=1)
            return c

        lax.fori_loop(0, m // 2, body, 0, unroll=4)

    @pl.when(r == 0)
    def _():
        issue(tok_ref, 0)

    @pl.when(r + 1 < nused_ref[0])
    def _():
        issue(tok_next_ref, 1 - slot)

    @pl.when(r < nused_ref[0])
    def _():
        pltpu.make_async_copy(h_hbm.at[pl.ds(0, m * njp)], buf.at[slot, pl.ds(0, m * njp)], sem.at[slot]).wait()
        for j in range(njp):
            lo, hi = _load_token_cols(buf.at[slot], m, njp, j)
            o_ref[:, j * LANES:(j + 1) * LANES] = lo.astype(o_ref.dtype)
            o_ref[:, (njp + j) * LANES:(njp + j + 1) * LANES] = hi.astype(o_ref.dtype)

    @pl.when(r >= nused_ref[0])
    def _():
        o_ref[...] = jnp.zeros_like(o_ref)


def _gather_rows(h_tiles, row_tok, n_used, m, njp):
    n_blocks = row_tok.shape[0] // m
    tok3 = row_tok.reshape(n_blocks, 1, m)
    return pl.pallas_call(
        functools.partial(_gather_kernel, njp=njp),
        out_shape=jax.ShapeDtypeStruct((n_blocks * m, 2 * njp * LANES), BF16),
        grid_spec=pltpu.PrefetchScalarGridSpec(
            num_scalar_prefetch=1,
            grid=(n_blocks,),
            in_specs=[
                pl.BlockSpec((1, 1, m), lambda r, nu: (r, 0, 0), memory_space=pltpu.SMEM),
                pl.BlockSpec((1, 1, m), lambda r, nu: (jnp.minimum(r + 1, n_blocks - 1), 0, 0),
                             memory_space=pltpu.SMEM),
                pl.BlockSpec(memory_space=pl.ANY),
            ],
            out_specs=pl.BlockSpec((m, 2 * njp * LANES), lambda r, nu: (r, 0)),
            scratch_shapes=[pltpu.VMEM((2, m * _token_pitch(njp), LANES), U32), pltpu.SemaphoreType.DMA((2,))],
        ),
        compiler_params=_params(("arbitrary",), disable_bounds_checks=True),
        name="expert_row_gather",
    )(n_used, tok3, tok3, h_tiles)


def _group_start(be_ref, r):
    return jnp.logical_or(r == 0, be_ref[r] != be_ref[jnp.maximum(r - 1, 0)])


def _gmm1_kernel(be_ref, nne_ref, nvalid_ref, nused_ref, x_ref, w_hbm, perm_ref, bg_ref, bl_ref, a_ref,
                 stage, wg_s, wl_s, sem):
    j = pl.program_id(0)
    r = pl.program_id(1)
    pw = perm_ref.shape[0]
    tw = stage.shape[1]

    def fetch(jj, e):
        return pltpu.make_async_copy(w_hbm.at[e, :, pl.ds(pl.multiple_of(jj * tw, tw), tw)], stage, sem)

    @pl.when((j == 0) & (r == 0))
    def _():
        fetch(0, be_ref[0]).start()

    @pl.when(r < nused_ref[0])
    def _():
        @pl.when(_group_start(be_ref, r))
        def _():
            fetch(j, be_ref[r]).wait()
            for c in range(tw // pw):
                wb = stage[:, c * pw:(c + 1) * pw].astype(BF16)
                sp = jnp.dot(wb, perm_ref[...], preferred_element_type=F32).astype(BF16)
                wg_s[:, c * (pw // 2):(c + 1) * (pw // 2)] = sp[:, :pw // 2]
                wl_s[:, c * (pw // 2):(c + 1) * (pw // 2)] = sp[:, pw // 2:]
            nxt = nne_ref[be_ref[r]]

            @pl.when(nxt >= 0)
            def _():
                fetch(j, nxt).start()

            @pl.when((nxt < 0) & (j + 1 < pl.num_programs(0)))
            def _():
                fetch(j + 1, be_ref[0]).start()

        def compute(rows):
            x = x_ref[:rows, :]
            ug = jnp.dot(x, wg_s[...], preferred_element_type=F32) + bg_ref[...]
            ul = jnp.dot(x, wl_s[...], preferred_element_type=F32) + bl_ref[...]
            glu = jnp.minimum(ug, SWIGLU_LIMIT)
            lin = jnp.clip(ul, -SWIGLU_LIMIT, SWIGLU_LIMIT)
            a_ref[:rows, :] = (glu * jax.nn.sigmoid(SWIGLU_ALPHA * glu) * (lin + 1.0)).astype(a_ref.dtype)
            if rows < m:
                a_ref[rows:, :] = jnp.zeros((m - rows, a_ref.shape[1]), a_ref.dtype)

        m = x_ref.shape[0]
        pl.when(nvalid_ref[r] > m // 2)(functools.partial(compute, m))
        pl.when(nvalid_ref[r] <= m // 2)(functools.partial(compute, m // 2))

    @pl.when(r >= nused_ref[0])
    def _():
        a_ref[...] = jnp.zeros_like(a_ref)


def _deinterleave_perm(width):
    src = np.arange(width)
    dst = np.where(src % 2 == 0, src // 2, width // 2 + src // 2)
    p = np.zeros((width, width), np.float32)
    p[src, dst] = 1.0
    return jnp.asarray(p, BF16)


def _gmm1(xb, w1, b1g, b1l, block_e, next_expert, n_valid, n_used, m, tn):
    rows, d = xb.shape
    f = w1.shape[2] // 2
    n_blocks = rows // m
    pw = 2 * LANES

    def rc(r, nu):
        return jnp.minimum(r, nu[0] - 1)

    return pl.pallas_call(
        _gmm1_kernel,
        out_shape=jax.ShapeDtypeStruct((rows, f), BF16),
        grid_spec=pltpu.PrefetchScalarGridSpec(
            num_scalar_prefetch=4,
            grid=(f // tn, n_blocks),
            in_specs=[
                pl.BlockSpec((m, d), lambda j, r, be, nne, nv, nu: (rc(r, nu), 0)),
                pl.BlockSpec(memory_space=pl.ANY),
                pl.BlockSpec((pw, pw), lambda j, r, be, nne, nv, nu: (0, 0)),
                pl.BlockSpec((None, 1, tn), lambda j, r, be, nne, nv, nu: (be[rc(r, nu)], 0, j)),
                pl.BlockSpec((None, 1, tn), lambda j, r, be, nne, nv, nu: (be[rc(r, nu)], 0, j)),
            ],
            out_specs=pl.BlockSpec((m, tn), lambda j, r, be, nne, nv, nu: (r, j)),
            scratch_shapes=[pltpu.VMEM((d, 2 * tn), F32), pltpu.VMEM((d, tn), BF16), pltpu.VMEM((d, tn), BF16),
                            pltpu.SemaphoreType.DMA(())],
        ),
        compiler_params=_params(("arbitrary", "arbitrary")),
        name="expert_mlp1_swiglu",
    )(block_e, next_expert, n_valid, n_used, xb, w1, _deinterleave_perm(pw), b1g, b1l)


def _gmm2_kernel(be_ref, nne_ref, nvalid_ref, nused_ref, a_ref, w_hbm, b_ref, y_ref, stage, w_s, sem):
    r = pl.program_id(0)

    def fetch(e):
        return pltpu.make_async_copy(w_hbm.at[e], stage, sem)

    @pl.when(r == 0)
    def _():
        fetch(be_ref[0]).start()

    @pl.when(r < nused_ref[0])
    def _():
        @pl.when(_group_start(be_ref, r))
        def _():
            fetch(be_ref[r]).wait()
            w_s[...] = stage[...].astype(BF16)
            nxt = nne_ref[be_ref[r]]

            @pl.when(nxt >= 0)
            def _():
                fetch(nxt).start()

        m = a_ref.shape[0]
        half = w_s.shape[1] // 2
        njp = half // LANES
        cw = min(MXU_COLS, half)

        def compute(rows):
            a = a_ref[:rows, :]
            for c in range(half // cw):
                lo_cols = slice(c * cw, (c + 1) * cw)
                hi_cols = slice(half + c * cw, half + (c + 1) * cw)
                y_lo = jnp.dot(a, w_s[:, lo_cols], preferred_element_type=F32) + b_ref[:, lo_cols]
                y_hi = jnp.dot(a, w_s[:, hi_cols], preferred_element_type=F32) + b_ref[:, hi_cols]
                for jj in range(cw // LANES):
                    lanes = slice(jj * LANES, (jj + 1) * LANES)
                    y_ref[pl.ds(c * (cw // LANES) + jj, rows, stride=njp), :] = _pack_pairs(y_lo[:, lanes],
                                                                                            y_hi[:, lanes])
            if rows < m:
                y_ref[rows * njp:, :] = jnp.zeros(((m - rows) * njp, LANES), y_ref.dtype)

        pl.when(nvalid_ref[r] > m // 2)(functools.partial(compute, m))
        pl.when(nvalid_ref[r] <= m // 2)(functools.partial(compute, m // 2))

    @pl.when(r >= nused_ref[0])
    def _():
        y_ref[...] = jnp.zeros_like(y_ref)


def _gmm2(a, w2, b2, block_e, next_expert, n_valid, n_used, m):
    rows, f = a.shape
    d = w2.shape[2]
    njp = d // (2 * LANES)
    n_blocks = rows // m

    def rc(r, nu):
        return jnp.minimum(r, nu[0] - 1)

    return pl.pallas_call(
        _gmm2_kernel,
        out_shape=jax.ShapeDtypeStruct((rows * njp, LANES), U32),
        grid_spec=pltpu.PrefetchScalarGridSpec(
            num_scalar_prefetch=4,
            grid=(n_blocks,),
            in_specs=[
                pl.BlockSpec((m, f), lambda r, be, nne, nv, nu: (rc(r, nu), 0)),
                pl.BlockSpec(memory_space=pl.ANY),
                pl.BlockSpec((None, 1, d), lambda r, be, nne, nv, nu: (be[rc(r, nu)], 0, 0)),
            ],
            out_specs=pl.BlockSpec((m * njp, LANES), lambda r, be, nne, nv, nu: (r, 0)),
            scratch_shapes=[pltpu.VMEM((f, d), F32), pltpu.VMEM((f, d), BF16), pltpu.SemaphoreType.DMA(())],
        ),
        compiler_params=_params(("arbitrary",)),
        name="expert_mlp2",
    )(block_e, next_expert, n_valid, n_used, a, w2, b2)


def _combine_kernel(dest_ref, dest_next_ref, y_hbm, gate_ref, x2_ref, g_ref, o_ref, buf, sem, *, final_norm):
    i = pl.program_id(0)
    tc, d = x2_ref.shape
    njp = d // (2 * LANES)
    slot = i % 2

    def issue(dest, s):
        def body(t, c):
            for k in range(TOP_K):
                _token_copy(y_hbm, dest[0, 0, t * TOP_K + k], buf.at[s, k], t, njp, sem.at[s]).start(priority=k % 2)
            return c

        lax.fori_loop(0, tc, body, 0, unroll=2)

    @pl.when(i == 0)
    def _():
        issue(dest_ref, 0)

    @pl.when(i + 1 < pl.num_programs(0))
    def _():
        issue(dest_next_ref, 1 - slot)

    for k in range(TOP_K):
        pltpu.make_async_copy(y_hbm.at[pl.ds(0, tc * njp)], buf.at[slot, k, pl.ds(0, tc * njp)],
                              sem.at[slot]).wait()
    lo_cols, hi_cols = [], []
    for j in range(njp):
        acc_lo = x2_ref[:, j * LANES:(j + 1) * LANES]
        acc_hi = x2_ref[:, (njp + j) * LANES:(njp + j + 1) * LANES]
        for k in range(TOP_K):
            lo, hi = _load_token_cols(buf.at[slot, k], tc, njp, j)
            g = gate_ref[:, k:k + 1]
            acc_lo = acc_lo + g * lo
            acc_hi = acc_hi + g * hi
        lo_cols.append(acc_lo)
        hi_cols.append(acc_hi)
    acc = jnp.concatenate(lo_cols + hi_cols, axis=1)
    if final_norm:
        ms = jnp.mean(acc * acc, axis=-1, keepdims=True)
        acc = acc * lax.rsqrt(ms + RMS_EPS) * g_ref[...]
    o_ref[...] = acc


def _combine(y_tiles, dest, gates, x2, g_final, tc, final_norm):
    t, d = x2.shape
    njp = d // (2 * LANES)
    nt = t // tc
    dest3 = dest.reshape(nt, 1, tc * TOP_K)
    return pl.pallas_call(
        functools.partial(_combine_kernel, final_norm=final_norm),
        out_shape=jax.ShapeDtypeStruct((t, d), F32),
        grid=(nt,),
        in_specs=[
            pl.BlockSpec((1, 1, tc * TOP_K), lambda i: (i, 0, 0), memory_space=pltpu.SMEM),
            pl.BlockSpec((1, 1, tc * TOP_K), lambda i: (jnp.minimum(i + 1, nt - 1), 0, 0),
                         memory_space=pltpu.SMEM),
            pl.BlockSpec(memory_space=pl.ANY),
            pl.BlockSpec((tc, LANES), lambda i: (i, 0)),
            pl.BlockSpec((tc, d), lambda i: (i, 0)),
            pl.BlockSpec((1, d), lambda i: (0, 0)),
        ],
        out_specs=pl.BlockSpec((tc, d), lambda i: (i, 0)),
        scratch_shapes=[pltpu.VMEM((2, TOP_K, tc * _token_pitch(njp), LANES), U32),
                        pltpu.SemaphoreType.DMA((2,))],
        compiler_params=_params(("arbitrary",), disable_bounds_checks=True),
        name="moe_combine_final_norm",
    )(dest3, dest3, y_tiles, gates, x2, g_final.reshape(1, d))


def _layer(tiles, x, norm_mix_g, w_in, b_forget, w_branch_fox, w_branch_moba, w_out, norm_ffn_g,
           w_router, b_router, w_mlp1, b_mlp1, w_mlp2, b_mlp2):
    b, s, d = x.shape
    t = b * s
    width = N_HEADS * HEAD_DIM
    xt = x.reshape(t, d)
    q_scale = LOG2E * HEAD_DIM ** -0.5

    o = np.cumsum([0, width, width, width, N_HEADS, width, width, width, d, d])
    w_fox = w_in[:, o[0]:o[3]].astype(BF16)
    w_fl = jnp.pad(w_in[:, o[3]:o[4]], ((0, 0), (0, LANES - N_HEADS))).astype(BF16)
    w_mqk = _rope_head_order(w_in[:, o[4]:o[6]].astype(BF16))
    w_mv = w_in[:, o[6]:o[7]].astype(BF16)
    w_g = w_in[:, o[7]:o[9]].astype(BF16)
    colscale = jnp.concatenate([jnp.full((1, width), q_scale, F32), jnp.ones((1, 2 * width), F32)], axis=1)

    proj = functools.partial(_proj, seq=s, tm=tiles.proj_rows)
    h = _rmsnorm(xt, norm_mix_g, tm=tiles.norm_rows)
    fox = proj(h, w_fox, colscale, tn=tiles.proj_cols, out_dtype=BF16, heads=True, name="proj_fox_qkv")
    mqk = proj(h, w_mqk, colscale[:, :2 * width], tn=tiles.proj_cols, out_dtype=BF16, heads=True,
               rope=_rope_tables(s), name="proj_moba_qk")
    mv = proj(h, w_mv, colscale[:, width:2 * width], tn=tiles.proj_cols, out_dtype=BF16, heads=True,
              name="proj_moba_v")
    gates = proj(h, w_g, jnp.ones((1, 2 * d), F32), tn=tiles.proj_cols, out_dtype=BF16, heads=False,
                 name="proj_gates")
    fl = proj(h, w_fl, jnp.ones((1, LANES), F32), tn=LANES, out_dtype=F32, heads=False, name="proj_forget")

    fl_t = fl[:, :N_HEADS].reshape(b, s, N_HEADS).transpose(0, 2, 1)
    c = _forget_cumsum(fl_t, b_forget)
    c_rows = c.reshape(b, N_HEADS, 1, s)

    o_fox = _attention(fox, 0, fox, N_HEADS, fox, 2 * N_HEADS, c_rows, moba=False, name="fox_attention")
    o_moba = _attention(mqk, 0, mqk, N_HEADS, mv, 0, None, moba=True, name="moba_attention")

    w_r = jnp.pad(w_router, ((0, 0), (0, LANES - N_EXPERTS))).astype(BF16)
    b_r = jnp.concatenate([b_router.astype(F32), jnp.full((LANES - N_EXPERTS,), NEG_INF, F32)]).reshape(1, LANES)
    x2, h2, logits = _mix(o_fox, o_moba, gates, xt, w_branch_fox.astype(BF16), w_branch_moba.astype(BF16),
                          w_out.astype(BF16), norm_ffn_g.reshape(1, d), w_r, b_r, tm=tiles.mix_rows)

    meta, gate_w, cnt = _router(logits, tr=tiles.router_rows)

    m = tiles.moe_rows
    assign = t * TOP_K
    n_blocks = (assign + N_EXPERTS * (m - 1) + m - 1) // m
    counts = cnt[0, :N_EXPERTS].astype(jnp.int32)
    padded = (counts + m - 1) // m * m
    pend = jnp.cumsum(padded)
    pstart = pend - padded
    experts = jnp.arange(N_EXPERTS, dtype=jnp.int32)
    e_sel = meta[:, :TOP_K]
    dest = meta[:, TOP_K:2 * TOP_K] + jnp.sum(jnp.where(e_sel[..., None] == experts, pstart, 0), axis=-1)
    n_used = (pend[-1] // m).astype(jnp.int32).reshape(1)
    first_row = jnp.arange(n_blocks, dtype=jnp.int32) * m
    block_e = jnp.minimum(jnp.sum((pend[None, :] <= first_row[:, None]).astype(jnp.int32), axis=1), N_EXPERTS - 1)
    keys = jnp.sort((e_sel * t + jnp.arange(t, dtype=jnp.int32)[:, None]).reshape(-1))
    start = jnp.cumsum(counts) - counts
    of_block = block_e[:, None] == experts[None, :]
    blk_start = jnp.sum(jnp.where(of_block, start, 0), axis=1)
    blk_count = jnp.sum(jnp.where(of_block, counts, 0), axis=1)
    blk_first = first_row - jnp.sum(jnp.where(of_block, pstart, 0), axis=1)
    in_expert = blk_first[:, None] + jnp.arange(m, dtype=jnp.int32)[None, :]
    src = jnp.clip(blk_start[:, None] + in_expert, 0, assign - 1)
    row_tok = jnp.where((in_expert < blk_count[:, None]).reshape(-1), keys[src.reshape(-1)] % t,
                        jnp.arange(n_blocks * m, dtype=jnp.int32) % t).astype(jnp.int32)

    xb = _gather_rows(h2, row_tok, n_used, m, d // (2 * LANES))
    f = w_mlp1.shape[2] // 2
    b1g = b_mlp1[:, 0::2].reshape(N_EXPERTS, 1, f)
    b1l = b_mlp1[:, 1::2].reshape(N_EXPERTS, 1, f)
    later = (experts[None, :] > experts[:, None]) & (counts[None, :] > 0)
    next_expert = jnp.min(jnp.where(later, experts[None, :], N_EXPERTS), axis=1)
    next_expert = jnp.where(next_expert == N_EXPERTS, -1, next_expert).astype(jnp.int32)
    n_valid = jnp.clip(blk_count - blk_first, 0, m).astype(jnp.int32)
    a = _gmm1(xb, w_mlp1, b1g, b1l, block_e, next_expert, n_valid, n_used, m, tn=tiles.mlp1_cols)
    y = _gmm2(a, w_mlp2, b_mlp2.reshape(N_EXPERTS, 1, d), block_e, next_expert, n_valid, n_used, m)
    return x2, y, dest, gate_w


def kernel(x, norm_mix_g, w_in, b_forget, w_branch_fox, w_branch_moba, w_out, norm_ffn_g, w_router, b_router,
           w_mlp1, b_mlp1, w_mlp2, b_mlp2, norm_final_g):
    depth = w_in.shape[0]
    b, s, d = x.shape
    tiles = _tiles(b * s, s, w_mlp1.shape[3] // 2)
    for l in range(depth):
        x2, y, dest, gate_w = _layer(tiles, x, norm_mix_g[l], w_in[l], b_forget[l], w_branch_fox[l],
                                     w_branch_moba[l], w_out[l], norm_ffn_g[l], w_router[l], b_router[l],
                                     w_mlp1[l], b_mlp1[l], w_mlp2[l], b_mlp2[l])
        out = _combine(y, dest, gate_w, x2, norm_final_g, tc=tiles.combine_rows, final_norm=l == depth - 1)
        x = out.reshape(b, s, d)
    return x
```

```python
import functools
import math
from typing import NamedTuple

import jax
import jax.numpy as jnp
import numpy as np
from jax import lax
from jax.experimental import pallas as pl
from jax.experimental.pallas import tpu as pltpu

HEAD_DIM = 128
N_HEADS = 8
MOBA_BLOCK = 256
MOBA_TOPK = 3
ROPE_THETA = 500000.0
ROPE_DIM = HEAD_DIM // 4
N_EXPERTS = 32
TOP_K = 4
SWIGLU_ALPHA = 1.702
SWIGLU_LIMIT = 7.0
RMS_EPS = 1e-5

LANES = 128
SUBLANES = 8
MXU_COLS = 256
VMEM_LIMIT_BYTES = 56 * 1024 * 1024
MOE_ROWS = 512
ATTN_Q_ROWS = 256

F32 = jnp.float32
BF16 = jnp.bfloat16
U32 = jnp.uint32
NEG_INF = float("-inf")
MASKED = -1e30
LOG2E = math.log2(math.e)


def _params(sem, **kw):
    return pltpu.CompilerParams(dimension_semantics=sem, vmem_limit_bytes=VMEM_LIMIT_BYTES, **kw)


class _Tiles(NamedTuple):
    norm_rows: int
    proj_rows: int
    proj_cols: int
    mix_rows: int
    router_rows: int
    moe_rows: int
    mlp1_cols: int
    combine_rows: int


def _tiles(t, s, f):
    return _Tiles(norm_rows=min(512, t), proj_rows=min(2048, s), proj_cols=1024, mix_rows=min(256, t),
                  router_rows=min(512, t), moe_rows=MOE_ROWS, mlp1_cols=min(1024, f), combine_rows=min(128, t))


def _pack_pairs(lo, hi):
    lo_b = lax.bitcast_convert_type(lo.astype(BF16).astype(F32), U32) >> 16
    hi_b = lax.bitcast_convert_type(hi.astype(BF16).astype(F32), U32) & jnp.uint32(0xFFFF0000)
    return hi_b | lo_b


def _unpack_pairs(w):
    lo = lax.bitcast_convert_type(w << 16, F32)
    hi = lax.bitcast_convert_type(w & jnp.uint32(0xFFFF0000), F32)
    return lo, hi


def _store_token_tiles(ref, val):
    n, d = val.shape
    njp = d // (2 * LANES)
    for j in range(njp):
        lo = val[:, j * LANES:(j + 1) * LANES]
        hi = val[:, (njp + j) * LANES:(njp + j + 1) * LANES]
        ref[pl.ds(j, n, stride=njp), :] = _pack_pairs(lo, hi)


def _token_pitch(njp):
    p = -(-njp // 4)
    return 4 * (p if p % 2 else p + 1)


def _token_copy(src_hbm, row, dst_vmem, slot, njp, sem):
    return pltpu.make_async_copy(src_hbm.at[pl.ds(pl.multiple_of(row * njp, njp), njp)],
                                 dst_vmem.at[pl.ds(slot * _token_pitch(njp), njp)], sem)


def _load_token_cols(buf, n, njp, j):
    return _unpack_pairs(buf[pl.ds(j, n, stride=_token_pitch(njp)), :])


def _rmsnorm_kernel(x_ref, g_ref, wf_ref, o_ref, fl_ref):
    x = x_ref[...]
    ms = jnp.mean(x * x, axis=-1, keepdims=True)
    h = (x * lax.rsqrt(ms + RMS_EPS) * g_ref[...]).astype(o_ref.dtype)
    o_ref[...] = h
    fl_ref[...] = jnp.dot(h, wf_ref[...], preferred_element_type=F32)


def _rmsnorm(x, g, w_fl, tm):
    t, d = x.shape
    return pl.pallas_call(
        _rmsnorm_kernel,
        out_shape=(jax.ShapeDtypeStruct((t, d), BF16), jax.ShapeDtypeStruct((t, LANES), F32)),
        grid=(t // tm,),
        in_specs=[pl.BlockSpec((tm, d), lambda i: (i, 0)), pl.BlockSpec((1, d), lambda i: (0, 0)),
                  pl.BlockSpec((d, LANES), lambda i: (0, 0))],
        out_specs=(pl.BlockSpec((tm, d), lambda i: (i, 0)), pl.BlockSpec((tm, LANES), lambda i: (i, 0))),
        compiler_params=_params(("parallel",)),
        name="rmsnorm_mix",
    )(x, g.reshape(1, d), w_fl)


def _proj_kernel(a_ref, w_ref, cs_ref, *rest, rotary, heads):
    if rotary:
        cos_ref, sin_ref, o_ref = rest
    else:
        (o_ref,) = rest
    res = jnp.dot(a_ref[...], w_ref[...], preferred_element_type=F32) * cs_ref[...]
    for j in range(res.shape[1] // LANES):
        r = res[:, j * LANES:(j + 1) * LANES]
        if rotary:
            r = r * cos_ref[...] + pltpu.roll(r, LANES // 2, 1) * sin_ref[...]
        if heads:
            o_ref[0, j] = r.astype(o_ref.dtype)
        else:
            o_ref[:, j * LANES:(j + 1) * LANES] = r.astype(o_ref.dtype)


def _proj(a, w, colscale, *, seq, tm, tn, out_dtype, heads, rope=None, name):
    t, k = a.shape
    n = w.shape[1]
    tn = min(tn, n)
    sblk = seq // tm
    in_specs = [
        pl.BlockSpec((tm, k), lambda m, j: (m, 0)),
        pl.BlockSpec((k, tn), lambda m, j: (0, j)),
        pl.BlockSpec((1, tn), lambda m, j: (0, j)),
    ]
    args = [a, w, colscale]
    if rope is not None:
        in_specs += [pl.BlockSpec((tm, LANES), lambda m, j: (m % sblk, 0))] * 2
        args += list(rope)
    if heads:
        out_shape = jax.ShapeDtypeStruct((t // seq, n // LANES, seq, LANES), out_dtype)
        out_spec = pl.BlockSpec((1, tn // LANES, tm, LANES), lambda m, j: (m // sblk, j, m % sblk, 0))
    else:
        out_shape = jax.ShapeDtypeStruct((t, n), out_dtype)
        out_spec = pl.BlockSpec((tm, tn), lambda m, j: (m, j))
    return pl.pallas_call(
        functools.partial(_proj_kernel, rotary=rope is not None, heads=heads),
        out_shape=out_shape,
        grid=(t // tm, n // tn),
        in_specs=in_specs,
        out_specs=out_spec,
        compiler_params=_params(("parallel", "arbitrary")),
        name=name,
    )(*args)


def _rope_head_order(w):
    half = ROPE_DIM // 2
    k, n = w.shape
    w = w.reshape(k, n // HEAD_DIM, HEAD_DIM)
    w = jnp.concatenate([w[..., :half], w[..., 2 * half:LANES // 2 + half], w[..., half:2 * half],
                         w[..., LANES // 2 + half:]], axis=-1)
    return w.reshape(k, n)


def _rope_tables(seq):
    half = ROPE_DIM // 2
    inv_freq = 1.0 / (ROPE_THETA ** (jnp.arange(half, dtype=F32) / half))
    ang = jnp.arange(seq, dtype=F32)[:, None] * inv_freq[None, :]
    cos, sin = jnp.cos(ang), jnp.sin(ang)
    ones = jnp.ones((seq, LANES // 2 - half), F32)
    zeros = jnp.zeros((seq, LANES // 2 - half), F32)
    cos_t = jnp.concatenate([cos, ones, cos, ones], axis=1)
    sin_t = jnp.concatenate([-sin, zeros, sin, zeros], axis=1)
    return cos_t, sin_t


def _forget_kernel(fl_ref, b_ref, c_ref):
    z = fl_ref[...] + b_ref[...]
    logf = jnp.minimum(z, 0.0) - jnp.log1p(jnp.exp(-jnp.abs(z)))
    row = lax.broadcasted_iota(jnp.int32, (LANES, LANES), 0)
    col = lax.broadcasted_iota(jnp.int32, (LANES, LANES), 1)
    tri = (row <= col).astype(BF16)
    carry = jnp.zeros((logf.shape[0], 1), F32)
    for j in range(logf.shape[1] // LANES):
        xb = logf[:, j * LANES:(j + 1) * LANES]
        hi = xb.astype(BF16)
        r1 = xb - hi.astype(F32)
        mid = r1.astype(BF16)
        lo = (r1 - mid.astype(F32)).astype(BF16)
        cs = (jnp.dot(hi, tri, preferred_element_type=F32) + jnp.dot(mid, tri, preferred_element_type=F32)
              + jnp.dot(lo, tri, preferred_element_type=F32)) + carry
        c_ref[:, j * LANES:(j + 1) * LANES] = cs * LOG2E
        carry = cs[:, LANES - 1:LANES]


def _forget_cumsum(fl_t, b_forget):
    b, h, s = fl_t.shape
    return pl.pallas_call(
        _forget_kernel,
        out_shape=jax.ShapeDtypeStruct((b, h, s), F32),
        grid=(b,),
        in_specs=[pl.BlockSpec((None, h, s), lambda i: (i, 0, 0)), pl.BlockSpec((h, 1), lambda i: (0, 0))],
        out_specs=pl.BlockSpec((None, h, s), lambda i: (i, 0, 0)),
        compiler_params=_params(("parallel",)),
        name="forget_cumsum",
    )(fl_t, b_forget.reshape(h, 1).astype(F32))


def _attn_kernel(*refs, moba, tq, tk):
    if moba:
        q_ref, k_ref, v_ref, o_ref, vaug_ref, km_ref, kaug_ref, qaug_ref = refs
    else:
        q_ref, k_ref, v_ref, c_ref, o_ref, vaug_ref = refs
    d = q_ref.shape[1]
    nkb = q_ref.shape[0] // tk
    nt = (((1,), (1,)), ((), ()))
    row = lax.broadcasted_iota(jnp.int32, (tq, tk), 0)
    col = lax.broadcasted_iota(jnp.int32, (tq, tk), 1)

    vaug_ref[:, :d] = v_ref[...]
    vaug_ref[:, d:] = jnp.ones((v_ref.shape[0], LANES), BF16)
    if moba:
        lane = lax.broadcasted_iota(jnp.int32, (tk, LANES), 1)
        km_ref[...] = jnp.zeros_like(km_ref)
        for n in range(nkb):
            kb = k_ref[n * tk:(n + 1) * tk, :]
            km_ref[n:n + 1, :] = jnp.mean(kb.astype(F32), axis=0, keepdims=True)
            kaug_ref[n * tk:(n + 1) * tk, :d] = kb
            kaug_ref[n * tk:(n + 1) * tk, d:] = (lane == n).astype(BF16)
        s_len = q_ref.shape[0]
        g = lax.dot_general(km_ref[...].astype(BF16), q_ref[...], nt, preferred_element_type=F32)[:SUBLANES]
        blk = lax.broadcasted_iota(jnp.int32, (SUBLANES, s_len), 0)
        past = blk * tk + (tk - 1) < lax.broadcasted_iota(jnp.int32, (SUBLANES, s_len), 1)
        pen = jnp.zeros((SUBLANES, s_len), F32)
        for n in range(nkb - 1):
            g_n = g[n:n + 1, :]
            beats = ((g > g_n) | ((g == g_n) & (blk < n))) & past
            rank = jnp.sum(beats.astype(F32), axis=0, keepdims=True)
            pen = jnp.where((blk == n) & past & (rank >= MOBA_TOPK), MASKED, pen)
        pen = jnp.concatenate([pen, jnp.zeros((LANES - SUBLANES, s_len), F32)], axis=0)
        qaug_ref[:, :d] = q_ref[...]
        qaug_ref[:, d:] = pen.T.astype(BF16)

    def logits(q, n):
        s = lax.dot_general(q, (kaug_ref if moba else k_ref)[n * tk:(n + 1) * tk, :], nt,
                            preferred_element_type=F32)
        return s if moba else s - c_ref[:, n * tk:(n + 1) * tk]

    for nb in range(nkb):
        for h in range(tk // tq):
            r0 = nb * tk + h * tq
            q = (qaug_ref if moba else q_ref)[r0:r0 + tq, :]
            s = jnp.where(row + h * tq >= col, logits(q, nb), NEG_INF)
            m = jnp.max(s, axis=1, keepdims=True)
            acc = jnp.dot(jnp.exp2(s - m).astype(BF16), vaug_ref[nb * tk:(nb + 1) * tk, :],
                          preferred_element_type=F32)
            for n in range(nb):
                s = logits(q, n)
                m_new = jnp.maximum(m, jnp.max(s, axis=1, keepdims=True))
                acc = jnp.exp2(m - m_new) * acc + jnp.dot(jnp.exp2(s - m_new).astype(BF16),
                                                          vaug_ref[n * tk:(n + 1) * tk, :],
                                                          preferred_element_type=F32)
                m = m_new
            o_ref[r0:r0 + tq, :] = (acc[:, :d] / acc[:, d:]).astype(o_ref.dtype)


def _attention(q_arr, q_off, k_arr, k_off, v_arr, v_off, c_rows, *, moba, name):
    b, _, s, d = q_arr.shape
    tk = MOBA_BLOCK
    tq = min(ATTN_Q_ROWS, tk)
    assert s % tk == 0 and s // tk <= SUBLANES
    in_specs = [
        pl.BlockSpec((None, None, s, d), lambda bi, h: (bi, q_off + h, 0, 0)),
        pl.BlockSpec((None, None, s, d), lambda bi, h: (bi, k_off + h, 0, 0)),
        pl.BlockSpec((None, None, s, d), lambda bi, h: (bi, v_off + h, 0, 0)),
    ]
    args = [q_arr, k_arr, v_arr]
    scratch = [pltpu.VMEM((s, d + LANES), BF16)]
    if moba:
        scratch += [pltpu.VMEM((LANES, d), F32), pltpu.VMEM((s, d + LANES), BF16), pltpu.VMEM((s, d + LANES), BF16)]
    else:
        in_specs.append(pl.BlockSpec((None, None, 1, s), lambda bi, h: (bi, h, 0, 0)))
        args.append(c_rows)
    return pl.pallas_call(
        functools.partial(_attn_kernel, moba=moba, tq=tq, tk=tk),
        out_shape=jax.ShapeDtypeStruct((b * s, N_HEADS * d), BF16),
        grid=(b, N_HEADS),
        in_specs=in_specs,
        out_specs=pl.BlockSpec((s, d), lambda bi, h: (bi, h)),
        scratch_shapes=scratch,
        compiler_params=_params(("parallel", "parallel")),
        name=name,
    )(*args)


def _mix_kernel(of_ref, om_ref, g_ref, x_ref, wf_ref, wm_ref, wo_ref, gn_ref, wr_ref, br_ref,
                x2_ref, h2_ref, lg_ref):
    d = x_ref.shape[1]
    yf = jnp.dot(of_ref[...], wf_ref[...], preferred_element_type=F32)
    ym = jnp.dot(om_ref[...], wm_ref[...], preferred_element_type=F32)
    mixed = jax.nn.sigmoid(g_ref[:, :d].astype(F32)) * yf + jax.nn.sigmoid(g_ref[:, d:].astype(F32)) * ym
    x2 = x_ref[...] + jnp.dot(mixed.astype(BF16), wo_ref[...], preferred_element_type=F32)
    x2_ref[...] = x2
    ms = jnp.mean(x2 * x2, axis=-1, keepdims=True)
    h2 = x2 * lax.rsqrt(ms + RMS_EPS) * gn_ref[...]
    _store_token_tiles(h2_ref, h2)
    lg_ref[...] = jnp.dot(h2.astype(BF16), wr_ref[...], preferred_element_type=F32) + br_ref[...]


def _mix(o_fox, o_moba, gates, x, w_f, w_m, w_o, g_ffn, w_r, b_r, tm):
    t, d = x.shape
    w = o_fox.shape[1]
    njp = d // (2 * LANES)
    const = lambda i: (0, 0)
    resident = dict(pipeline_mode=pl.Buffered(1))
    return pl.pallas_call(
        _mix_kernel,
        out_shape=(jax.ShapeDtypeStruct((t, d), F32), jax.ShapeDtypeStruct((t * njp, LANES), U32),
                   jax.ShapeDtypeStruct((t, LANES), F32)),
        grid=(t // tm,),
        in_specs=[
            pl.BlockSpec((tm, w), lambda i: (i, 0)),
            pl.BlockSpec((tm, w), lambda i: (i, 0)),
            pl.BlockSpec((tm, 2 * d), lambda i: (i, 0)),
            pl.BlockSpec((tm, d), lambda i: (i, 0)),
            pl.BlockSpec((w, d), const, **resident),
            pl.BlockSpec((w, d), const, **resident),
            pl.BlockSpec((d, d), const, **resident),
            pl.BlockSpec((1, d), const),
            pl.BlockSpec((d, LANES), const),
            pl.BlockSpec((1, LANES), const),
        ],
        out_specs=(pl.BlockSpec((tm, d), lambda i: (i, 0)), pl.BlockSpec((tm * njp, LANES), lambda i: (i, 0)),
                   pl.BlockSpec((tm, LANES), lambda i: (i, 0))),
        compiler_params=_params(("parallel",)),
        name="mix_out_router",
    )(o_fox, o_moba, gates, x, w_f, w_m, w_o, g_ffn, w_r, b_r)


def _router_kernel(lg_ref, meta_ref, gate_ref, cnt_ref, carry_ref):
    @pl.when(pl.program_id(0) == 0)
    def _():
        carry_ref[...] = jnp.zeros_like(carry_ref)

    l = lg_ref[...]
    tr = l.shape[0]
    lane = lax.broadcasted_iota(jnp.int32, (tr, LANES), 1).astype(F32)
    vals, idxs = [], []
    for _ in range(TOP_K):
        m = jnp.max(l, axis=1, keepdims=True)
        ix = jnp.min(jnp.where(l == m, lane, float(LANES)), axis=1, keepdims=True)
        vals.append(m)
        idxs.append(ix)
        l = jnp.where(lane == ix, NEG_INF, l)
    ex = [jnp.exp(v - vals[0]) for v in vals]
    den = ex[0] + ex[1] + ex[2] + ex[3]
    onehot = jnp.zeros((tr, LANES), F32)
    for ix in idxs:
        onehot = onehot + (lane == ix).astype(F32)
    row = lax.broadcasted_iota(jnp.int32, (tr, tr), 0)
    col = lax.broadcasted_iota(jnp.int32, (tr, tr), 1)
    before = (col < row).astype(BF16)
    running = jnp.dot(before, onehot.astype(BF16), preferred_element_type=F32) + carry_ref[...]
    meta = jnp.zeros((tr, LANES), F32)
    gate = jnp.zeros((tr, LANES), F32)
    for k in range(TOP_K):
        pos = jnp.sum(jnp.where(lane == idxs[k], running, 0.0), axis=1, keepdims=True)
        meta = jnp.where(lane == float(k), idxs[k], meta)
        meta = jnp.where(lane == float(TOP_K + k), pos, meta)
        gate = jnp.where(lane == float(k), ex[k] / den, gate)
    meta_ref[...] = meta.astype(jnp.int32)
    gate_ref[...] = gate
    carry_ref[...] += jnp.sum(onehot, axis=0, keepdims=True)
    cnt_ref[...] = carry_ref[...]


def _router(logits, tr):
    t = logits.shape[0]
    return pl.pallas_call(
        _router_kernel,
        out_shape=(jax.ShapeDtypeStruct((t, LANES), jnp.int32), jax.ShapeDtypeStruct((t, LANES), F32),
                   jax.ShapeDtypeStruct((1, LANES), F32)),
        grid=(t // tr,),
        in_specs=[pl.BlockSpec((tr, LANES), lambda i: (i, 0))],
        out_specs=(pl.BlockSpec((tr, LANES), lambda i: (i, 0)), pl.BlockSpec((tr, LANES), lambda i: (i, 0)),
                   pl.BlockSpec((1, LANES), lambda i: (0, 0))),
        scratch_shapes=[pltpu.VMEM((1, LANES), F32)],
        compiler_params=_params(("arbitrary",)),
        name="router_topk",
    )(logits)


def _gather_kernel(nused_ref, tok_ref, tok_next_ref, h_hbm, o_ref, buf, sem, *, njp):
    r = pl.program_id(0)
    m = o_ref.shape[0]
    slot = r % 2

    def issue(tok, s):
        def body(i, c):
            _token_copy(h_hbm, tok[0, 0, 2 * i], buf.at[s], 2 * i, njp, sem.at[s]).start(priority=0)
            _token_copy(h_hbm, tok[0, 0, 2 * i + 1], buf.at[s], 2 * i + 1, njp, sem.at[s]).start(priority=1)
            return c

        lax.fori_loop(0, m // 2, body, 0, unroll=4)

    @pl.when(r == 0)
    def _():
        issue(tok_ref, 0)

    @pl.when(r + 1 < nused_ref[0])
    def _():
        issue(tok_next_ref, 1 - slot)

    @pl.when(r < nused_ref[0])
    def _():
        pltpu.make_async_copy(h_hbm.at[pl.ds(0, m * njp)], buf.at[slot, pl.ds(0, m * njp)], sem.at[slot]).wait()
        for j in range(njp):
            lo, hi = _load_token_cols(buf.at[slot], m, njp, j)
            o_ref[:, j * LANES:(j + 1) * LANES] = lo.astype(o_ref.dtype)
            o_ref[:, (njp + j) * LANES:(njp + j + 1) * LANES] = hi.astype(o_ref.dtype)

    @pl.when(r >= nused_ref[0])
    def _():
        o_ref[...] = jnp.zeros_like(o_ref)


def _gather_rows(h_tiles, row_tok, n_used, m, njp):
    n_blocks = row_tok.shape[0] // m
    tok3 = row_tok.reshape(n_blocks, 1, m)
    return pl.pallas_call(
        functools.partial(_gather_kernel, njp=njp),
        out_shape=jax.ShapeDtypeStruct((n_blocks * m, 2 * njp * LANES), BF16),
        grid_spec=pltpu.PrefetchScalarGridSpec(
            num_scalar_prefetch=1,
            grid=(n_blocks,),
            in_specs=[
                pl.BlockSpec((1, 1, m), lambda r, nu: (r, 0, 0), memory_space=pltpu.SMEM),
                pl.BlockSpec((1, 1, m), lambda r, nu: (jnp.minimum(r + 1, n_blocks - 1), 0, 0),
                             memory_space=pltpu.SMEM),
                pl.BlockSpec(memory_space=pl.ANY),
            ],
            out_specs=pl.BlockSpec((m, 2 * njp * LANES), lambda r, nu: (r, 0)),
            scratch_shapes=[pltpu.VMEM((2, m * _token_pitch(njp), LANES), U32), pltpu.SemaphoreType.DMA((2,))],
        ),
        compiler_params=_params(("arbitrary",), disable_bounds_checks=True),
        name="expert_row_gather",
    )(n_used, tok3, tok3, h_tiles)


def _group_start(be_ref, r):
    return jnp.logical_or(r == 0, be_ref[r] != be_ref[jnp.maximum(r - 1, 0)])


def _gmm1_kernel(be_ref, nne_ref, nused_ref, x_ref, w_hbm, perm_ref, bg_ref, bl_ref, a_ref,
                 stage, wg_s, wl_s, sem):
    j = pl.program_id(0)
    r = pl.program_id(1)
    pw = perm_ref.shape[0]
    tw = stage.shape[1]

    def fetch(jj, e):
        return pltpu.make_async_copy(w_hbm.at[e, :, pl.ds(pl.multiple_of(jj * tw, tw), tw)], stage, sem)

    @pl.when((j == 0) & (r == 0))
    def _():
        fetch(0, be_ref[0]).start()

    @pl.when(r < nused_ref[0])
    def _():
        @pl.when(_group_start(be_ref, r))
        def _():
            fetch(j, be_ref[r]).wait()
            for c in range(tw // pw):
                wb = stage[:, c * pw:(c + 1) * pw].astype(BF16)
                sp = jnp.dot(wb, perm_ref[...], preferred_element_type=F32).astype(BF16)
                wg_s[:, c * (pw // 2):(c + 1) * (pw // 2)] = sp[:, :pw // 2]
                wl_s[:, c * (pw // 2):(c + 1) * (pw // 2)] = sp[:, pw // 2:]
            nxt = nne_ref[be_ref[r]]

            @pl.when(nxt >= 0)
            def _():
                fetch(j, nxt).start()

            @pl.when((nxt < 0) & (j + 1 < pl.num_programs(0)))
            def _():
                fetch(j + 1, be_ref[0]).start()

        x = x_ref[...]
        ug = jnp.dot(x, wg_s[...], preferred_element_type=F32) + bg_ref[...]
        ul = jnp.dot(x, wl_s[...], preferred_element_type=F32) + bl_ref[...]
        glu = jnp.minimum(ug, SWIGLU_LIMIT)
        lin = jnp.clip(ul, -SWIGLU_LIMIT, SWIGLU_LIMIT)
        a_ref[...] = (glu * jax.nn.sigmoid(SWIGLU_ALPHA * glu) * (lin + 1.0)).astype(a_ref.dtype)

    @pl.when(r >= nused_ref[0])
    def _():
        a_ref[...] = jnp.zeros_like(a_ref)


def _deinterleave_perm(width):
    src = np.arange(width)
    dst = np.where(src % 2 == 0, src // 2, width // 2 + src // 2)
    p = np.zeros((width, width), np.float32)
    p[src, dst] = 1.0
    return jnp.asarray(p, BF16)


def _gmm1(xb, w1, b1g, b1l, block_e, next_expert, n_used, m, tn):
    rows, d = xb.shape
    f = w1.shape[2] // 2
    n_blocks = rows // m
    pw = 2 * LANES

    def rc(r, nu):
        return jnp.minimum(r, nu[0] - 1)

    return pl.pallas_call(
        _gmm1_kernel,
        out_shape=jax.ShapeDtypeStruct((rows, f), BF16),
        grid_spec=pltpu.PrefetchScalarGridSpec(
            num_scalar_prefetch=3,
            grid=(f // tn, n_blocks),
            in_specs=[
                pl.BlockSpec((m, d), lambda j, r, be, nne, nu: (rc(r, nu), 0)),
                pl.BlockSpec(memory_space=pl.ANY),
                pl.BlockSpec((pw, pw), lambda j, r, be, nne, nu: (0, 0)),
                pl.BlockSpec((None, 1, tn), lambda j, r, be, nne, nu: (be[rc(r, nu)], 0, j)),
                pl.BlockSpec((None, 1, tn), lambda j, r, be, nne, nu: (be[rc(r, nu)], 0, j)),
            ],
            out_specs=pl.BlockSpec((m, tn), lambda j, r, be, nne, nu: (r, j)),
            scratch_shapes=[pltpu.VMEM((d, 2 * tn), F32), pltpu.VMEM((d, tn), BF16), pltpu.VMEM((d, tn), BF16),
                            pltpu.SemaphoreType.DMA(())],
        ),
        compiler_params=_params(("arbitrary", "arbitrary")),
        name="expert_mlp1_swiglu",
    )(block_e, next_expert, n_used, xb, w1, _deinterleave_perm(pw), b1g, b1l)


def _gmm2_kernel(be_ref, nne_ref, nused_ref, a_ref, w_hbm, b_ref, y_ref, stage, w_s, sem):
    r = pl.program_id(0)

    def fetch(e):
        return pltpu.make_async_copy(w_hbm.at[e], stage, sem)

    @pl.when(r == 0)
    def _():
        fetch(be_ref[0]).start()

    @pl.when(r < nused_ref[0])
    def _():
        @pl.when(_group_start(be_ref, r))
        def _():
            fetch(be_ref[r]).wait()
            w_s[...] = stage[...].astype(BF16)
            nxt = nne_ref[be_ref[r]]

            @pl.when(nxt >= 0)
            def _():
                fetch(nxt).start()

        a = a_ref[...]
        n = a.shape[0]
        half = w_s.shape[1] // 2
        njp = half // LANES
        cw = min(MXU_COLS, half)
        for c in range(half // cw):
            lo_cols = slice(c * cw, (c + 1) * cw)
            hi_cols = slice(half + c * cw, half + (c + 1) * cw)
            y_lo = jnp.dot(a, w_s[:, lo_cols], preferred_element_type=F32) + b_ref[:, lo_cols]
            y_hi = jnp.dot(a, w_s[:, hi_cols], preferred_element_type=F32) + b_ref[:, hi_cols]
            for jj in range(cw // LANES):
                lanes = slice(jj * LANES, (jj + 1) * LANES)
                y_ref[pl.ds(c * (cw // LANES) + jj, n, stride=njp), :] = _pack_pairs(y_lo[:, lanes], y_hi[:, lanes])

    @pl.when(r >= nused_ref[0])
    def _():
        y_ref[...] = jnp.zeros_like(y_ref)


def _gmm2(a, w2, b2, block_e, next_expert, n_used, m):
    rows, f = a.shape
    d = w2.shape[2]
    njp = d // (2 * LANES)
    n_blocks = rows // m

    def rc(r, nu):
        return jnp.minimum(r, nu[0] - 1)

    return pl.pallas_call(
        _gmm2_kernel,
        out_shape=jax.ShapeDtypeStruct((rows * njp, LANES), U32),
        grid_spec=pltpu.PrefetchScalarGridSpec(
            num_scalar_prefetch=3,
            grid=(n_blocks,),
            in_specs=[
                pl.BlockSpec((m, f), lambda r, be, nne, nu: (rc(r, nu), 0)),
                pl.BlockSpec(memory_space=pl.ANY),
                pl.BlockSpec((None, 1, d), lambda r, be, nne, nu: (be[rc(r, nu)], 0, 0)),
            ],
            out_specs=pl.BlockSpec((m * njp, LANES), lambda r, be, nne, nu: (r, 0)),
            scratch_shapes=[pltpu.VMEM((f, d), F32), pltpu.VMEM((f, d), BF16), pltpu.SemaphoreType.DMA(())],
        ),
        compiler_params=_params(("arbitrary",)),
        name="expert_mlp2",
    )(block_e, next_expert, n_used, a, w2, b2)


def _combine_kernel(dest_ref, dest_next_ref, y_hbm, gate_ref, x2_ref, g_ref, o_ref, buf, sem, *, final_norm):
    i = pl.program_id(0)
    tc, d = x2_ref.shape
    njp = d // (2 * LANES)
    slot = i % 2

    def issue(dest, s):
        def body(t, c):
            for k in range(TOP_K):
                _token_copy(y_hbm, dest[0, 0, t * TOP_K + k], buf.at[s, k], t, njp, sem.at[s]).start(priority=k % 2)
            return c

        lax.fori_loop(0, tc, body, 0, unroll=2)

    @pl.when(i == 0)
    def _():
        issue(dest_ref, 0)

    @pl.when(i + 1 < pl.num_programs(0))
    def _():
        issue(dest_next_ref, 1 - slot)

    for k in range(TOP_K):
        pltpu.make_async_copy(y_hbm.at[pl.ds(0, tc * njp)], buf.at[slot, k, pl.ds(0, tc * njp)],
                              sem.at[slot]).wait()
    lo_cols, hi_cols = [], []
    for j in range(njp):
        acc_lo = x2_ref[:, j * LANES:(j + 1) * LANES]
        acc_hi = x2_ref[:, (njp + j) * LANES:(njp + j + 1) * LANES]
        for k in range(TOP_K):
            lo, hi = _load_token_cols(buf.at[slot, k], tc, njp, j)
            g = gate_ref[:, k:k + 1]
            acc_lo = acc_lo + g * lo
            acc_hi = acc_hi + g * hi
        lo_cols.append(acc_lo)
        hi_cols.append(acc_hi)
    acc = jnp.concatenate(lo_cols + hi_cols, axis=1)
    if final_norm:
        ms = jnp.mean(acc * acc, axis=-1, keepdims=True)
        acc = acc * lax.rsqrt(ms + RMS_EPS) * g_ref[...]
    o_ref[...] = acc


def _combine(y_tiles, dest, gates, x2, g_final, tc, final_norm):
    t, d = x2.shape
    njp = d // (2 * LANES)
    nt = t // tc
    dest3 = dest.reshape(nt, 1, tc * TOP_K)
    return pl.pallas_call(
        functools.partial(_combine_kernel, final_norm=final_norm),
        out_shape=jax.ShapeDtypeStruct((t, d), F32),
        grid=(nt,),
        in_specs=[
            pl.BlockSpec((1, 1, tc * TOP_K), lambda i: (i, 0, 0), memory_space=pltpu.SMEM),
            pl.BlockSpec((1, 1, tc * TOP_K), lambda i: (jnp.minimum(i + 1, nt - 1), 0, 0),
                         memory_space=pltpu.SMEM),
            pl.BlockSpec(memory_space=pl.ANY),
            pl.BlockSpec((tc, LANES), lambda i: (i, 0)),
            pl.BlockSpec((tc, d), lambda i: (i, 0)),
            pl.BlockSpec((1, d), lambda i: (0, 0)),
        ],
        out_specs=pl.BlockSpec((tc, d), lambda i: (i, 0)),
        scratch_shapes=[pltpu.VMEM((2, TOP_K, tc * _token_pitch(njp), LANES), U32),
                        pltpu.SemaphoreType.DMA((2,))],
        compiler_params=_params(("arbitrary",), disable_bounds_checks=True),
        name="moe_combine_final_norm",
    )(dest3, dest3, y_tiles, gates, x2, g_final.reshape(1, d))


def _layer(tiles, x, norm_mix_g, w_in, b_forget, w_branch_fox, w_branch_moba, w_out, norm_ffn_g,
           w_router, b_router, w_mlp1, b_mlp1, w_mlp2, b_mlp2):
    b, s, d = x.shape
    t = b * s
    width = N_HEADS * HEAD_DIM
    xt = x.reshape(t, d)
    q_scale = LOG2E * HEAD_DIM ** -0.5

    o = np.cumsum([0, width, width, width, N_HEADS, width, width, width, d, d])
    w_fox = w_in[:, o[0]:o[3]].astype(BF16)
    w_fl = jnp.pad(w_in[:, o[3]:o[4]], ((0, 0), (0, LANES - N_HEADS))).astype(BF16)
    w_mqk = _rope_head_order(w_in[:, o[4]:o[6]].astype(BF16))
    w_mv = w_in[:, o[6]:o[7]].astype(BF16)
    w_g = w_in[:, o[7]:o[9]].astype(BF16)
    colscale = jnp.concatenate([jnp.full((1, width), q_scale, F32), jnp.ones((1, 2 * width), F32)], axis=1)

    proj = functools.partial(_proj, seq=s, tm=tiles.proj_rows)
    h, fl = _rmsnorm(xt, norm_mix_g, w_fl, tm=tiles.norm_rows)
    fox = proj(h, w_fox, colscale, tn=tiles.proj_cols, out_dtype=BF16, heads=True, name="proj_fox_qkv")
    mqk = proj(h, w_mqk, colscale[:, :2 * width], tn=tiles.proj_cols, out_dtype=BF16, heads=True,
               rope=_rope_tables(s), name="proj_moba_qk")
    mv = proj(h, w_mv, colscale[:, width:2 * width], tn=tiles.proj_cols, out_dtype=BF16, heads=True,
              name="proj_moba_v")
    gates = proj(h, w_g, jnp.ones((1, 2 * d), F32), tn=tiles.proj_cols, out_dtype=BF16, heads=False,
                 name="proj_gates")

    fl_t = fl[:, :N_HEADS].reshape(b, s, N_HEADS).transpose(0, 2, 1)
    c = _forget_cumsum(fl_t, b_forget)
    c_rows = c.reshape(b, N_HEADS, 1, s)

    o_fox = _attention(fox, 0, fox, N_HEADS, fox, 2 * N_HEADS, c_rows, moba=False, name="fox_attention")
    o_moba = _attention(mqk, 0, mqk, N_HEADS, mv, 0, None, moba=True, name="moba_attention")

    w_r = jnp.pad(w_router, ((0, 0), (0, LANES - N_EXPERTS))).astype(BF16)
    b_r = jnp.concatenate([b_router.astype(F32), jnp.full((LANES - N_EXPERTS,), NEG_INF, F32)]).reshape(1, LANES)
    x2, h2, logits = _mix(o_fox, o_moba, gates, xt, w_branch_fox.astype(BF16), w_branch_moba.astype(BF16),
                          w_out.astype(BF16), norm_ffn_g.reshape(1, d), w_r, b_r, tm=tiles.mix_rows)

    meta, gate_w, cnt = _router(logits, tr=tiles.router_rows)

    m = tiles.moe_rows
    assign = t * TOP_K
    n_blocks = (assign + N_EXPERTS * (m - 1) + m - 1) // m
    counts = cnt[0, :N_EXPERTS].astype(jnp.int32)
    padded = (counts + m - 1) // m * m
    pend = jnp.cumsum(padded)
    pstart = pend - padded
    experts = jnp.arange(N_EXPERTS, dtype=jnp.int32)
    e_sel = meta[:, :TOP_K]
    dest = meta[:, TOP_K:2 * TOP_K] + jnp.sum(jnp.where(e_sel[..., None] == experts, pstart, 0), axis=-1)
    n_used = (pend[-1] // m).astype(jnp.int32).reshape(1)
    first_row = jnp.arange(n_blocks, dtype=jnp.int32) * m
    block_e = jnp.minimum(jnp.sum((pend[None, :] <= first_row[:, None]).astype(jnp.int32), axis=1), N_EXPERTS - 1)
    keys = jnp.sort((e_sel * t + jnp.arange(t, dtype=jnp.int32)[:, None]).reshape(-1))
    start = jnp.cumsum(counts) - counts
    of_block = block_e[:, None] == experts[None, :]
    blk_start = jnp.sum(jnp.where(of_block, start, 0), axis=1)
    blk_count = jnp.sum(jnp.where(of_block, counts, 0), axis=1)
    blk_first = first_row - jnp.sum(jnp.where(of_block, pstart, 0), axis=1)
    in_expert = blk_first[:, None] + jnp.arange(m, dtype=jnp.int32)[None, :]
    src = jnp.clip(blk_start[:, None] + in_expert, 0, assign - 1)
    row_tok = jnp.where((in_expert < blk_count[:, None]).reshape(-1), keys[src.reshape(-1)] % t,
                        jnp.arange(n_blocks * m, dtype=jnp.int32) % t).astype(jnp.int32)

    xb = _gather_rows(h2, row_tok, n_used, m, d // (2 * LANES))
    f = w_mlp1.shape[2] // 2
    b1g = b_mlp1[:, 0::2].reshape(N_EXPERTS, 1, f)
    b1l = b_mlp1[:, 1::2].reshape(N_EXPERTS, 1, f)
    later = (experts[None, :] > experts[:, None]) & (counts[None, :] > 0)
    next_expert = jnp.min(jnp.where(later, experts[None, :], N_EXPERTS), axis=1)
    next_expert = jnp.where(next_expert == N_EXPERTS, -1, next_expert).astype(jnp.int32)
    a = _gmm1(xb, w_mlp1, b1g, b1l, block_e, next_expert, n_used, m, tn=tiles.mlp1_cols)
    y = _gmm2(a, w_mlp2, b_mlp2.reshape(N_EXPERTS, 1, d), block_e, next_expert, n_used, m)
    return x2, y, dest, gate_w


def kernel(x, norm_mix_g, w_in, b_forget, w_branch_fox, w_branch_moba, w_out, norm_ffn_g, w_router, b_router,
           w_mlp1, b_mlp1, w_mlp2, b_mlp2, norm_final_g):
    depth = w_in.shape[0]
    b, s, d = x.shape
    tiles = _tiles(b * s, s, w_mlp1.shape[3] // 2)
    for l in range(depth):
        x2, y, dest, gate_w = _layer(tiles, x, norm_mix_g[l], w_in[l], b_forget[l], w_branch_fox[l],
                                     w_branch_moba[l], w_out[l], norm_ffn_g[l], w_router[l], b_router[l],
                                     w_mlp1[l], b_mlp1[l], w_mlp2[l], b_mlp2[l])
        out = _combine(y, dest, gate_w, x2, norm_final_g, tc=tiles.combine_rows, final_norm=l == depth - 1)
        x = out.reshape(b, s, d)
    return x
```

```python
import functools
import math
from typing import NamedTuple

import jax
import jax.numpy as jnp
import numpy as np
from jax import lax
from jax.experimental import pallas as pl
from jax.experimental.pallas import tpu as pltpu

HEAD_DIM = 128
N_HEADS = 8
MOBA_BLOCK = 256
MOBA_TOPK = 3
ROPE_THETA = 500000.0
ROPE_DIM = HEAD_DIM // 4
N_EXPERTS = 32
TOP_K = 4
SWIGLU_ALPHA = 1.702
SWIGLU_LIMIT = 7.0
RMS_EPS = 1e-5

LANES = 128
SUBLANES = 8
MXU_COLS = 256
VMEM_LIMIT_BYTES = 56 * 1024 * 1024
MOE_ROWS = 512
ATTN_Q_ROWS = 256

F32 = jnp.float32
BF16 = jnp.bfloat16
U32 = jnp.uint32
NEG_INF = float("-inf")
MASKED = -1e30
LOG2E = math.log2(math.e)


def _params(sem, **kw):
    return pltpu.CompilerParams(dimension_semantics=sem, vmem_limit_bytes=VMEM_LIMIT_BYTES, **kw)


class _Tiles(NamedTuple):
    norm_rows: int
    proj_rows: int
    proj_cols: int
    mix_rows: int
    moe_rows: int
    mlp1_cols: int
    combine_rows: int


def _tiles(t, s, f):
    return _Tiles(norm_rows=min(512, t), proj_rows=min(2048, s), proj_cols=1024, mix_rows=min(256, t),
                  moe_rows=MOE_ROWS, mlp1_cols=min(1024, f), combine_rows=min(128, t))


def _pack_pairs(lo, hi):
    lo_b = lax.bitcast_convert_type(lo.astype(BF16).astype(F32), U32) >> 16
    hi_b = lax.bitcast_convert_type(hi.astype(BF16).astype(F32), U32) & jnp.uint32(0xFFFF0000)
    return hi_b | lo_b


def _unpack_pairs(w):
    lo = lax.bitcast_convert_type(w << 16, F32)
    hi = lax.bitcast_convert_type(w & jnp.uint32(0xFFFF0000), F32)
    return lo, hi


def _store_token_tiles(ref, val):
    n, d = val.shape
    njp = d // (2 * LANES)
    for j in range(njp):
        lo = val[:, j * LANES:(j + 1) * LANES]
        hi = val[:, (njp + j) * LANES:(njp + j + 1) * LANES]
        ref[pl.ds(j, n, stride=njp), :] = _pack_pairs(lo, hi)


def _token_pitch(njp):
    p = -(-njp // 4)
    return 4 * (p if p % 2 else p + 1)


def _token_copy(src_hbm, row, dst_vmem, slot, njp, sem):
    return pltpu.make_async_copy(src_hbm.at[pl.ds(pl.multiple_of(row * njp, njp), njp)],
                                 dst_vmem.at[pl.ds(slot * _token_pitch(njp), njp)], sem)


def _load_token_cols(buf, n, njp, j):
    return _unpack_pairs(buf[pl.ds(j, n, stride=_token_pitch(njp)), :])


def _rmsnorm_kernel(x_ref, g_ref, wf_ref, o_ref, fl_ref):
    x = x_ref[...]
    ms = jnp.mean(x * x, axis=-1, keepdims=True)
    h = (x * lax.rsqrt(ms + RMS_EPS) * g_ref[...]).astype(o_ref.dtype)
    o_ref[...] = h
    fl_ref[...] = jnp.dot(h, wf_ref[...], preferred_element_type=F32)


def _rmsnorm(x, g, w_fl, tm):
    t, d = x.shape
    return pl.pallas_call(
        _rmsnorm_kernel,
        out_shape=(jax.ShapeDtypeStruct((t, d), BF16), jax.ShapeDtypeStruct((t, LANES), F32)),
        grid=(t // tm,),
        in_specs=[pl.BlockSpec((tm, d), lambda i: (i, 0)), pl.BlockSpec((1, d), lambda i: (0, 0)),
                  pl.BlockSpec((d, LANES), lambda i: (0, 0))],
        out_specs=(pl.BlockSpec((tm, d), lambda i: (i, 0)), pl.BlockSpec((tm, LANES), lambda i: (i, 0))),
        compiler_params=_params(("parallel",)),
        name="rmsnorm_mix",
    )(x, g.reshape(1, d), w_fl)


def _proj_kernel(a_ref, w_ref, cs_ref, *rest, rotary, heads):
    if rotary:
        cos_ref, sin_ref, o_ref = rest
    else:
        (o_ref,) = rest
    res = jnp.dot(a_ref[...], w_ref[...], preferred_element_type=F32) * cs_ref[...]
    for j in range(res.shape[1] // LANES):
        r = res[:, j * LANES:(j + 1) * LANES]
        if rotary:
            r = r * cos_ref[...] + pltpu.roll(r, LANES // 2, 1) * sin_ref[...]
        if heads:
            o_ref[0, j] = r.astype(o_ref.dtype)
        else:
            o_ref[:, j * LANES:(j + 1) * LANES] = r.astype(o_ref.dtype)


def _proj(a, w, colscale, *, seq, tm, tn, out_dtype, heads, rope=None, name):
    t, k = a.shape
    n = w.shape[1]
    tn = min(tn, n)
    sblk = seq // tm
    in_specs = [
        pl.BlockSpec((tm, k), lambda m, j: (m, 0)),
        pl.BlockSpec((k, tn), lambda m, j: (0, j)),
        pl.BlockSpec((1, tn), lambda m, j: (0, j)),
    ]
    args = [a, w, colscale]
    if rope is not None:
        in_specs += [pl.BlockSpec((tm, LANES), lambda m, j: (m % sblk, 0))] * 2
        args += list(rope)
    if heads:
        out_shape = jax.ShapeDtypeStruct((t // seq, n // LANES, seq, LANES), out_dtype)
        out_spec = pl.BlockSpec((1, tn // LANES, tm, LANES), lambda m, j: (m // sblk, j, m % sblk, 0))
    else:
        out_shape = jax.ShapeDtypeStruct((t, n), out_dtype)
        out_spec = pl.BlockSpec((tm, tn), lambda m, j: (m, j))
    return pl.pallas_call(
        functools.partial(_proj_kernel, rotary=rope is not None, heads=heads),
        out_shape=out_shape,
        grid=(t // tm, n // tn),
        in_specs=in_specs,
        out_specs=out_spec,
        compiler_params=_params(("parallel", "arbitrary")),
        name=name,
    )(*args)


def _rope_head_order(w):
    half = ROPE_DIM // 2
    k, n = w.shape
    w = w.reshape(k, n // HEAD_DIM, HEAD_DIM)
    w = jnp.concatenate([w[..., :half], w[..., 2 * half:LANES // 2 + half], w[..., half:2 * half],
                         w[..., LANES // 2 + half:]], axis=-1)
    return w.reshape(k, n)


def _rope_tables(seq):
    half = ROPE_DIM // 2
    inv_freq = 1.0 / (ROPE_THETA ** (jnp.arange(half, dtype=F32) / half))
    ang = jnp.arange(seq, dtype=F32)[:, None] * inv_freq[None, :]
    cos, sin = jnp.cos(ang), jnp.sin(ang)
    ones = jnp.ones((seq, LANES // 2 - half), F32)
    zeros = jnp.zeros((seq, LANES // 2 - half), F32)
    cos_t = jnp.concatenate([cos, ones, cos, ones], axis=1)
    sin_t = jnp.concatenate([-sin, zeros, sin, zeros], axis=1)
    return cos_t, sin_t


def _forget_kernel(fl_ref, b_ref, c_ref):
    z = fl_ref[...] + b_ref[...]
    logf = jnp.minimum(z, 0.0) - jnp.log1p(jnp.exp(-jnp.abs(z)))
    row = lax.broadcasted_iota(jnp.int32, (LANES, LANES), 0)
    col = lax.broadcasted_iota(jnp.int32, (LANES, LANES), 1)
    tri = (row <= col).astype(BF16)
    carry = jnp.zeros((logf.shape[0], 1), F32)
    for j in range(logf.shape[1] // LANES):
        xb = logf[:, j * LANES:(j + 1) * LANES]
        hi = xb.astype(BF16)
        r1 = xb - hi.astype(F32)
        mid = r1.astype(BF16)
        lo = (r1 - mid.astype(F32)).astype(BF16)
        cs = (jnp.dot(hi, tri, preferred_element_type=F32) + jnp.dot(mid, tri, preferred_element_type=F32)
              + jnp.dot(lo, tri, preferred_element_type=F32)) + carry
        c_ref[:, j * LANES:(j + 1) * LANES] = cs * LOG2E
        carry = cs[:, LANES - 1:LANES]


def _forget_cumsum(fl_t, b_forget):
    b, h, s = fl_t.shape
    return pl.pallas_call(
        _forget_kernel,
        out_shape=jax.ShapeDtypeStruct((b, h, s), F32),
        grid=(b,),
        in_specs=[pl.BlockSpec((None, h, s), lambda i: (i, 0, 0)), pl.BlockSpec((h, 1), lambda i: (0, 0))],
        out_specs=pl.BlockSpec((None, h, s), lambda i: (i, 0, 0)),
        compiler_params=_params(("parallel",)),
        name="forget_cumsum",
    )(fl_t, b_forget.reshape(h, 1).astype(F32))


def _attn_kernel(*refs, moba, tq, tk):
    if moba:
        q_ref, k_ref, v_ref, o_ref, vaug_ref, km_ref, kaug_ref, qaug_ref = refs
    else:
        q_ref, k_ref, v_ref, c_ref, o_ref, vaug_ref = refs
    d = q_ref.shape[1]
    nkb = q_ref.shape[0] // tk
    nt = (((1,), (1,)), ((), ()))
    row = lax.broadcasted_iota(jnp.int32, (tq, tk), 0)
    col = lax.broadcasted_iota(jnp.int32, (tq, tk), 1)

    vaug_ref[:, :d] = v_ref[...]
    vaug_ref[:, d:] = jnp.ones((v_ref.shape[0], LANES), BF16)
    if moba:
        lane = lax.broadcasted_iota(jnp.int32, (tk, LANES), 1)
        km_ref[...] = jnp.zeros_like(km_ref)
        for n in range(nkb):
            kb = k_ref[n * tk:(n + 1) * tk, :]
            km_ref[n:n + 1, :] = jnp.mean(kb.astype(F32), axis=0, keepdims=True)
            kaug_ref[n * tk:(n + 1) * tk, :d] = kb
            kaug_ref[n * tk:(n + 1) * tk, d:] = (lane == n).astype(BF16)
        s_len = q_ref.shape[0]
        g = lax.dot_general(km_ref[...].astype(BF16), q_ref[...], nt, preferred_element_type=F32)[:SUBLANES]
        blk = lax.broadcasted_iota(jnp.int32, (SUBLANES, s_len), 0)
        past = blk * tk + (tk - 1) < lax.broadcasted_iota(jnp.int32, (SUBLANES, s_len), 1)
        pen = jnp.zeros((SUBLANES, s_len), F32)
        for n in range(nkb - 1):
            g_n = g[n:n + 1, :]
            beats = ((g > g_n) | ((g == g_n) & (blk < n))) & past
            rank = jnp.sum(beats.astype(F32), axis=0, keepdims=True)
            pen = jnp.where((blk == n) & past & (rank >= MOBA_TOPK), MASKED, pen)
        pen = jnp.concatenate([pen, jnp.zeros((LANES - SUBLANES, s_len), F32)], axis=0)
        qaug_ref[:, :d] = q_ref[...]
        qaug_ref[:, d:] = pen.T.astype(BF16)

    def logits(q, n):
        s = lax.dot_general(q, (kaug_ref if moba else k_ref)[n * tk:(n + 1) * tk, :], nt,
                            preferred_element_type=F32)
        return s if moba else s - c_ref[:, n * tk:(n + 1) * tk]

    for nb in range(nkb):
        for h in range(tk // tq):
            r0 = nb * tk + h * tq
            q = (qaug_ref if moba else q_ref)[r0:r0 + tq, :]
            s = jnp.where(row + h * tq >= col, logits(q, nb), NEG_INF)
            m = jnp.max(s, axis=1, keepdims=True)
            acc = jnp.dot(jnp.exp2(s - m).astype(BF16), vaug_ref[nb * tk:(nb + 1) * tk, :],
                          preferred_element_type=F32)
            for n in range(nb):
                s = logits(q, n)
                m_new = jnp.maximum(m, jnp.max(s, axis=1, keepdims=True))
                acc = jnp.exp2(m - m_new) * acc + jnp.dot(jnp.exp2(s - m_new).astype(BF16),
                                                          vaug_ref[n * tk:(n + 1) * tk, :],
                                                          preferred_element_type=F32)
                m = m_new
            o_ref[r0:r0 + tq, :] = (acc[:, :d] / acc[:, d:]).astype(o_ref.dtype)


def _attention(q_arr, q_off, k_arr, k_off, v_arr, v_off, c_rows, *, moba, name):
    b, _, s, d = q_arr.shape
    tk = MOBA_BLOCK
    tq = min(ATTN_Q_ROWS, tk)
    assert s % tk == 0 and s // tk <= SUBLANES
    in_specs = [
        pl.BlockSpec((None, None, s, d), lambda bi, h: (bi, q_off + h, 0, 0)),
        pl.BlockSpec((None, None, s, d), lambda bi, h: (bi, k_off + h, 0, 0)),
        pl.BlockSpec((None, None, s, d), lambda bi, h: (bi, v_off + h, 0, 0)),
    ]
    args = [q_arr, k_arr, v_arr]
    scratch = [pltpu.VMEM((s, d + LANES), BF16)]
    if moba:
        scratch += [pltpu.VMEM((LANES, d), F32), pltpu.VMEM((s, d + LANES), BF16), pltpu.VMEM((s, d + LANES), BF16)]
    else:
        in_specs.append(pl.BlockSpec((None, None, 1, s), lambda bi, h: (bi, h, 0, 0)))
        args.append(c_rows)
    return pl.pallas_call(
        functools.partial(_attn_kernel, moba=moba, tq=tq, tk=tk),
        out_shape=jax.ShapeDtypeStruct((b * s, N_HEADS * d), BF16),
        grid=(b, N_HEADS),
        in_specs=in_specs,
        out_specs=pl.BlockSpec((s, d), lambda bi, h: (bi, h)),
        scratch_shapes=scratch,
        compiler_params=_params(("parallel", "parallel")),
        name=name,
    )(*args)


def _mix_kernel(of_ref, om_ref, g_ref, x_ref, wf_ref, wm_ref, wo_ref, gn_ref, wr_ref, br_ref,
                x2_ref, h2_ref, meta_ref, gate_ref, cnt_ref, carry_ref):
    d = x_ref.shape[1]
    yf = jnp.dot(of_ref[...], wf_ref[...], preferred_element_type=F32)
    ym = jnp.dot(om_ref[...], wm_ref[...], preferred_element_type=F32)
    mixed = jax.nn.sigmoid(g_ref[:, :d].astype(F32)) * yf + jax.nn.sigmoid(g_ref[:, d:].astype(F32)) * ym
    x2 = x_ref[...] + jnp.dot(mixed.astype(BF16), wo_ref[...], preferred_element_type=F32)
    x2_ref[...] = x2
    ms = jnp.mean(x2 * x2, axis=-1, keepdims=True)
    h2 = x2 * lax.rsqrt(ms + RMS_EPS) * gn_ref[...]
    _store_token_tiles(h2_ref, h2)
    logits = jnp.dot(h2.astype(BF16), wr_ref[...], preferred_element_type=F32) + br_ref[...]
    _route(logits, meta_ref, gate_ref, cnt_ref, carry_ref)


def _mix(o_fox, o_moba, gates, x, w_f, w_m, w_o, g_ffn, w_r, b_r, tm):
    t, d = x.shape
    w = o_fox.shape[1]
    njp = d // (2 * LANES)
    const = lambda i: (0, 0)
    resident = dict(pipeline_mode=pl.Buffered(1))
    return pl.pallas_call(
        _mix_kernel,
        out_shape=(jax.ShapeDtypeStruct((t, d), F32), jax.ShapeDtypeStruct((t * njp, LANES), U32),
                   jax.ShapeDtypeStruct((t, LANES), jnp.int32), jax.ShapeDtypeStruct((t, LANES), F32),
                   jax.ShapeDtypeStruct((1, LANES), F32)),
        grid=(t // tm,),
        in_specs=[
            pl.BlockSpec((tm, w), lambda i: (i, 0)),
            pl.BlockSpec((tm, w), lambda i: (i, 0)),
            pl.BlockSpec((tm, 2 * d), lambda i: (i, 0)),
            pl.BlockSpec((tm, d), lambda i: (i, 0)),
            pl.BlockSpec((w, d), const, **resident),
            pl.BlockSpec((w, d), const, **resident),
            pl.BlockSpec((d, d), const, **resident),
            pl.BlockSpec((1, d), const),
            pl.BlockSpec((d, LANES), const),
            pl.BlockSpec((1, LANES), const),
        ],
        out_specs=(pl.BlockSpec((tm, d), lambda i: (i, 0)), pl.BlockSpec((tm * njp, LANES), lambda i: (i, 0)),
                   pl.BlockSpec((tm, LANES), lambda i: (i, 0)), pl.BlockSpec((tm, LANES), lambda i: (i, 0)),
                   pl.BlockSpec((1, LANES), lambda i: (0, 0))),
        scratch_shapes=[pltpu.VMEM((1, LANES), F32)],
        compiler_params=_params(("arbitrary",)),
        name="mix_out_router",
    )(o_fox, o_moba, gates, x, w_f, w_m, w_o, g_ffn, w_r, b_r)


def _route(l, meta_ref, gate_ref, cnt_ref, carry_ref):
    @pl.when(pl.program_id(0) == 0)
    def _():
        carry_ref[...] = jnp.zeros_like(carry_ref)

    tr = l.shape[0]
    lane = lax.broadcasted_iota(jnp.int32, (tr, LANES), 1).astype(F32)
    vals, idxs = [], []
    for _ in range(TOP_K):
        m = jnp.max(l, axis=1, keepdims=True)
        ix = jnp.min(jnp.where(l == m, lane, float(LANES)), axis=1, keepdims=True)
        vals.append(m)
        idxs.append(ix)
        l = jnp.where(lane == ix, NEG_INF, l)
    ex = [jnp.exp(v - vals[0]) for v in vals]
    den = ex[0] + ex[1] + ex[2] + ex[3]
    onehot = jnp.zeros((tr, LANES), F32)
    for ix in idxs:
        onehot = onehot + (lane == ix).astype(F32)
    row = lax.broadcasted_iota(jnp.int32, (tr, tr), 0)
    col = lax.broadcasted_iota(jnp.int32, (tr, tr), 1)
    before = (col < row).astype(BF16)
    running = jnp.dot(before, onehot.astype(BF16), preferred_element_type=F32) + carry_ref[...]
    meta = jnp.zeros((tr, LANES), F32)
    gate = jnp.zeros((tr, LANES), F32)
    for k in range(TOP_K):
        pos = jnp.sum(jnp.where(lane == idxs[k], running, 0.0), axis=1, keepdims=True)
        meta = jnp.where(lane == float(k), idxs[k], meta)
        meta = jnp.where(lane == float(TOP_K + k), pos, meta)
        gate = jnp.where(lane == float(k), ex[k] / den, gate)
    meta_ref[...] = meta.astype(jnp.int32)
    gate_ref[...] = gate
    carry_ref[...] += jnp.sum(onehot, axis=0, keepdims=True)
    cnt_ref[...] = carry_ref[...]


def _gather_kernel(nused_ref, tok_ref, tok_next_ref, h_hbm, o_ref, buf, sem, *, njp):
    r = pl.program_id(0)
    m = o_ref.shape[0]
    slot = r % 2

    def issue(tok, s):
        def body(i, c):
            _token_copy(h_hbm, tok[0, 0, 2 * i], buf.at[s], 2 * i, njp, sem.at[s]).start(priority=0)
            _token_copy(h_hbm, tok[0, 0, 2 * i + 1], buf.at[s], 2 * i + 1, njp, sem.at[s]).start(priority=1)
            return c

        lax.fori_loop(0, m // 2, body, 0, unroll=4)

    @pl.when(r == 0)
    def _():
        issue(tok_ref, 0)

    @pl.when(r + 1 < nused_ref[0])
    def _():
        issue(tok_next_ref, 1 - slot)

    @pl.when(r < nused_ref[0])
    def _():
        pltpu.make_async_copy(h_hbm.at[pl.ds(0, m * njp)], buf.at[slot, pl.ds(0, m * njp)], sem.at[slot]).wait()
        for j in range(njp):
            lo, hi = _load_token_cols(buf.at[slot], m, njp, j)
            o_ref[:, j * LANES:(j + 1) * LANES] = lo.astype(o_ref.dtype)
            o_ref[:, (njp + j) * LANES:(njp + j + 1) * LANES] = hi.astype(o_ref.dtype)

    @pl.when(r >= nused_ref[0])
    def _():
        o_ref[...] = jnp.zeros_like(o_ref)


def _gather_rows(h_tiles, row_tok, n_used, m, njp):
    n_blocks = row_tok.shape[0] // m
    tok3 = row_tok.reshape(n_blocks, 1, m)
    return pl.pallas_call(
        functools.partial(_gather_kernel, njp=njp),
        out_shape=jax.ShapeDtypeStruct((n_blocks * m, 2 * njp * LANES), BF16),
        grid_spec=pltpu.PrefetchScalarGridSpec(
            num_scalar_prefetch=1,
            grid=(n_blocks,),
            in_specs=[
                pl.BlockSpec((1, 1, m), lambda r, nu: (r, 0, 0), memory_space=pltpu.SMEM),
                pl.BlockSpec((1, 1, m), lambda r, nu: (jnp.minimum(r + 1, n_blocks - 1), 0, 0),
                             memory_space=pltpu.SMEM),
                pl.BlockSpec(memory_space=pl.ANY),
            ],
            out_specs=pl.BlockSpec((m, 2 * njp * LANES), lambda r, nu: (r, 0)),
            scratch_shapes=[pltpu.VMEM((2, m * _token_pitch(njp), LANES), U32), pltpu.SemaphoreType.DMA((2,))],
        ),
        compiler_params=_params(("arbitrary",), disable_bounds_checks=True),
        name="expert_row_gather",
    )(n_used, tok3, tok3, h_tiles)


def _group_start(be_ref, r):
    return jnp.logical_or(r == 0, be_ref[r] != be_ref[jnp.maximum(r - 1, 0)])


def _gmm1_kernel(be_ref, nne_ref, nused_ref, x_ref, w_hbm, perm_ref, bg_ref, bl_ref, a_ref,
                 stage, wg_s, wl_s, sem):
    j = pl.program_id(0)
    r = pl.program_id(1)
    pw = perm_ref.shape[0]
    tw = stage.shape[1]

    def fetch(jj, e):
        return pltpu.make_async_copy(w_hbm.at[e, :, pl.ds(pl.multiple_of(jj * tw, tw), tw)], stage, sem)

    @pl.when((j == 0) & (r == 0))
    def _():
        fetch(0, be_ref[0]).start()

    @pl.when(r < nused_ref[0])
    def _():
        @pl.when(_group_start(be_ref, r))
        def _():
            fetch(j, be_ref[r]).wait()
            for c in range(tw // pw):
                wb = stage[:, c * pw:(c + 1) * pw].astype(BF16)
                sp = jnp.dot(wb, perm_ref[...], preferred_element_type=F32).astype(BF16)
                wg_s[:, c * (pw // 2):(c + 1) * (pw // 2)] = sp[:, :pw // 2]
                wl_s[:, c * (pw // 2):(c + 1) * (pw // 2)] = sp[:, pw // 2:]
            nxt = nne_ref[be_ref[r]]

            @pl.when(nxt >= 0)
            def _():
                fetch(j, nxt).start()

            @pl.when((nxt < 0) & (j + 1 < pl.num_programs(0)))
            def _():
                fetch(j + 1, be_ref[0]).start()

        x = x_ref[...]
        ug = jnp.dot(x, wg_s[...], preferred_element_type=F32) + bg_ref[...]
        ul = jnp.dot(x, wl_s[...], preferred_element_type=F32) + bl_ref[...]
        glu = jnp.minimum(ug, SWIGLU_LIMIT)
        lin = jnp.clip(ul, -SWIGLU_LIMIT, SWIGLU_LIMIT)
        a_ref[...] = (glu * jax.nn.sigmoid(SWIGLU_ALPHA * glu) * (lin + 1.0)).astype(a_ref.dtype)

    @pl.when(r >= nused_ref[0])
    def _():
        a_ref[...] = jnp.zeros_like(a_ref)


def _deinterleave_perm(width):
    src = np.arange(width)
    dst = np.where(src % 2 == 0, src // 2, width // 2 + src // 2)
    p = np.zeros((width, width), np.float32)
    p[src, dst] = 1.0
    return jnp.asarray(p, BF16)


def _gmm1(xb, w1, b1g, b1l, block_e, next_expert, n_used, m, tn):
    rows, d = xb.shape
    f = w1.shape[2] // 2
    n_blocks = rows // m
    pw = 2 * LANES

    def rc(r, nu):
        return jnp.minimum(r, nu[0] - 1)

    return pl.pallas_call(
        _gmm1_kernel,
        out_shape=jax.ShapeDtypeStruct((rows, f), BF16),
        grid_spec=pltpu.PrefetchScalarGridSpec(
            num_scalar_prefetch=3,
            grid=(f // tn, n_blocks),
            in_specs=[
                pl.BlockSpec((m, d), lambda j, r, be, nne, nu: (rc(r, nu), 0)),
                pl.BlockSpec(memory_space=pl.ANY),
                pl.BlockSpec((pw, pw), lambda j, r, be, nne, nu: (0, 0)),
                pl.BlockSpec((None, 1, tn), lambda j, r, be, nne, nu: (be[rc(r, nu)], 0, j)),
                pl.BlockSpec((None, 1, tn), lambda j, r, be, nne, nu: (be[rc(r, nu)], 0, j)),
            ],
            out_specs=pl.BlockSpec((m, tn), lambda j, r, be, nne, nu: (r, j)),
            scratch_shapes=[pltpu.VMEM((d, 2 * tn), F32), pltpu.VMEM((d, tn), BF16), pltpu.VMEM((d, tn), BF16),
                            pltpu.SemaphoreType.DMA(())],
        ),
        compiler_params=_params(("arbitrary", "arbitrary")),
        name="expert_mlp1_swiglu",
    )(block_e, next_expert, n_used, xb, w1, _deinterleave_perm(pw), b1g, b1l)


def _gmm2_kernel(be_ref, nne_ref, nused_ref, a_ref, w_hbm, b_ref, y_ref, stage, w_s, sem):
    r = pl.program_id(0)

    def fetch(e):
        return pltpu.make_async_copy(w_hbm.at[e], stage, sem)

    @pl.when(r == 0)
    def _():
        fetch(be_ref[0]).start()

    @pl.when(r < nused_ref[0])
    def _():
        @pl.when(_group_start(be_ref, r))
        def _():
            fetch(be_ref[r]).wait()
            w_s[...] = stage[...].astype(BF16)
            nxt = nne_ref[be_ref[r]]

            @pl.when(nxt >= 0)
            def _():
                fetch(nxt).start()

        a = a_ref[...]
        n = a.shape[0]
        half = w_s.shape[1] // 2
        njp = half // LANES
        cw = min(MXU_COLS, half)
        for c in range(half // cw):
            lo_cols = slice(c * cw, (c + 1) * cw)
            hi_cols = slice(half + c * cw, half + (c + 1) * cw)
            y_lo = jnp.dot(a, w_s[:, lo_cols], preferred_element_type=F32) + b_ref[:, lo_cols]
            y_hi = jnp.dot(a, w_s[:, hi_cols], preferred_element_type=F32) + b_ref[:, hi_cols]
            for jj in range(cw // LANES):
                lanes = slice(jj * LANES, (jj + 1) * LANES)
                y_ref[pl.ds(c * (cw // LANES) + jj, n, stride=njp), :] = _pack_pairs(y_lo[:, lanes], y_hi[:, lanes])

    @pl.when(r >= nused_ref[0])
    def _():
        y_ref[...] = jnp.zeros_like(y_ref)


def _gmm2(a, w2, b2, block_e, next_expert, n_used, m):
    rows, f = a.shape
    d = w2.shape[2]
    njp = d // (2 * LANES)
    n_blocks = rows // m

    def rc(r, nu):
        return jnp.minimum(r, nu[0] - 1)

    return pl.pallas_call(
        _gmm2_kernel,
        out_shape=jax.ShapeDtypeStruct((rows * njp, LANES), U32),
        grid_spec=pltpu.PrefetchScalarGridSpec(
            num_scalar_prefetch=3,
            grid=(n_blocks,),
            in_specs=[
                pl.BlockSpec((m, f), lambda r, be, nne, nu: (rc(r, nu), 0)),
                pl.BlockSpec(memory_space=pl.ANY),
                pl.BlockSpec((None, 1, d), lambda r, be, nne, nu: (be[rc(r, nu)], 0, 0)),
            ],
            out_specs=pl.BlockSpec((m * njp, LANES), lambda r, be, nne, nu: (r, 0)),
            scratch_shapes=[pltpu.VMEM((f, d), F32), pltpu.VMEM((f, d), BF16), pltpu.SemaphoreType.DMA(())],
        ),
        compiler_params=_params(("arbitrary",)),
        name="expert_mlp2",
    )(block_e, next_expert, n_used, a, w2, b2)


def _combine_kernel(dest_ref, dest_next_ref, y_hbm, gate_ref, x2_ref, g_ref, o_ref, buf, sem, *, final_norm):
    i = pl.program_id(0)
    tc, d = x2_ref.shape
    njp = d // (2 * LANES)
    slot = i % 2

    def issue(dest, s):
        def body(t, c):
            for k in range(TOP_K):
                _token_copy(y_hbm, dest[0, 0, t * TOP_K + k], buf.at[s, k], t, njp, sem.at[s]).start(priority=k % 2)
            return c

        lax.fori_loop(0, tc, body, 0, unroll=2)

    @pl.when(i == 0)
    def _():
        issue(dest_ref, 0)

    @pl.when(i + 1 < pl.num_programs(0))
    def _():
        issue(dest_next_ref, 1 - slot)

    for k in range(TOP_K):
        pltpu.make_async_copy(y_hbm.at[pl.ds(0, tc * njp)], buf.at[slot, k, pl.ds(0, tc * njp)],
                              sem.at[slot]).wait()
    lo_cols, hi_cols = [], []
    for j in range(njp):
        acc_lo = x2_ref[:, j * LANES:(j + 1) * LANES]
        acc_hi = x2_ref[:, (njp + j) * LANES:(njp + j + 1) * LANES]
        for k in range(TOP_K):
            lo, hi = _load_token_cols(buf.at[slot, k], tc, njp, j)
            g = gate_ref[:, k:k + 1]
            acc_lo = acc_lo + g * lo
            acc_hi = acc_hi + g * hi
        lo_cols.append(acc_lo)
        hi_cols.append(acc_hi)
    acc = jnp.concatenate(lo_cols + hi_cols, axis=1)
    if final_norm:
        ms = jnp.mean(acc * acc, axis=-1, keepdims=True)
        acc = acc * lax.rsqrt(ms + RMS_EPS) * g_ref[...]
    o_ref[...] = acc


def _combine(y_tiles, dest, gates, x2, g_final, tc, final_norm):
    t, d = x2.shape
    njp = d // (2 * LANES)
    nt = t // tc
    dest3 = dest.reshape(nt, 1, tc * TOP_K)
    return pl.pallas_call(
        functools.partial(_combine_kernel, final_norm=final_norm),
        out_shape=jax.ShapeDtypeStruct((t, d), F32),
        grid=(nt,),
        in_specs=[
            pl.BlockSpec((1, 1, tc * TOP_K), lambda i: (i, 0, 0), memory_space=pltpu.SMEM),
            pl.BlockSpec((1, 1, tc * TOP_K), lambda i: (jnp.minimum(i + 1, nt - 1), 0, 0),
                         memory_space=pltpu.SMEM),
            pl.BlockSpec(memory_space=pl.ANY),
            pl.BlockSpec((tc, LANES), lambda i: (i, 0)),
            pl.BlockSpec((tc, d), lambda i: (i, 0)),
            pl.BlockSpec((1, d), lambda i: (0, 0)),
        ],
        out_specs=pl.BlockSpec((tc, d), lambda i: (i, 0)),
        scratch_shapes=[pltpu.VMEM((2, TOP_K, tc * _token_pitch(njp), LANES), U32),
                        pltpu.SemaphoreType.DMA((2,))],
        compiler_params=_params(("arbitrary",), disable_bounds_checks=True),
        name="moe_combine_final_norm",
    )(dest3, dest3, y_tiles, gates, x2, g_final.reshape(1, d))


def _layer(tiles, x, norm_mix_g, w_in, b_forget, w_branch_fox, w_branch_moba, w_out, norm_ffn_g,
           w_router, b_router, w_mlp1, b_mlp1, w_mlp2, b_mlp2):
    b, s, d = x.shape
    t = b * s
    width = N_HEADS * HEAD_DIM
    xt = x.reshape(t, d)
    q_scale = LOG2E * HEAD_DIM ** -0.5

    o = np.cumsum([0, width, width, width, N_HEADS, width, width, width, d, d])
    w_fox = w_in[:, o[0]:o[3]].astype(BF16)
    w_fl = jnp.pad(w_in[:, o[3]:o[4]], ((0, 0), (0, LANES - N_HEADS))).astype(BF16)
    w_mqk = _rope_head_order(w_in[:, o[4]:o[6]].astype(BF16))
    w_mv = w_in[:, o[6]:o[7]].astype(BF16)
    w_g = w_in[:, o[7]:o[9]].astype(BF16)
    colscale = jnp.concatenate([jnp.full((1, width), q_scale, F32), jnp.ones((1, 2 * width), F32)], axis=1)

    proj = functools.partial(_proj, seq=s, tm=tiles.proj_rows)
    h, fl = _rmsnorm(xt, norm_mix_g, w_fl, tm=tiles.norm_rows)
    fox = proj(h, w_fox, colscale, tn=tiles.proj_cols, out_dtype=BF16, heads=True, name="proj_fox_qkv")
    mqk = proj(h, w_mqk, colscale[:, :2 * width], tn=tiles.proj_cols, out_dtype=BF16, heads=True,
               rope=_rope_tables(s), name="proj_moba_qk")
    mv = proj(h, w_mv, colscale[:, width:2 * width], tn=tiles.proj_cols, out_dtype=BF16, heads=True,
              name="proj_moba_v")
    gates = proj(h, w_g, jnp.ones((1, 2 * d), F32), tn=tiles.proj_cols, out_dtype=BF16, heads=False,
                 name="proj_gates")

    fl_t = fl[:, :N_HEADS].reshape(b, s, N_HEADS).transpose(0, 2, 1)
    c = _forget_cumsum(fl_t, b_forget)
    c_rows = c.reshape(b, N_HEADS, 1, s)

    o_fox = _attention(fox, 0, fox, N_HEADS, fox, 2 * N_HEADS, c_rows, moba=False, name="fox_attention")
    o_moba = _attention(mqk, 0, mqk, N_HEADS, mv, 0, None, moba=True, name="moba_attention")

    w_r = jnp.pad(w_router, ((0, 0), (0, LANES - N_EXPERTS))).astype(BF16)
    b_r = jnp.concatenate([b_router.astype(F32), jnp.full((LANES - N_EXPERTS,), NEG_INF, F32)]).reshape(1, LANES)
    x2, h2, meta, gate_w, cnt = _mix(o_fox, o_moba, gates, xt, w_branch_fox.astype(BF16),
                                     w_branch_moba.astype(BF16), w_out.astype(BF16), norm_ffn_g.reshape(1, d),
                                     w_r, b_r, tm=tiles.mix_rows)

    m = tiles.moe_rows
    assign = t * TOP_K
    n_blocks = (assign + N_EXPERTS * (m - 1) + m - 1) // m
    counts = cnt[0, :N_EXPERTS].astype(jnp.int32)
    padded = (counts + m - 1) // m * m
    pend = jnp.cumsum(padded)
    pstart = pend - padded
    experts = jnp.arange(N_EXPERTS, dtype=jnp.int32)
    e_sel = meta[:, :TOP_K]
    dest = meta[:, TOP_K:2 * TOP_K] + jnp.sum(jnp.where(e_sel[..., None] == experts, pstart, 0), axis=-1)
    n_used = (pend[-1] // m).astype(jnp.int32).reshape(1)
    first_row = jnp.arange(n_blocks, dtype=jnp.int32) * m
    block_e = jnp.minimum(jnp.sum((pend[None, :] <= first_row[:, None]).astype(jnp.int32), axis=1), N_EXPERTS - 1)
    keys = jnp.sort((e_sel * t + jnp.arange(t, dtype=jnp.int32)[:, None]).reshape(-1))
    start = jnp.cumsum(counts) - counts
    of_block = block_e[:, None] == experts[None, :]
    blk_start = jnp.sum(jnp.where(of_block, start, 0), axis=1)
    blk_count = jnp.sum(jnp.where(of_block, counts, 0), axis=1)
    blk_first = first_row - jnp.sum(jnp.where(of_block, pstart, 0), axis=1)
    in_expert = blk_first[:, None] + jnp.arange(m, dtype=jnp.int32)[None, :]
    src = jnp.clip(blk_start[:, None] + in_expert, 0, assign - 1)
    row_tok = jnp.where((in_expert < blk_count[:, None]).reshape(-1), keys[src.reshape(-1)] % t,
                        jnp.arange(n_blocks * m, dtype=jnp.int32) % t).astype(jnp.int32)

    xb = _gather_rows(h2, row_tok, n_used, m, d // (2 * LANES))
    f = w_mlp1.shape[2] // 2
    b1g = b_mlp1[:, 0::2].reshape(N_EXPERTS, 1, f)
    b1l = b_mlp1[:, 1::2].reshape(N_EXPERTS, 1, f)
    later = (experts[None, :] > experts[:, None]) & (counts[None, :] > 0)
    next_expert = jnp.min(jnp.where(later, experts[None, :], N_EXPERTS), axis=1)
    next_expert = jnp.where(next_expert == N_EXPERTS, -1, next_expert).astype(jnp.int32)
    a = _gmm1(xb, w_mlp1, b1g, b1l, block_e, next_expert, n_used, m, tn=tiles.mlp1_cols)
    y = _gmm2(a, w_mlp2, b_mlp2.reshape(N_EXPERTS, 1, d), block_e, next_expert, n_used, m)
    return x2, y, dest, gate_w


def kernel(x, norm_mix_g, w_in, b_forget, w_branch_fox, w_branch_moba, w_out, norm_ffn_g, w_router, b_router,
           w_mlp1, b_mlp1, w_mlp2, b_mlp2, norm_final_g):
    depth = w_in.shape[0]
    b, s, d = x.shape
    tiles = _tiles(b * s, s, w_mlp1.shape[3] // 2)
    for l in range(depth):
        x2, y, dest, gate_w = _layer(tiles, x, norm_mix_g[l], w_in[l], b_forget[l], w_branch_fox[l],
                                     w_branch_moba[l], w_out[l], norm_ffn_g[l], w_router[l], b_router[l],
                                     w_mlp1[l], b_mlp1[l], w_mlp2[l], b_mlp2[l])
        out = _combine(y, dest, gate_w, x2, norm_final_g, tc=tiles.combine_rows, final_norm=l == depth - 1)
        x = out.reshape(b, s, d)
    return x
```
